```python
import jax, jax.numpy as jnp
from jax import lax
import numpy as np

D_MODEL = 1024
BATCH = 8
SEQ = 8192
DEPTH = 4

D_MIX = D_MODEL
HEAD_DIM = 64
A_HEADS = 6
A_WIDTH = A_HEADS * HEAD_DIM
A_ROT_DIM = HEAD_DIM // 4
DILATED_CONFIGS = ((128, 1), (512, 4), (2048, 16))
WIN_BLOCK = 128
B_HEADS = 6
B_NOPE = 64
B_ROPE = 32
B_V = 64
B_WIDTH = B_HEADS * B_V
Q_LORA = 384
KV_LORA = 128
Q_BLOCK = 128
C_GROUPS = 4
C_GROUP_DIM = 64
C_WIDTH = C_GROUPS * C_GROUP_DIM
D_IN = 3 * A_WIDTH + Q_LORA + KV_LORA + B_ROPE + C_WIDTH
ROPE_THETA = 500000.0
D_FF = 2816
N_EXPERTS = 8
TOP_K = 2
D_FF_EXPERT = 3584
MOE_BLOCK = 512
N_DENSE = (DEPTH + 1) // 2
N_MOE = DEPTH // 2
RMS_EPS = 1e-6
NEG_INF = -1e30

kernel_name = "hybrid_dilated_mla_fnet_moe_encoder"


def _rmsnorm(x, g):
    xf = x.astype(jnp.float32)
    y = xf * lax.rsqrt(jnp.mean(xf * xf, axis=-1, keepdims=True) + RMS_EPS)
    return (y * g.astype(jnp.float32)).astype(x.dtype)


def _rope(x, rot_dim):
    seq = x.shape[1]
    half = rot_dim // 2
    inv_freq = ROPE_THETA ** (-jnp.arange(half, dtype=jnp.float32) / half)
    ang = jnp.arange(seq, dtype=jnp.float32)[:, None] * inv_freq[None, :]
    cos = jnp.cos(ang)[None, :, None, :]
    sin = jnp.sin(ang)[None, :, None, :]
    xr = x[..., :rot_dim].astype(jnp.float32)
    x1, x2 = xr[..., :half], xr[..., half:]
    rot = jnp.concatenate([x1 * cos - x2 * sin, x2 * cos + x1 * sin], axis=-1).astype(x.dtype)
    return jnp.concatenate([rot, x[..., rot_dim:]], axis=-1)


def _strided_window_attention(q, k, v, dilation, radius):
    bsz, seq, heads, dh = q.shape
    length = seq // dilation
    n_blk = -(-length // WIN_BLOCK)
    lp = n_blk * WIN_BLOCK
    span = WIN_BLOCK + 2 * radius

    def residues(t):
        return t.reshape(bsz, length, dilation, heads, dh).transpose(0, 2, 1, 3, 4)

    qr = jnp.pad(residues(q), ((0, 0), (0, 0), (0, lp - length), (0, 0), (0, 0)))
    kpad = ((0, 0), (0, 0), (radius, lp - length + radius), (0, 0), (0, 0))
    kr = jnp.pad(residues(k), kpad)
    vr = jnp.pad(residues(v), kpad)
    idx = jnp.arange(n_blk)[:, None] * WIN_BLOCK + jnp.arange(span)[None, :]
    kb = kr[:, :, idx]
    vb = vr[:, :, idx]
    qb = qr.reshape(bsz, dilation, n_blk, WIN_BLOCK, heads, dh)
    s = jnp.einsum('bdnqhc,bdnkhc->bdnhqk', qb, kb).astype(jnp.float32) * (dh ** -0.5)
    qpos = jnp.arange(n_blk)[:, None] * WIN_BLOCK + jnp.arange(WIN_BLOCK)[None, :]
    kpos = idx - radius
    rel = kpos[:, None, :] - qpos[:, :, None]
    valid = (jnp.abs(rel) <= radius) & (kpos >= 0)[:, None, :] & (kpos < length)[:, None, :]
    s = jnp.where(valid[None, None, :, None], s, NEG_INF)
    m = jnp.max(s, axis=-1, keepdims=True)
    p = jnp.exp(s - m)
    den = jnp.sum(p, axis=-1, keepdims=True)
    o = jnp.einsum('bdnhqk,bdnkhc->bdnqhc', (p / den).astype(v.dtype), vb)
    lse = (m + jnp.log(den))[..., 0]
    o = o.reshape(bsz, dilation, lp, heads, dh)[:, :, :length]
    lse = lse.transpose(0, 1, 2, 4, 3).reshape(bsz, dilation, lp, heads)[:, :, :length]
    o = o.transpose(0, 2, 1, 3, 4).reshape(bsz, seq, heads, dh)
    lse = lse.transpose(0, 2, 1, 3).reshape(bsz, seq, heads)
    return o, lse


def _dilated_attention(q, k, v):
    outs, lses = [], []
    for window, dilation in DILATED_CONFIGS:
        o, lse = _strided_window_attention(q, k, v, dilation, (window // 2) // dilation)
        outs.append(o)
        lses.append(lse)
    w = jax.nn.softmax(jnp.stack(lses, axis=0), axis=0)
    o = jnp.sum(jnp.stack(outs, axis=0).astype(jnp.float32) * w[..., None], axis=0)
    return o.astype(q.dtype)


def _block_attention(q, k, v):
    bsz, seq, heads, dq = q.shape
    dv = v.shape[-1]
    nq = seq // Q_BLOCK
    qb = q.reshape(bsz, nq, Q_BLOCK, heads, dq).transpose(1, 0, 2, 3, 4)
    scale = dq ** -0.5

    def one_block(qblk):
        s = jnp.einsum('bqhc,bkhc->bhqk', qblk, k).astype(jnp.float32) * scale
        p = jax.nn.softmax(s, axis=-1)
        return jnp.einsum('bhqk,bkhc->bqhc', p.astype(v.dtype), v)

    o = lax.map(one_block, qb)
    return o.transpose(1, 0, 2, 3, 4).reshape(bsz, seq, heads, dv)


def _fourier_mix(f):
    bsz, seq, _ = f.shape
    fg = f.reshape(bsz, seq, C_GROUPS, C_GROUP_DIM).astype(jnp.float32)
    y = jnp.fft.fft2(fg, axes=(1, 3), norm='ortho').real
    return y.reshape(bsz, seq, C_WIDTH).astype(f.dtype)


def _swiglu(h, wg, wu, wd):
    return (jax.nn.silu(h @ wg) * (h @ wu)) @ wd


def _moe_swiglu(h, w_router, w_gate, w_up, w_down):
    n_tok, dm = h.shape
    logits = h.astype(jnp.float32) @ w_router.astype(jnp.float32)
    top_val, top_idx = lax.top_k(logits, TOP_K)
    gates = jax.nn.softmax(top_val, axis=-1)
    n_assign = n_tok * TOP_K
    flat_e = top_idx.reshape(-1)
    flat_tok = jnp.repeat(jnp.arange(n_tok, dtype=jnp.int32), TOP_K)
    order = jnp.argsort(flat_e)
    e_sorted = flat_e[order]
    tok_sorted = flat_tok[order]
    gate_sorted = gates.reshape(-1)[order]
    counts = jnp.bincount(flat_e, length=N_EXPERTS)
    padded = ((counts + MOE_BLOCK - 1) // MOE_BLOCK) * MOE_BLOCK
    start = jnp.cumsum(counts) - counts
    cum_padded = jnp.cumsum(padded)
    start_padded = cum_padded - padded
    rank = jnp.arange(n_assign, dtype=jnp.int32) - start[e_sorted]
    dest = start_padded[e_sorted] + rank
    n_blocks = -(-n_assign // MOE_BLOCK) + N_EXPERTS
    n_rows = n_blocks * MOE_BLOCK
    tok_pad = jnp.full((n_rows,), n_tok, dtype=jnp.int32).at[dest].set(tok_sorted)
    h_ext = jnp.concatenate([h, jnp.zeros((1, dm), h.dtype)], axis=0)
    xb = h_ext[tok_pad].reshape(n_blocks, MOE_BLOCK, dm)
    block_start = jnp.arange(n_blocks, dtype=jnp.int32) * MOE_BLOCK
    block_expert = jnp.minimum(
        jnp.sum(block_start[:, None] >= cum_padded[None, :], axis=1), N_EXPERTS - 1)

    def one_block(args):
        xe, e = args
        return _swiglu(xe, w_gate[e], w_up[e], w_down[e])

    yb = lax.map(one_block, (xb, block_expert)).reshape(n_rows, dm)
    y_assign = yb[dest] * gate_sorted[:, None].astype(h.dtype)
    return jax.ops.segment_sum(y_assign, tok_sorted, num_segments=n_tok)


def setup_inputs(seed: int = 0) -> dict:
    key = jax.random.key(seed)
    ks = jax.random.split(key, 18)
    f32 = jnp.float32

    def nrm(k, shape, fan_in):
        return jax.random.normal(k, shape, f32) * (fan_in ** -0.5)

    def gain(k, shape):
        return 1.0 + 0.02 * jax.random.normal(k, shape, f32)

    return {
        'x': jax.random.normal(ks[0], (BATCH, SEQ, D_MODEL), f32),
        'attn_norm': gain(ks[1], (DEPTH, D_MODEL)),
        'w_in': nrm(ks[2], (DEPTH, D_MODEL, D_IN), D_MODEL),
        'q_norm': gain(ks[3], (DEPTH, Q_LORA)),
        'w_uq': nrm(ks[4], (DEPTH, Q_LORA, B_HEADS * (B_NOPE + B_ROPE)), Q_LORA),
        'kv_norm': gain(ks[5], (DEPTH, KV_LORA)),
        'w_ukv': nrm(ks[6], (DEPTH, KV_LORA, B_HEADS * (B_NOPE + B_V)), KV_LORA),
        'mix_gain': gain(ks[7], (DEPTH, D_MIX)),
        'w_out': nrm(ks[8], (DEPTH, D_MIX, D_MODEL), D_MIX),
        'ffn_norm': gain(ks[9], (DEPTH, D_MODEL)),
        'w_ffn_gate': nrm(ks[10], (N_DENSE, D_MODEL, D_FF), D_MODEL),
        'w_ffn_up': nrm(ks[11], (N_DENSE, D_MODEL, D_FF), D_MODEL),
        'w_ffn_down': nrm(ks[12], (N_DENSE, D_FF, D_MODEL), D_FF),
        'w_router': nrm(ks[13], (N_MOE, D_MODEL, N_EXPERTS), D_MODEL),
        'w_exp_gate': nrm(ks[14], (N_MOE, N_EXPERTS, D_MODEL, D_FF_EXPERT), D_MODEL),
        'w_exp_up': nrm(ks[15], (N_MOE, N_EXPERTS, D_MODEL, D_FF_EXPERT), D_MODEL),
        'w_exp_down': nrm(ks[16], (N_MOE, N_EXPERTS, D_FF_EXPERT, D_MODEL), D_FF_EXPERT),
        'final_norm': gain(ks[17], (D_MODEL,)),
    }


def reference(x, attn_norm, w_in, q_norm, w_uq, kv_norm, w_ukv, mix_gain, w_out, ffn_norm,
              w_ffn_gate, w_ffn_up, w_ffn_down, w_router, w_exp_gate, w_exp_up, w_exp_down,
              final_norm):
    bsz, seq, dm = x.shape
    cuts = [A_WIDTH, 2 * A_WIDTH, 3 * A_WIDTH, 3 * A_WIDTH + Q_LORA,
            3 * A_WIDTH + Q_LORA + KV_LORA, 3 * A_WIDTH + Q_LORA + KV_LORA + B_ROPE]
    for l in range(DEPTH):
        h = _rmsnorm(x, attn_norm[l])
        z = jnp.einsum('bsd,de->bse', h, w_in[l])
        qa, ka, va, cq, ckv, kpe, fc = jnp.split(z, cuts, axis=-1)

        qa = _rope(qa.reshape(bsz, seq, A_HEADS, HEAD_DIM), A_ROT_DIM)
        ka = _rope(ka.reshape(bsz, seq, A_HEADS, HEAD_DIM), A_ROT_DIM)
        va = va.reshape(bsz, seq, A_HEADS, HEAD_DIM)
        o_a = _dilated_attention(qa, ka, va).reshape(bsz, seq, A_WIDTH)

        qb = jnp.einsum('bsr,re->bse', _rmsnorm(cq, q_norm[l]), w_uq[l])
        qb = qb.reshape(bsz, seq, B_HEADS, B_NOPE + B_ROPE)
        qb = jnp.concatenate([qb[..., :B_NOPE], _rope(qb[..., B_NOPE:], B_ROPE)], axis=-1)
        kv = jnp.einsum('bsr,re->bse', _rmsnorm(ckv, kv_norm[l]), w_ukv[l])
        kv = kv.reshape(bsz, seq, B_HEADS, B_NOPE + B_V)
        k_pe = _rope(kpe.reshape(bsz, seq, 1, B_ROPE), B_ROPE)
        kb = jnp.concatenate(
            [kv[..., :B_NOPE], jnp.broadcast_to(k_pe, (bsz, seq, B_HEADS, B_ROPE))], axis=-1)
        vb = kv[..., B_NOPE:]
        o_b = _block_attention(qb, kb, vb).reshape(bsz, seq, B_WIDTH)

        o_c = _fourier_mix(fc)

        g = mix_gain[l]
        mix = jnp.concatenate([
            _rmsnorm(o_a, g[:A_WIDTH]),
            _rmsnorm(o_b, g[A_WIDTH:A_WIDTH + B_WIDTH]),
            _rmsnorm(o_c, g[A_WIDTH + B_WIDTH:]),
        ], axis=-1)
        x = x + jnp.einsum('bse,ed->bsd', mix, w_out[l])

        h = _rmsnorm(x, ffn_norm[l])
        if l % 2 == 0:
            i = l // 2
            y = _swiglu(h, w_ffn_gate[i], w_ffn_up[i], w_ffn_down[i])
        else:
            i = l // 2
            y = _moe_swiglu(h.reshape(bsz * seq, dm), w_router[i], w_exp_gate[i],
                            w_exp_up[i], w_exp_down[i]).reshape(bsz, seq, dm)
        x = x + y
    return _rmsnorm(x, final_norm)
```

```python
import functools

import jax
import jax.numpy as jnp
import numpy as np
from jax import lax
from jax.experimental import pallas as pl
from jax.experimental.pallas import tpu as pltpu

F32 = jnp.float32
BF16 = jnp.bfloat16

HEAD_DIM = 64
A_HEADS = 6
A_WIDTH = A_HEADS * HEAD_DIM
A_ROT_DIM = HEAD_DIM // 4
DILATIONS = (1, 4, 16)
WINDOW_RADIUS = 64
B_HEADS = 6
B_NOPE = 64
B_ROPE = 32
B_V = 64
B_WIDTH = B_HEADS * B_V
Q_LORA = 384
KV_LORA = 128
C_GROUPS = 4
C_GROUP_DIM = 64
C_WIDTH = C_GROUPS * C_GROUP_DIM
ROPE_THETA = 500000.0
N_EXPERTS = 8
TOP_K = 2
RMS_EPS = 1e-6
NEG_INF = -1e30

LANES = 128
HEAD_PAD = 128
VMEM_LIMIT_BYTES = 56 * 1024 * 1024

TM_PROJ = 512
TQ_WIN = 512
WIN_SUB = 128
TQ_MLA = 512
TK_MLA = 512
TM_DFT = 1024
TK_DFT = 1024
TM_FFN = 512
TM_ROUTER = 512
TM_EXP = 1024
TF_EXP = 512


def _cparams(sem):
    return pltpu.CompilerParams(dimension_semantics=sem, vmem_limit_bytes=VMEM_LIMIT_BYTES)


def _rms(x, g):
    return x * lax.rsqrt(jnp.mean(x * x, axis=-1, keepdims=True) + RMS_EPS) * g


def _dot(a, b):
    return jnp.dot(a, b, preferred_element_type=F32)


def _dot_nt(a, b):
    return lax.dot_general(a, b, (((1,), (1,)), ((), ())), preferred_element_type=F32)


def _proj_body(x_ref, g_ref, wm_ref, rot_ref, cosa_ref, sina_ref, cosk_ref, sink_ref,
               cosq_ref, sinq_ref, qn_ref, wuq_ref, kvn_ref, wkn_ref, wv_ref, cs_ref,
               qa_ref, ka_ref, va_ref, qbt_ref, kb_ref, vbt_ref, xcs_ref):
    h = _rms(x_ref[...], g_ref[...]).astype(BF16)
    z = _dot(h, wm_ref[...])
    o = 0
    qa = z[:, o:o + A_WIDTH]; o += A_WIDTH
    ka = z[:, o:o + A_WIDTH]; o += A_WIDTH
    va = z[:, o:o + A_WIDTH]; o += A_WIDTH
    cq = z[:, o:o + Q_LORA]; o += Q_LORA
    ckv = z[:, o:o + KV_LORA]; o += KV_LORA
    kpe = z[:, o:o + HEAD_PAD]; o += HEAD_PAD
    kpe_rot = z[:, o:o + HEAD_PAD]; o += HEAD_PAD
    fc = z[:, o:o + C_WIDTH]

    cosa = cosa_ref[...]
    sina = sina_ref[...]
    rot = rot_ref[...]
    qa_ref[...] = (qa * cosa + _dot(qa.astype(BF16), rot) * sina).astype(BF16)
    ka_ref[...] = (ka * cosa + _dot(ka.astype(BF16), rot) * sina).astype(BF16)
    va_ref[...] = va.astype(BF16)

    cqn = _rms(cq, qn_ref[...]).astype(BF16)
    qbt = _dot_nt(wuq_ref[...], cqn) * ((B_NOPE + B_ROPE) ** -0.5)
    cosq = cosq_ref[...]
    sinq = sinq_ref[...]
    half = B_ROPE // 2
    for hd in range(B_HEADS):
        r0 = hd * HEAD_PAD
        x1 = qbt[r0 + B_NOPE:r0 + B_NOPE + half]
        x2 = qbt[r0 + B_NOPE + half:r0 + B_NOPE + B_ROPE]
        qbt_ref[0, r0:r0 + B_NOPE, :] = qbt[r0:r0 + B_NOPE].astype(BF16)
        qbt_ref[0, r0 + B_NOPE:r0 + B_NOPE + half, :] = (x1 * cosq - x2 * sinq).astype(BF16)
        qbt_ref[0, r0 + B_NOPE + half:r0 + B_NOPE + B_ROPE, :] = (x2 * cosq + x1 * sinq).astype(BF16)
        qbt_ref[0, r0 + B_NOPE + B_ROPE:r0 + HEAD_PAD, :] = jnp.zeros(
            (HEAD_PAD - B_NOPE - B_ROPE, qbt.shape[1]), BF16)

    ckvn = _rms(ckv, kvn_ref[...]).astype(BF16)
    kn = _dot(ckvn, wkn_ref[...])
    kpe_r = kpe * cosk_ref[...] + kpe_rot * sink_ref[...]
    for hd in range(B_HEADS):
        kb_ref[0, hd] = (kn[:, hd * HEAD_PAD:(hd + 1) * HEAD_PAD] + kpe_r).astype(BF16)
    vbt_ref[0] = _dot_nt(wv_ref[...], ckvn).astype(BF16)

    xcs_ref[...] = _dot(fc.astype(BF16), cs_ref[...]).astype(BF16)


def _proj_call(x2, g, wm, rot, tabs, qn, wuq, kvn, wkn, wv, cs, batch, seq):
    t, d = x2.shape
    tm = min(TM_PROJ, seq)
    ns = seq // tm
    nm = wm.shape[1]
    cosa, sina, cosk, sink, cosq, sinq = tabs
    row = lambda i: (i, 0)
    pos = lambda i: (i % ns, 0)
    post = lambda i: (0, i % ns)
    const = lambda i: (0, 0)
    in_specs = [
        pl.BlockSpec((tm, d), row),
        pl.BlockSpec((1, d), const),
        pl.BlockSpec((d, nm), const),
        pl.BlockSpec((A_WIDTH, A_WIDTH), const),
        pl.BlockSpec((tm, A_WIDTH), pos),
        pl.BlockSpec((tm, A_WIDTH), pos),
        pl.BlockSpec((tm, HEAD_PAD), pos),
        pl.BlockSpec((tm, HEAD_PAD), pos),
        pl.BlockSpec((B_ROPE // 2, tm), post),
        pl.BlockSpec((B_ROPE // 2, tm), post),
        pl.BlockSpec((1, Q_LORA), const),
        pl.BlockSpec((B_HEADS * HEAD_PAD, Q_LORA), const),
        pl.BlockSpec((1, KV_LORA), const),
        pl.BlockSpec((KV_LORA, B_HEADS * HEAD_PAD), const),
        pl.BlockSpec((B_WIDTH, KV_LORA), const),
        pl.BlockSpec((C_WIDTH, 2 * C_WIDTH), const),
    ]
    out_shape = (
        jax.ShapeDtypeStruct((t, A_WIDTH), BF16),
        jax.ShapeDtypeStruct((t, A_WIDTH), BF16),
        jax.ShapeDtypeStruct((t, A_WIDTH), BF16),
        jax.ShapeDtypeStruct((batch, B_HEADS * HEAD_PAD, seq), BF16),
        jax.ShapeDtypeStruct((batch, B_HEADS, seq, HEAD_PAD), BF16),
        jax.ShapeDtypeStruct((batch, B_WIDTH, seq), BF16),
        jax.ShapeDtypeStruct((t, 2 * C_WIDTH), BF16),
    )
    out_specs = (
        pl.BlockSpec((tm, A_WIDTH), row),
        pl.BlockSpec((tm, A_WIDTH), row),
        pl.BlockSpec((tm, A_WIDTH), row),
        pl.BlockSpec((1, B_HEADS * HEAD_PAD, tm), lambda i: (i // ns, 0, i % ns)),
        pl.BlockSpec((1, B_HEADS, tm, HEAD_PAD), lambda i: (i // ns, 0, i % ns, 0)),
        pl.BlockSpec((1, B_WIDTH, tm), lambda i: (i // ns, 0, i % ns)),
        pl.BlockSpec((tm, 2 * C_WIDTH), row),
    )
    return pl.pallas_call(
        _proj_body, grid=(t // tm,), in_specs=in_specs, out_specs=out_specs, out_shape=out_shape,
        compiler_params=_cparams(("arbitrary",)), name="proj",
    )(x2, g, wm, rot, cosa, sina, cosk, sink, cosq, sinq, qn, wuq, kvn, wkn, wv, cs)


def _win_body(q_ref, kp_ref, km_ref, kn_ref, vp_ref, vm_ref, vn_ref, o_ref, l_ref, *, seg_len, tq):
    j = pl.program_id(2)
    span = WIN_SUB + 2 * WINDOW_RADIUS
    lane = lax.broadcasted_iota(jnp.int32, (1, LANES), 1)
    lo_half = lane < HEAD_DIM
    rel = (lax.broadcasted_iota(jnp.int32, (WIN_SUB, span), 1) - WINDOW_RADIUS
           - lax.broadcasted_iota(jnp.int32, (WIN_SUB, span), 0))
    band = jnp.abs(rel) <= WINDOW_RADIUS
    kcol = lax.broadcasted_iota(jnp.int32, (1, span), 1)
    for pair in range(A_HEADS // 2):
        cs = slice(pair * LANES, (pair + 1) * LANES)
        q2 = q_ref[0, :, cs] * (HEAD_DIM ** -0.5)
        qh = (jnp.where(lo_half, q2, 0).astype(BF16), jnp.where(lo_half, 0, q2).astype(BF16))
        kcat = jnp.concatenate([kp_ref[0, :, cs], km_ref[0, :, cs], kn_ref[0, :, cs]], axis=0)
        vcat = jnp.concatenate([vp_ref[0, :, cs], vm_ref[0, :, cs], vn_ref[0, :, cs]], axis=0)
        for sb in range(tq // WIN_SUB):
            r0 = sb * WIN_SUB
            ks = kcat[r0:r0 + span]
            vs = vcat[r0:r0 + span]
            kpos = j * tq + r0 - WINDOW_RADIUS + kcol
            valid = band & (kpos >= 0) & (kpos < seg_len)
            outs, lses = [], []
            for hh in range(2):
                s = _dot_nt(qh[hh][r0:r0 + WIN_SUB], ks)
                s = jnp.where(valid, s, NEG_INF)
                m = jnp.max(s, axis=-1, keepdims=True)
                p = jnp.exp(s - m)
                den = jnp.sum(p, axis=-1, keepdims=True)
                outs.append(_dot(p.astype(BF16), vs) / den)
                lses.append(m + jnp.log(den))
            o_ref[0, r0:r0 + WIN_SUB, cs] = jnp.where(lo_half, outs[0], outs[1])
            l_ref[0, r0:r0 + WIN_SUB, cs] = jnp.where(lo_half, lses[0], lses[1])


def _win_call(qa, ka, va, batch, seq, dil):
    seg = seq // dil
    tq = min(TQ_WIN, seg)
    nb = tq // WINDOW_RADIUS
    last = seg // WINDOW_RADIUS - 1
    view = lambda a: a.reshape(batch, seg, dil * A_WIDTH)
    main = pl.BlockSpec((1, tq, A_WIDTH), lambda b, r, j: (b, j, r))
    prev = pl.BlockSpec((1, WINDOW_RADIUS, A_WIDTH), lambda b, r, j: (b, jnp.maximum(j * nb - 1, 0), r))
    nxt = pl.BlockSpec((1, WINDOW_RADIUS, A_WIDTH), lambda b, r, j: (b, jnp.minimum((j + 1) * nb, last), r))
    o, l = pl.pallas_call(
        functools.partial(_win_body, seg_len=seg, tq=tq),
        grid=(batch, dil, seg // tq),
        in_specs=[main, prev, main, nxt, prev, main, nxt],
        out_specs=(main, main),
        out_shape=(jax.ShapeDtypeStruct((batch, seg, dil * A_WIDTH), F32),
                   jax.ShapeDtypeStruct((batch, seg, dil * A_WIDTH), F32)),
        compiler_params=_cparams(("arbitrary", "arbitrary", "arbitrary")), name=f"win_d{dil}",
    )(view(qa), view(ka), view(ka), view(ka), view(va), view(va), view(va))
    return o.reshape(batch * seq, A_WIDTH), l.reshape(batch * seq, A_WIDTH)


def _mla_body(qt_ref, k_ref, vt_ref, o_ref, *, seq, tk):
    tq = qt_ref.shape[2]
    accs = []
    for hh in range(2):
        qt = qt_ref[0, hh * HEAD_PAD:(hh + 1) * HEAD_PAD, :]

        def step(kt, carry, hh=hh, qt=qt):
            m, l, acc = carry
            start = pl.multiple_of(kt * tk, tk)
            st = _dot(k_ref[0, hh, pl.ds(start, tk), :], qt)
            m_new = jnp.maximum(m, jnp.max(st, axis=0, keepdims=True))
            alpha = jnp.exp(m - m_new)
            p = jnp.exp(st - m_new)
            l = alpha * l + jnp.sum(p, axis=0, keepdims=True)
            vt = vt_ref[0, hh * B_V:(hh + 1) * B_V, pl.ds(start, tk)]
            acc = alpha * acc + _dot(vt, p.astype(BF16))
            return m_new, l, acc

        init = (jnp.full((1, tq), NEG_INF, F32), jnp.zeros((1, tq), F32), jnp.zeros((B_V, tq), F32))
        _, l, acc = lax.fori_loop(0, seq // tk, step, init)
        accs.append(acc / l)
    o_ref[...] = jnp.concatenate(accs, axis=0).T.astype(BF16)


def _mla_call(qbt, kb, vbt, batch, seq):
    tq = min(TQ_MLA, seq)
    tk = min(TK_MLA, seq)
    nq = seq // tq
    return pl.pallas_call(
        functools.partial(_mla_body, seq=seq, tk=tk),
        grid=(batch, B_HEADS // 2, nq),
        in_specs=[
            pl.BlockSpec((1, 2 * HEAD_PAD, tq), lambda b, hp, qi: (b, hp, qi)),
            pl.BlockSpec((1, 2, seq, HEAD_PAD), lambda b, hp, qi: (b, hp, 0, 0)),
            pl.BlockSpec((1, 2 * B_V, seq), lambda b, hp, qi: (b, hp, 0)),
        ],
        out_specs=pl.BlockSpec((tq, 2 * B_V), lambda b, hp, qi: (b * nq + qi, hp)),
        out_shape=jax.ShapeDtypeStruct((batch * seq, B_WIDTH), BF16),
        compiler_params=_cparams(("arbitrary", "arbitrary", "arbitrary")), name="mla_attn",
    )(qbt, kb, vbt)


def _dft_body(c_ref, ns_ref, x_ref, o_ref, acc_ref):
    k = pl.program_id(1)

    @pl.when(k == 0)
    def _():
        acc_ref[...] = jnp.zeros_like(acc_ref)

    c = c_ref[...]
    ns = ns_ref[...]
    for b in range(x_ref.shape[0]):
        acc_ref[b] += _dot(c, x_ref[b, :, 0:C_WIDTH]) + _dot(ns, x_ref[b, :, C_WIDTH:2 * C_WIDTH])

    @pl.when(k == pl.num_programs(1) - 1)
    def _():
        o_ref[...] = acc_ref[...].astype(o_ref.dtype)


def _dft_call(cmat, nsmat, xcs, batch, seq):
    tm = min(TM_DFT, seq)
    tk = min(TK_DFT, seq)
    x3 = xcs.reshape(batch, seq, 2 * C_WIDTH)
    y = pl.pallas_call(
        _dft_body, grid=(seq // tm, seq // tk),
        in_specs=[pl.BlockSpec((tm, tk), lambda i, k: (i, k)),
                  pl.BlockSpec((tm, tk), lambda i, k: (i, k)),
                  pl.BlockSpec((batch, tk, 2 * C_WIDTH), lambda i, k: (0, k, 0))],
        out_specs=pl.BlockSpec((batch, tm, C_WIDTH), lambda i, k: (0, i, 0)),
        out_shape=jax.ShapeDtypeStruct((batch, seq, C_WIDTH), BF16),
        scratch_shapes=[pltpu.VMEM((batch, tm, C_WIDTH), F32)],
        compiler_params=_cparams(("arbitrary", "arbitrary")), name="seq_dft",
    )(cmat, nsmat, x3)
    return y.reshape(batch * seq, C_WIDTH)


def _out_body(x_ref, o1_ref, o4_ref, o16_ref, l1_ref, l4_ref, l16_ref, ob_ref, oc_ref,
              ga_ref, gb_ref, gc_ref, wa_ref, wb_ref, wc_ref, y_ref):
    l1, l4, l16 = l1_ref[...], l4_ref[...], l16_ref[...]
    m = jnp.maximum(jnp.maximum(l1, l4), l16)
    e1, e4, e16 = jnp.exp(l1 - m), jnp.exp(l4 - m), jnp.exp(l16 - m)
    oa = (o1_ref[...] * e1 + o4_ref[...] * e4 + o16_ref[...] * e16) / (e1 + e4 + e16)
    na = _rms(oa, ga_ref[...]).astype(BF16)
    nb = _rms(ob_ref[...].astype(F32), gb_ref[...]).astype(BF16)
    nc = _rms(oc_ref[...].astype(F32), gc_ref[...]).astype(BF16)
    y_ref[...] = (x_ref[...] + _dot(na, wa_ref[...]) + _dot(nb, wb_ref[...]) + _dot(nc, wc_ref[...]))


def _out_call(x2, oa, la, ob, oc, ga, gb, gc, wa, wb, wc):
    t, d = x2.shape
    tm = min(TM_PROJ, t)
    row = lambda i: (i, 0)
    const = lambda i: (0, 0)
    wide = lambda w: pl.BlockSpec((tm, w), row)
    return pl.pallas_call(
        _out_body, grid=(t // tm,),
        in_specs=[wide(d)] + [wide(A_WIDTH)] * 6 + [wide(B_WIDTH), wide(C_WIDTH),
                  pl.BlockSpec((1, A_WIDTH), const), pl.BlockSpec((1, B_WIDTH), const),
                  pl.BlockSpec((1, C_WIDTH), const),
                  pl.BlockSpec((A_WIDTH, d), const), pl.BlockSpec((B_WIDTH, d), const),
                  pl.BlockSpec((C_WIDTH, d), const)],
        out_specs=wide(d),
        out_shape=jax.ShapeDtypeStruct((t, d), F32),
        compiler_params=_cparams(("arbitrary",)), name="out_proj",
    )(x2, oa[0], oa[1], oa[2], la[0], la[1], la[2], ob, oc, ga, gb, gc, wa, wb, wc)


def _ffn_body(x_ref, g_ref, wg_ref, wu_ref, wd_ref, y_ref, h_ref, acc_ref):
    f = pl.program_id(1)

    @pl.when(f == 0)
    def _():
        h_ref[...] = _rms(x_ref[...], g_ref[...]).astype(BF16)
        acc_ref[...] = jnp.zeros_like(acc_ref)

    h = h_ref[...]
    gate = _dot(h, wg_ref[...])
    up = _dot(h, wu_ref[...])
    acc_ref[...] += _dot((jax.nn.silu(gate) * up).astype(BF16), wd_ref[...])

    @pl.when(f == pl.num_programs(1) - 1)
    def _():
        y_ref[...] = x_ref[...] + acc_ref[...]


def _ffn_tile(d_ff):
    for steps in (1, 2, 4, 11, 22):
        if d_ff % steps == 0 and (d_ff // steps) % LANES == 0 and d_ff // steps <= 1536:
            return d_ff // steps
    return d_ff


def _ffn_call(x2, g, wg, wu, wd):
    t, d = x2.shape
    d_ff = wg.shape[1]
    tm = min(TM_FFN, t)
    tf = _ffn_tile(d_ff)
    return pl.pallas_call(
        _ffn_body, grid=(t // tm, d_ff // tf),
        in_specs=[pl.BlockSpec((tm, d), lambda i, f: (i, 0)),
                  pl.BlockSpec((1, d), lambda i, f: (0, 0)),
                  pl.BlockSpec((d, tf), lambda i, f: (0, f)),
                  pl.BlockSpec((d, tf), lambda i, f: (0, f)),
                  pl.BlockSpec((tf, d), lambda i, f: (f, 0))],
        out_specs=pl.BlockSpec((tm, d), lambda i, f: (i, 0)),
        out_shape=jax.ShapeDtypeStruct((t, d), F32),
        scratch_shapes=[pltpu.VMEM((tm, d), BF16), pltpu.VMEM((tm, d), F32)],
        compiler_params=_cparams(("arbitrary", "arbitrary")), name="dense_ffn",
    )(x2, g, wg, wu, wd)


def _router_body(x_ref, g_ref, wr_ref, idx_ref, gate_ref):
    h = _rms(x_ref[...], g_ref[...])
    logits = lax.dot_general(wr_ref[...], h, (((1,), (1,)), ((), ())),
                             precision=lax.Precision.HIGHEST, preferred_element_type=F32)
    eid = lax.broadcasted_iota(jnp.int32, logits.shape, 0)
    m1 = jnp.max(logits, axis=0, keepdims=True)
    i1 = jnp.min(jnp.where(logits == m1, eid, N_EXPERTS), axis=0, keepdims=True)
    rest = jnp.where(eid == i1, -jnp.inf, logits)
    m2 = jnp.max(rest, axis=0, keepdims=True)
    i2 = jnp.min(jnp.where(rest == m2, eid, N_EXPERTS), axis=0, keepdims=True)
    e2 = jnp.exp(m2 - m1)
    idx_ref[...] = jnp.concatenate([i1, i2], axis=0)
    gate_ref[...] = jnp.concatenate([1.0 / (1.0 + e2), e2 / (1.0 + e2)], axis=0)


def _router_call(x2, g, wr_t):
    t, d = x2.shape
    tm = min(TM_ROUTER, t)
    return pl.pallas_call(
        _router_body, grid=(t // tm,),
        in_specs=[pl.BlockSpec((tm, d), lambda i: (i, 0)),
                  pl.BlockSpec((1, d), lambda i: (0, 0)),
                  pl.BlockSpec((N_EXPERTS, d), lambda i: (0, 0))],
        out_specs=(pl.BlockSpec((TOP_K, tm), lambda i: (0, i)),
                   pl.BlockSpec((TOP_K, tm), lambda i: (0, i))),
        out_shape=(jax.ShapeDtypeStruct((TOP_K, t), jnp.int32),
                   jax.ShapeDtypeStruct((TOP_K, t), F32)),
        compiler_params=_cparams(("arbitrary",)), name="router",
    )(x2, g, wr_t)


def _expert_body(bexp_ref, brows_ref, tok_ref, dst_ref, x_hbm, g_ref, wg_ref, wu_ref, wd_ref,
                 y_hbm, xbuf, hbuf, acc_ref, sems):
    i = pl.program_id(0)
    f = pl.program_id(1)
    rows = xbuf.shape[0]

    def row_in(r, src_row):
        return pltpu.make_async_copy(x_hbm.at[pl.ds(src_row, 1)], xbuf.at[pl.ds(r, 1)], sems.at[0])

    def row_out(r, dst_row):
        return pltpu.make_async_copy(acc_ref.at[pl.ds(r, 1)], y_hbm.at[pl.ds(dst_row, 1)], sems.at[1])

    @pl.when(brows_ref[i] > 0)
    def _():
        @pl.when(f == 0)
        def _():
            def start(r, c):
                row_in(r, tok_ref[0, 0, r]).start()
                return c
            lax.fori_loop(0, rows, start, 0, unroll=8)

            def wait(r, c):
                row_in(r, 0).wait()
                return c
            lax.fori_loop(0, rows, wait, 0, unroll=8)
            hbuf[...] = _rms(xbuf[...], g_ref[...]).astype(BF16)
            acc_ref[...] = jnp.zeros_like(acc_ref)

        h = hbuf[...]
        gate = _dot(h, wg_ref[0])
        up = _dot(h, wu_ref[0])
        acc_ref[...] += _dot((jax.nn.silu(gate) * up).astype(BF16), wd_ref[0])

        @pl.when(f == pl.num_programs(1) - 1)
        def _():
            def start(r, c):
                row_out(r, dst_ref[0, 0, r]).start()
                return c
            lax.fori_loop(0, rows, start, 0, unroll=8)

            def wait(r, c):
                row_out(r, 0).wait()
                return c
            lax.fori_loop(0, rows, wait, 0, unroll=8)


def _expert_call(x2, g, wg, wu, wd, bexp, brows, row_tok, row_dst, n_out_rows):
    t, d = x2.shape
    n_blocks = bexp.shape[0]
    d_ff = wg.shape[2]
    tf = TF_EXP if d_ff % TF_EXP == 0 else d_ff
    grid_spec = pltpu.PrefetchScalarGridSpec(
        num_scalar_prefetch=2,
        grid=(n_blocks, d_ff // tf),
        in_specs=[
            pl.BlockSpec((1, 1, TM_EXP), lambda i, f, be, br: (i, 0, 0), memory_space=pltpu.SMEM),
            pl.BlockSpec((1, 1, TM_EXP), lambda i, f, be, br: (i, 0, 0), memory_space=pltpu.SMEM),
            pl.BlockSpec(memory_space=pl.ANY),
            pl.BlockSpec((1, d), lambda i, f, be, br: (0, 0)),
            pl.BlockSpec((1, d, tf), lambda i, f, be, br: (be[i], 0, f)),
            pl.BlockSpec((1, d, tf), lambda i, f, be, br: (be[i], 0, f)),
            pl.BlockSpec((1, tf, d), lambda i, f, be, br: (be[i], f, 0)),
        ],
        out_specs=pl.BlockSpec(memory_space=pl.ANY),
        scratch_shapes=[pltpu.VMEM((TM_EXP, d), F32), pltpu.VMEM((TM_EXP, d), BF16),
                        pltpu.VMEM((TM_EXP, d), F32), pltpu.SemaphoreType.DMA((2,))],
    )
    return pl.pallas_call(
        _expert_body, grid_spec=grid_spec,
        out_shape=jax.ShapeDtypeStruct((n_out_rows, d), F32),
        compiler_params=_cparams(("arbitrary", "arbitrary")), name="experts",
    )(bexp, brows, row_tok, row_dst, x2, g, wg, wu, wd)


def _combine_body(x_ref, y0_ref, y1_ref, gate_ref, gf_ref, o_ref, *, final):
    g0 = gate_ref[:, 0:1]
    g1 = gate_ref[:, 1:2]
    y = x_ref[...] + (y0_ref[...] * g0 + y1_ref[...] * g1)
    o_ref[...] = _rms(y, gf_ref[...]) if final else y


def _combine_call(x2, y2, gates_t, gfinal, final):
    t, d = x2.shape
    tm = min(TM_ROUTER, t)
    nt = t // tm
    return pl.pallas_call(
        functools.partial(_combine_body, final=final), grid=(nt,),
        in_specs=[pl.BlockSpec((tm, d), lambda i: (i, 0)),
                  pl.BlockSpec((tm, d), lambda i: (i, 0)),
                  pl.BlockSpec((tm, d), lambda i: (i + nt, 0)),
                  pl.BlockSpec((tm, TOP_K), lambda i: (i, 0)),
                  pl.BlockSpec((1, d), lambda i: (0, 0))],
        out_specs=pl.BlockSpec((tm, d), lambda i: (i, 0)),
        out_shape=jax.ShapeDtypeStruct((t, d), F32),
        compiler_params=_cparams(("arbitrary",)), name="moe_combine",
    )(x2, y2, y2, gates_t, gfinal)


def _final_norm_body(x_ref, g_ref, o_ref):
    o_ref[...] = _rms(x_ref[...], g_ref[...])


def _final_norm_call(x2, g):
    t, d = x2.shape
    tm = min(TM_ROUTER, t)
    return pl.pallas_call(
        _final_norm_body, grid=(t // tm,),
        in_specs=[pl.BlockSpec((tm, d), lambda i: (i, 0)), pl.BlockSpec((1, d), lambda i: (0, 0))],
        out_specs=pl.BlockSpec((tm, d), lambda i: (i, 0)),
        out_shape=jax.ShapeDtypeStruct((t, d), F32),
        compiler_params=_cparams(("arbitrary",)), name="final_norm",
    )(x2, g)


def _route_plan(idx, t):
    n_assign = TOP_K * t
    flat_e = idx.reshape(-1)
    order = jnp.argsort(flat_e).astype(jnp.int32)
    counts = jnp.sum(flat_e[:, None] == jnp.arange(N_EXPERTS, dtype=jnp.int32)[None, :], axis=0,
                     dtype=jnp.int32)
    padded = ((counts + TM_EXP - 1) // TM_EXP) * TM_EXP
    start = jnp.cumsum(counts) - counts
    cum_padded = jnp.cumsum(padded)
    start_padded = cum_padded - padded
    n_blocks = -(-n_assign // TM_EXP) + N_EXPERTS
    block_start = jnp.arange(n_blocks, dtype=jnp.int32) * TM_EXP
    bexp = jnp.minimum(jnp.sum(block_start[:, None] >= cum_padded[None, :], axis=1),
                       N_EXPERTS - 1).astype(jnp.int32)
    off = block_start[:, None] + jnp.arange(TM_EXP, dtype=jnp.int32)[None, :] - start_padded[bexp][:, None]
    valid = off < counts[bexp][:, None]
    a = order[jnp.clip(start[bexp][:, None] + off, 0, n_assign - 1)]
    brows = jnp.sum(valid, axis=1, dtype=jnp.int32)
    pad_rank = (jnp.cumsum((~valid).reshape(-1).astype(jnp.int32)) - 1).reshape(valid.shape)
    row_tok = jnp.where(valid, a % t, 0).astype(jnp.int32)
    row_dst = jnp.where(valid, a, n_assign + pad_rank).astype(jnp.int32)
    n_out_rows = n_blocks * TM_EXP
    return (bexp, brows, row_tok.reshape(n_blocks, 1, TM_EXP), row_dst.reshape(n_blocks, 1, TM_EXP),
            n_out_rows)


def _rope_tables(seq):
    pos = jnp.arange(seq, dtype=F32)[:, None]
    half_a = A_ROT_DIM // 2
    ang_a = pos * (ROPE_THETA ** (-jnp.arange(half_a, dtype=F32) / half_a))[None, :]
    one = jnp.ones((seq, HEAD_DIM - A_ROT_DIM), F32)
    cos_head = jnp.concatenate([jnp.cos(ang_a), jnp.cos(ang_a), one], axis=1)
    sin_head = jnp.concatenate([jnp.sin(ang_a), jnp.sin(ang_a), 0 * one], axis=1)
    cosa = jnp.tile(cos_head, (1, A_HEADS))
    sina = jnp.tile(sin_head, (1, A_HEADS))
    half_b = B_ROPE // 2
    ang_b = pos * (ROPE_THETA ** (-jnp.arange(half_b, dtype=F32) / half_b))[None, :]
    zl = jnp.zeros((seq, B_NOPE), F32)
    zr = jnp.zeros((seq, HEAD_PAD - B_NOPE - B_ROPE), F32)
    cosk = jnp.concatenate([zl, jnp.cos(ang_b), jnp.cos(ang_b), zr], axis=1)
    sink = jnp.concatenate([zl, jnp.sin(ang_b), jnp.sin(ang_b), zr], axis=1)
    return cosa, sina, cosk, sink, jnp.cos(ang_b).T, jnp.sin(ang_b).T


def _rotate_half_matrix():
    r = np.zeros((A_WIDTH, A_WIDTH), np.float32)
    half = A_ROT_DIM // 2
    for hd in range(A_HEADS):
        for i in range(half):
            r[hd * HEAD_DIM + half + i, hd * HEAD_DIM + i] = -1.0
            r[hd * HEAD_DIM + i, hd * HEAD_DIM + half + i] = 1.0
    return jnp.asarray(r, BF16)


def _dft_constants(seq):
    n = jnp.arange(seq, dtype=jnp.int32)
    ang = ((n[:, None] * n[None, :]) % seq).astype(F32) * (2.0 * np.pi / seq)
    scale = seq ** -0.5
    cmat = (jnp.cos(ang) * scale).astype(BF16)
    nsmat = (-jnp.sin(ang) * scale).astype(BF16)
    c = np.arange(C_GROUP_DIM)
    angc = 2.0 * np.pi * ((c[:, None] * c[None, :]) % C_GROUP_DIM) / C_GROUP_DIM
    eye = np.eye(C_GROUPS)
    cs = np.concatenate([np.kron(eye, np.cos(angc)), np.kron(eye, np.sin(angc))], axis=1)
    return cmat, nsmat, jnp.asarray(cs * C_GROUP_DIM ** -0.5, BF16)


def _layer_weights(w_in, w_uq, w_ukv, w_out):
    d = w_in.shape[0]
    c_kpe = 3 * A_WIDTH + Q_LORA + KV_LORA
    w_kpe = w_in[:, c_kpe:c_kpe + B_ROPE]
    half = B_ROPE // 2
    w_kpe_rot = jnp.concatenate([-w_kpe[:, half:], w_kpe[:, :half]], axis=1)
    zl = jnp.zeros((d, B_NOPE), w_in.dtype)
    zr = jnp.zeros((d, HEAD_PAD - B_NOPE - B_ROPE), w_in.dtype)
    wm = jnp.concatenate([w_in[:, :c_kpe], zl, w_kpe, zr, zl, w_kpe_rot, zr,
                          w_in[:, c_kpe + B_ROPE:]], axis=1).astype(BF16)
    uq = w_uq.reshape(Q_LORA, B_HEADS, B_NOPE + B_ROPE)
    uq = jnp.pad(uq, ((0, 0), (0, 0), (0, HEAD_PAD - B_NOPE - B_ROPE)))
    wuq_t = uq.reshape(Q_LORA, B_HEADS * HEAD_PAD).T.astype(BF16)
    ukv = w_ukv.reshape(KV_LORA, B_HEADS, B_NOPE + B_V)
    wkn = jnp.pad(ukv[:, :, :B_NOPE], ((0, 0), (0, 0), (0, HEAD_PAD - B_NOPE)))
    wkn = wkn.reshape(KV_LORA, B_HEADS * HEAD_PAD).astype(BF16)
    wv_t = ukv[:, :, B_NOPE:].reshape(KV_LORA, B_WIDTH).T.astype(BF16)
    wo = w_out.astype(BF16)
    return wm, wuq_t, wkn, wv_t, wo[:A_WIDTH], wo[A_WIDTH:A_WIDTH + B_WIDTH], wo[A_WIDTH + B_WIDTH:]


def kernel(x, attn_norm, w_in, q_norm, w_uq, kv_norm, w_ukv, mix_gain, w_out, ffn_norm,
           w_ffn_gate, w_ffn_up, w_ffn_down, w_router, w_exp_gate, w_exp_up, w_exp_down,
           final_norm):
    batch, seq, d = x.shape
    depth = w_in.shape[0]
    t = batch * seq
    assert seq % (max(DILATIONS) * 2 * WIN_SUB) == 0 and t % TM_EXP == 0
    tabs = _rope_tables(seq)
    rot = _rotate_half_matrix()
    cmat, nsmat, cs = _dft_constants(seq)
    row = lambda v: v.reshape(1, -1)

    x2 = x.reshape(t, d)
    for l in range(depth):
        wm, wuq_t, wkn, wv_t, wa, wb, wc = _layer_weights(w_in[l], w_uq[l], w_ukv[l], w_out[l])
        qa, ka, va, qbt, kb, vbt, xcs = _proj_call(
            x2, row(attn_norm[l]), wm, rot, tabs, row(q_norm[l]), wuq_t, row(kv_norm[l]), wkn, wv_t,
            cs, batch, seq)
        win = [_win_call(qa, ka, va, batch, seq, dil) for dil in DILATIONS]
        ob = _mla_call(qbt, kb, vbt, batch, seq)
        oc = _dft_call(cmat, nsmat, xcs, batch, seq)
        g = mix_gain[l]
        x2 = _out_call(x2, [w[0] for w in win], [w[1] for w in win], ob, oc,
                       row(g[:A_WIDTH]), row(g[A_WIDTH:A_WIDTH + B_WIDTH]), row(g[A_WIDTH + B_WIDTH:]),
                       wa, wb, wc)
        i = l // 2
        if l % 2 == 0:
            x2 = _ffn_call(x2, row(ffn_norm[l]), w_ffn_gate[i].astype(BF16), w_ffn_up[i].astype(BF16),
                           w_ffn_down[i].astype(BF16))
            if l == depth - 1:
                x2 = _final_norm_call(x2, row(final_norm))
        else:
            idx, gates = _router_call(x2, row(ffn_norm[l]), w_router[i].T)
            bexp, brows, row_tok, row_dst, n_out_rows = _route_plan(idx, t)
            y2 = _expert_call(x2, row(ffn_norm[l]), w_exp_gate[i].astype(BF16), w_exp_up[i].astype(BF16),
                              w_exp_down[i].astype(BF16), bexp, brows, row_tok, row_dst, n_out_rows)
            x2 = _combine_call(x2, y2, gates.T, row(final_norm), final=(l == depth - 1))
    return x2.reshape(batch, seq, d)
```

```python
import functools
import math

import jax
import jax.numpy as jnp
import numpy as np
from jax import lax
from jax.experimental import pallas as pl
from jax.experimental.pallas import tpu as pltpu

F32 = jnp.float32
BF16 = jnp.bfloat16

HEAD_DIM = 64
A_HEADS = 6
A_WIDTH = A_HEADS * HEAD_DIM
A_ROT_DIM = HEAD_DIM // 4
DILATIONS = (1, 4, 16)
WINDOW_RADIUS = 64
B_HEADS = 6
B_NOPE = 64
B_ROPE = 32
B_V = 64
B_WIDTH = B_HEADS * B_V
Q_LORA = 384
KV_LORA = 128
C_GROUPS = 4
C_GROUP_DIM = 64
C_WIDTH = C_GROUPS * C_GROUP_DIM
ROPE_THETA = 500000.0
N_EXPERTS = 8
TOP_K = 2
RMS_EPS = 1e-6
NEG_INF = -1e30

LANES = 128
BF16_SUBLANES = 16
HEAD_PAD = 128
VMEM_LIMIT_BYTES = 56 * 1024 * 1024

TM_PROJ = 512
TQ_WIN = 512
WIN_SUB = 128
TQ_MLA = 512
TK_MLA = 256
TM_DFT = 1024
TK_DFT = 1024
TM_FFN = 512
TM_ROUTER = 512
TM_EXP = 1024
TF_EXP = 512


def _cparams(sem):
    return pltpu.CompilerParams(dimension_semantics=sem, vmem_limit_bytes=VMEM_LIMIT_BYTES)


def _rms(x, g):
    return x * lax.rsqrt(jnp.mean(x * x, axis=-1, keepdims=True) + RMS_EPS) * g


def _dot(a, b):
    return jnp.dot(a, b, preferred_element_type=F32)


def _dot_nt(a, b):
    return lax.dot_general(a, b, (((1,), (1,)), ((), ())), preferred_element_type=F32)


def _proj_body(x_ref, g_ref, wm_ref, rot_ref, cosa_ref, sina_ref, cosk_ref, sink_ref,
               cosq_ref, sinq_ref, qn_ref, wuq_ref, kvn_ref, wkn_ref, wv_ref, cs_ref,
               qa_ref, ka_ref, va_ref, qbt_ref, kb_ref, vbt_ref, xcs_ref):
    h = _rms(x_ref[...], g_ref[...]).astype(BF16)
    z = _dot(h, wm_ref[...])
    o = 0
    qa = z[:, o:o + A_WIDTH]; o += A_WIDTH
    ka = z[:, o:o + A_WIDTH]; o += A_WIDTH
    va = z[:, o:o + A_WIDTH]; o += A_WIDTH
    cq = z[:, o:o + Q_LORA]; o += Q_LORA
    ckv = z[:, o:o + KV_LORA]; o += KV_LORA
    kpe = z[:, o:o + HEAD_PAD]; o += HEAD_PAD
    kpe_rot = z[:, o:o + HEAD_PAD]; o += HEAD_PAD
    fc = z[:, o:o + C_WIDTH]

    cosa = cosa_ref[...]
    sina = sina_ref[...]
    rot = rot_ref[...]
    qa_ref[...] = (qa * cosa + _dot(qa.astype(BF16), rot) * sina).astype(BF16)
    ka_ref[...] = (ka * cosa + _dot(ka.astype(BF16), rot) * sina).astype(BF16)
    va_ref[...] = va.astype(BF16)

    cqn = _rms(cq, qn_ref[...]).astype(BF16)
    qbt = _dot_nt(wuq_ref[...], cqn) * ((B_NOPE + B_ROPE) ** -0.5 * math.log2(math.e))
    cosq = cosq_ref[...]
    sinq = sinq_ref[...]
    half = B_ROPE // 2
    for hd in range(B_HEADS):
        r0 = hd * HEAD_PAD
        x1 = qbt[r0 + B_NOPE:r0 + B_NOPE + half]
        x2 = qbt[r0 + B_NOPE + half:r0 + B_NOPE + B_ROPE]
        qbt_ref[0, r0:r0 + B_NOPE, :] = qbt[r0:r0 + B_NOPE].astype(BF16)
        qbt_ref[0, r0 + B_NOPE:r0 + B_NOPE + half, :] = (x1 * cosq - x2 * sinq).astype(BF16)
        qbt_ref[0, r0 + B_NOPE + half:r0 + B_NOPE + B_ROPE, :] = (x2 * cosq + x1 * sinq).astype(BF16)
        qbt_ref[0, r0 + B_NOPE + B_ROPE:r0 + HEAD_PAD, :] = jnp.zeros(
            (HEAD_PAD - B_NOPE - B_ROPE, qbt.shape[1]), BF16)

    ckvn = _rms(ckv, kvn_ref[...]).astype(BF16)
    kn = _dot(ckvn, wkn_ref[...])
    kpe_r = kpe * cosk_ref[...] + kpe_rot * sink_ref[...]
    for hd in range(B_HEADS):
        kb_ref[0, hd] = (kn[:, hd * HEAD_PAD:(hd + 1) * HEAD_PAD] + kpe_r).astype(BF16)
    vbt_ref[0] = _dot_nt(wv_ref[...], ckvn).astype(BF16)

    xcs_ref[...] = _dot(fc.astype(BF16), cs_ref[...]).astype(BF16)


def _proj_call(x2, g, wm, rot, tabs, qn, wuq, kvn, wkn, wv, cs, batch, seq):
    t, d = x2.shape
    tm = min(TM_PROJ, seq)
    ns = seq // tm
    nm = wm.shape[1]
    cosa, sina, cosk, sink, cosq, sinq = tabs
    row = lambda i: (i, 0)
    pos = lambda i: (i % ns, 0)
    post = lambda i: (0, i % ns)
    const = lambda i: (0, 0)
    in_specs = [
        pl.BlockSpec((tm, d), row),
        pl.BlockSpec((1, d), const),
        pl.BlockSpec((d, nm), const),
        pl.BlockSpec((A_WIDTH, A_WIDTH), const),
        pl.BlockSpec((tm, A_WIDTH), pos),
        pl.BlockSpec((tm, A_WIDTH), pos),
        pl.BlockSpec((tm, HEAD_PAD), pos),
        pl.BlockSpec((tm, HEAD_PAD), pos),
        pl.BlockSpec((B_ROPE // 2, tm), post),
        pl.BlockSpec((B_ROPE // 2, tm), post),
        pl.BlockSpec((1, Q_LORA), const),
        pl.BlockSpec((B_HEADS * HEAD_PAD, Q_LORA), const),
        pl.BlockSpec((1, KV_LORA), const),
        pl.BlockSpec((KV_LORA, B_HEADS * HEAD_PAD), const),
        pl.BlockSpec((B_WIDTH, KV_LORA), const),
        pl.BlockSpec((C_WIDTH, 2 * C_WIDTH), const),
    ]
    out_shape = (
        jax.ShapeDtypeStruct((t, A_WIDTH), BF16),
        jax.ShapeDtypeStruct((t, A_WIDTH), BF16),
        jax.ShapeDtypeStruct((t, A_WIDTH), BF16),
        jax.ShapeDtypeStruct((batch, B_HEADS * HEAD_PAD, seq), BF16),
        jax.ShapeDtypeStruct((batch, B_HEADS, seq, HEAD_PAD), BF16),
        jax.ShapeDtypeStruct((batch, B_WIDTH, seq), BF16),
        jax.ShapeDtypeStruct((t, 2 * C_WIDTH), BF16),
    )
    out_specs = (
        pl.BlockSpec((tm, A_WIDTH), row),
        pl.BlockSpec((tm, A_WIDTH), row),
        pl.BlockSpec((tm, A_WIDTH), row),
        pl.BlockSpec((1, B_HEADS * HEAD_PAD, tm), lambda i: (i // ns, 0, i % ns)),
        pl.BlockSpec((1, B_HEADS, tm, HEAD_PAD), lambda i: (i // ns, 0, i % ns, 0)),
        pl.BlockSpec((1, B_WIDTH, tm), lambda i: (i // ns, 0, i % ns)),
        pl.BlockSpec((tm, 2 * C_WIDTH), row),
    )
    return pl.pallas_call(
        _proj_body, grid=(t // tm,), in_specs=in_specs, out_specs=out_specs, out_shape=out_shape,
        compiler_params=_cparams(("arbitrary",)), name="proj",
    )(x2, g, wm, rot, cosa, sina, cosk, sink, cosq, sinq, qn, wuq, kvn, wkn, wv, cs)


def _win_body(q_ref, kp_ref, km_ref, kn_ref, vp_ref, vm_ref, vn_ref, o_ref, l_ref, *, seg_len, tq):
    j = pl.program_id(2)
    span = WIN_SUB + 2 * WINDOW_RADIUS
    lane = lax.broadcasted_iota(jnp.int32, (1, LANES), 1)
    lo_half = lane < HEAD_DIM
    rel = (lax.broadcasted_iota(jnp.int32, (WIN_SUB, span), 1) - WINDOW_RADIUS
           - lax.broadcasted_iota(jnp.int32, (WIN_SUB, span), 0))
    band = jnp.abs(rel) <= WINDOW_RADIUS
    kcol = lax.broadcasted_iota(jnp.int32, (1, span), 1)
    for pair in range(A_HEADS // 2):
        cs = slice(pair * LANES, (pair + 1) * LANES)
        q2 = q_ref[0, :, cs] * (HEAD_DIM ** -0.5)
        qh = (jnp.where(lo_half, q2, 0).astype(BF16), jnp.where(lo_half, 0, q2).astype(BF16))
        kcat = jnp.concatenate([kp_ref[0, :, cs], km_ref[0, :, cs], kn_ref[0, :, cs]], axis=0)
        vcat = jnp.concatenate([vp_ref[0, :, cs], vm_ref[0, :, cs], vn_ref[0, :, cs]], axis=0)
        for sb in range(tq // WIN_SUB):
            r0 = sb * WIN_SUB
            ks = kcat[r0:r0 + span]
            vs = vcat[r0:r0 + span]
            kpos = j * tq + r0 - WINDOW_RADIUS + kcol
            valid = band & (kpos >= 0) & (kpos < seg_len)
            outs, lses = [], []
            for hh in range(2):
                s = _dot_nt(qh[hh][r0:r0 + WIN_SUB], ks)
                s = jnp.where(valid, s, NEG_INF)
                m = jnp.max(s, axis=-1, keepdims=True)
                p = jnp.exp(s - m)
                den = jnp.sum(p, axis=-1, keepdims=True)
                outs.append(_dot(p.astype(BF16), vs) / den)
                lses.append(m + jnp.log(den))
            o_ref[0, r0:r0 + WIN_SUB, cs] = jnp.where(lo_half, outs[0], outs[1])
            l_ref[0, r0:r0 + WIN_SUB, cs] = jnp.where(lo_half, lses[0], lses[1])


def _win_call(qa, ka, va, batch, seq, dil):
    seg = seq // dil
    tq = min(TQ_WIN, seg)
    nb = tq // WINDOW_RADIUS
    last = seg // WINDOW_RADIUS - 1
    view = lambda a: a.reshape(batch, seg, dil * A_WIDTH)
    main = pl.BlockSpec((1, tq, A_WIDTH), lambda b, r, j: (b, j, r))
    prev = pl.BlockSpec((1, WINDOW_RADIUS, A_WIDTH), lambda b, r, j: (b, jnp.maximum(j * nb - 1, 0), r))
    nxt = pl.BlockSpec((1, WINDOW_RADIUS, A_WIDTH), lambda b, r, j: (b, jnp.minimum((j + 1) * nb, last), r))
    o, l = pl.pallas_call(
        functools.partial(_win_body, seg_len=seg, tq=tq),
        grid=(batch, dil, seg // tq),
        in_specs=[main, prev, main, nxt, prev, main, nxt],
        out_specs=(main, main),
        out_shape=(jax.ShapeDtypeStruct((batch, seg, dil * A_WIDTH), F32),
                   jax.ShapeDtypeStruct((batch, seg, dil * A_WIDTH), F32)),
        compiler_params=_cparams(("arbitrary", "arbitrary", "arbitrary")), name=f"win_d{dil}",
    )(view(qa), view(ka), view(ka), view(ka), view(va), view(va), view(va))
    return o.reshape(batch * seq, A_WIDTH), l.reshape(batch * seq, A_WIDTH)


def _mla_body(qt_ref, k_ref, vt_ref, o_ref, sa, sb, pa, pb, acc_ref, *, seq, tk):
    tq = qt_ref.shape[2]
    nk = seq // tk
    qts = [qt_ref[0, h * HEAD_PAD:(h + 1) * HEAD_PAD, :] for h in range(B_HEADS)]
    ones = jnp.ones((BF16_SUBLANES, tk), BF16)

    def scores(h, kt, dst):
        start = pl.multiple_of(kt * tk, tk)
        dst[h] = _dot(k_ref[0, h, pl.ds(start, tk), :], qts[h])

    def values(h, kt, src, alpha):
        start = pl.multiple_of(kt * tk, tk)
        vt = jnp.concatenate([vt_ref[0, h * B_V:(h + 1) * B_V, pl.ds(start, tk)], ones], axis=0)
        acc_ref[h] = alpha * acc_ref[h] + _dot(vt, src[h])

    def softmax(h, src, dst, m):
        st = src[h]
        m_new = jnp.maximum(m, jnp.max(st, axis=0, keepdims=True))
        dst[h] = jnp.exp2(st - m_new).astype(BF16)
        return m_new, jnp.exp2(m - m_new)

    for h in range(B_HEADS):
        scores(h, 0, sa)
        acc_ref[h] = jnp.zeros((B_V + BF16_SUBLANES, tq), F32)
    pb[...] = jnp.zeros_like(pb)

    def step(i, carry):
        out = []
        for h in range(B_HEADS):
            m, alpha_prev = carry[h]
            t0 = 2 * i
            scores(h, t0 + 1, sb)
            values(h, jnp.maximum(t0 - 1, 0), pb, alpha_prev)
            m, alpha0 = softmax(h, sa, pa, m)
            scores(h, jnp.minimum(t0 + 2, nk - 1), sa)
            values(h, t0, pa, alpha0)
            m, alpha1 = softmax(h, sb, pb, m)
            out.append((m, alpha1))
        return tuple(out)

    init = tuple((jnp.full((1, tq), NEG_INF, F32), jnp.ones((1, tq), F32)) for _ in range(B_HEADS))
    res = lax.fori_loop(0, nk // 2, step, init)
    outs = []
    for h in range(B_HEADS):
        values(h, nk - 1, pb, res[h][1])
        acc = acc_ref[h]
        outs.append(acc[:B_V] / acc[B_V:B_V + 1])
    o_ref[...] = jnp.concatenate(outs, axis=0).T.astype(BF16)


def _mla_call(qbt, kb, vbt, batch, seq):
    tq = min(TQ_MLA, seq)
    tk = min(TK_MLA, seq)
    nq = seq // tq
    assert (seq // tk) % 2 == 0
    return pl.pallas_call(
        functools.partial(_mla_body, seq=seq, tk=tk),
        grid=(batch, nq),
        in_specs=[
            pl.BlockSpec((1, B_HEADS * HEAD_PAD, tq), lambda b, qi: (b, 0, qi)),
            pl.BlockSpec((1, B_HEADS, seq, HEAD_PAD), lambda b, qi: (b, 0, 0, 0)),
            pl.BlockSpec((1, B_WIDTH, seq), lambda b, qi: (b, 0, 0)),
        ],
        out_specs=pl.BlockSpec((tq, B_WIDTH), lambda b, qi: (b * nq + qi, 0)),
        out_shape=jax.ShapeDtypeStruct((batch * seq, B_WIDTH), BF16),
        scratch_shapes=[pltpu.VMEM((B_HEADS, tk, tq), F32), pltpu.VMEM((B_HEADS, tk, tq), F32),
                        pltpu.VMEM((B_HEADS, tk, tq), BF16), pltpu.VMEM((B_HEADS, tk, tq), BF16),
                        pltpu.VMEM((B_HEADS, B_V + BF16_SUBLANES, tq), F32)],
        compiler_params=_cparams(("arbitrary", "arbitrary")), name="mla_attn",
    )(qbt, kb, vbt)


def _dft_body(c_ref, ns_ref, x_ref, o_ref, acc_ref):
    k = pl.program_id(1)

    @pl.when(k == 0)
    def _():
        acc_ref[...] = jnp.zeros_like(acc_ref)

    c = c_ref[...]
    ns = ns_ref[...]
    for b in range(x_ref.shape[0]):
        acc_ref[b] += _dot(c, x_ref[b, :, 0:C_WIDTH]) + _dot(ns, x_ref[b, :, C_WIDTH:2 * C_WIDTH])

    @pl.when(k == pl.num_programs(1) - 1)
    def _():
        o_ref[...] = acc_ref[...].astype(o_ref.dtype)


def _dft_call(cmat, nsmat, xcs, batch, seq):
    tm = min(TM_DFT, seq)
    tk = min(TK_DFT, seq)
    x3 = xcs.reshape(batch, seq, 2 * C_WIDTH)
    y = pl.pallas_call(
        _dft_body, grid=(seq // tm, seq // tk),
        in_specs=[pl.BlockSpec((tm, tk), lambda i, k: (i, k)),
                  pl.BlockSpec((tm, tk), lambda i, k: (i, k)),
                  pl.BlockSpec((batch, tk, 2 * C_WIDTH), lambda i, k: (0, k, 0))],
        out_specs=pl.BlockSpec((batch, tm, C_WIDTH), lambda i, k: (0, i, 0)),
        out_shape=jax.ShapeDtypeStruct((batch, seq, C_WIDTH), BF16),
        scratch_shapes=[pltpu.VMEM((batch, tm, C_WIDTH), F32)],
        compiler_params=_cparams(("arbitrary", "arbitrary")), name="seq_dft",
    )(cmat, nsmat, x3)
    return y.reshape(batch * seq, C_WIDTH)


def _out_body(x_ref, o1_ref, o4_ref, o16_ref, l1_ref, l4_ref, l16_ref, ob_ref, oc_ref,
              ga_ref, gb_ref, gc_ref, wa_ref, wb_ref, wc_ref, y_ref):
    l1, l4, l16 = l1_ref[...], l4_ref[...], l16_ref[...]
    m = jnp.maximum(jnp.maximum(l1, l4), l16)
    e1, e4, e16 = jnp.exp(l1 - m), jnp.exp(l4 - m), jnp.exp(l16 - m)
    oa = (o1_ref[...] * e1 + o4_ref[...] * e4 + o16_ref[...] * e16) / (e1 + e4 + e16)
    na = _rms(oa, ga_ref[...]).astype(BF16)
    nb = _rms(ob_ref[...].astype(F32), gb_ref[...]).astype(BF16)
    nc = _rms(oc_ref[...].astype(F32), gc_ref[...]).astype(BF16)
    y_ref[...] = (x_ref[...] + _dot(na, wa_ref[...]) + _dot(nb, wb_ref[...]) + _dot(nc, wc_ref[...]))


def _out_call(x2, oa, la, ob, oc, ga, gb, gc, wa, wb, wc):
    t, d = x2.shape
    tm = min(TM_PROJ, t)
    row = lambda i: (i, 0)
    const = lambda i: (0, 0)
    wide = lambda w: pl.BlockSpec((tm, w), row)
    return pl.pallas_call(
        _out_body, grid=(t // tm,),
        in_specs=[wide(d)] + [wide(A_WIDTH)] * 6 + [wide(B_WIDTH), wide(C_WIDTH),
                  pl.BlockSpec((1, A_WIDTH), const), pl.BlockSpec((1, B_WIDTH), const),
                  pl.BlockSpec((1, C_WIDTH), const),
                  pl.BlockSpec((A_WIDTH, d), const), pl.BlockSpec((B_WIDTH, d), const),
                  pl.BlockSpec((C_WIDTH, d), const)],
        out_specs=wide(d),
        out_shape=jax.ShapeDtypeStruct((t, d), F32),
        compiler_params=_cparams(("arbitrary",)), name="out_proj",
    )(x2, oa[0], oa[1], oa[2], la[0], la[1], la[2], ob, oc, ga, gb, gc, wa, wb, wc)


def _ffn_body(x_ref, g_ref, wg_ref, wu_ref, wd_ref, y_ref, h_ref, acc_ref):
    f = pl.program_id(1)

    @pl.when(f == 0)
    def _():
        h_ref[...] = _rms(x_ref[...], g_ref[...]).astype(BF16)
        acc_ref[...] = jnp.zeros_like(acc_ref)

    h = h_ref[...]
    gate = _dot(h, wg_ref[...])
    up = _dot(h, wu_ref[...])
    acc_ref[...] += _dot((jax.nn.silu(gate) * up).astype(BF16), wd_ref[...])

    @pl.when(f == pl.num_programs(1) - 1)
    def _():
        y_ref[...] = x_ref[...] + acc_ref[...]


def _ffn_tile(d_ff):
    for steps in (1, 2, 4, 11, 22):
        if d_ff % steps == 0 and (d_ff // steps) % LANES == 0 and d_ff // steps <= 1536:
            return d_ff // steps
    return d_ff


def _ffn_call(x2, g, wg, wu, wd):
    t, d = x2.shape
    d_ff = wg.shape[1]
    tm = min(TM_FFN, t)
    tf = _ffn_tile(d_ff)
    return pl.pallas_call(
        _ffn_body, grid=(t // tm, d_ff // tf),
        in_specs=[pl.BlockSpec((tm, d), lambda i, f: (i, 0)),
                  pl.BlockSpec((1, d), lambda i, f: (0, 0)),
                  pl.BlockSpec((d, tf), lambda i, f: (0, f)),
                  pl.BlockSpec((d, tf), lambda i, f: (0, f)),
                  pl.BlockSpec((tf, d), lambda i, f: (f, 0))],
        out_specs=pl.BlockSpec((tm, d), lambda i, f: (i, 0)),
        out_shape=jax.ShapeDtypeStruct((t, d), F32),
        scratch_shapes=[pltpu.VMEM((tm, d), BF16), pltpu.VMEM((tm, d), F32)],
        compiler_params=_cparams(("arbitrary", "arbitrary")), name="dense_ffn",
    )(x2, g, wg, wu, wd)


def _router_body(x_ref, g_ref, wr_ref, idx_ref, gate_ref):
    h = _rms(x_ref[...], g_ref[...])
    logits = lax.dot_general(wr_ref[...], h, (((1,), (1,)), ((), ())),
                             precision=lax.Precision.HIGHEST, preferred_element_type=F32)
    eid = lax.broadcasted_iota(jnp.int32, logits.shape, 0)
    m1 = jnp.max(logits, axis=0, keepdims=True)
    i1 = jnp.min(jnp.where(logits == m1, eid, N_EXPERTS), axis=0, keepdims=True)
    rest = jnp.where(eid == i1, -jnp.inf, logits)
    m2 = jnp.max(rest, axis=0, keepdims=True)
    i2 = jnp.min(jnp.where(rest == m2, eid, N_EXPERTS), axis=0, keepdims=True)
    e2 = jnp.exp(m2 - m1)
    idx_ref[...] = jnp.concatenate([i1, i2], axis=0)
    gate_ref[...] = jnp.concatenate([1.0 / (1.0 + e2), e2 / (1.0 + e2)], axis=0)


def _router_call(x2, g, wr_t):
    t, d = x2.shape
    tm = min(TM_ROUTER, t)
    return pl.pallas_call(
        _router_body, grid=(t // tm,),
        in_specs=[pl.BlockSpec((tm, d), lambda i: (i, 0)),
                  pl.BlockSpec((1, d), lambda i: (0, 0)),
                  pl.BlockSpec((N_EXPERTS, d), lambda i: (0, 0))],
        out_specs=(pl.BlockSpec((TOP_K, tm), lambda i: (0, i)),
                   pl.BlockSpec((TOP_K, tm), lambda i: (0, i))),
        out_shape=(jax.ShapeDtypeStruct((TOP_K, t), jnp.int32),
                   jax.ShapeDtypeStruct((TOP_K, t), F32)),
        compiler_params=_cparams(("arbitrary",)), name="router",
    )(x2, g, wr_t)


def _expert_body(bexp_ref, brows_ref, tok_ref, dst_ref, x_hbm, g_ref, wg_ref, wu_ref, wd_ref,
                 y_hbm, xbuf, hbuf, acc_ref, sems):
    i = pl.program_id(0)
    f = pl.program_id(1)
    rows = xbuf.shape[0]

    def row_in(r, src_row):
        return pltpu.make_async_copy(x_hbm.at[pl.ds(src_row, 1)], xbuf.at[pl.ds(r, 1)], sems.at[0])

    def row_out(r, dst_row):
        return pltpu.make_async_copy(acc_ref.at[pl.ds(r, 1)], y_hbm.at[pl.ds(dst_row, 1)], sems.at[1])

    @pl.when(brows_ref[i] > 0)
    def _():
        @pl.when(f == 0)
        def _():
            def start(r, c):
                row_in(r, tok_ref[0, 0, r]).start()
                return c
            lax.fori_loop(0, rows, start, 0, unroll=8)

            def wait(r, c):
                row_in(r, 0).wait()
                return c
            lax.fori_loop(0, rows, wait, 0, unroll=8)
            hbuf[...] = _rms(xbuf[...], g_ref[...]).astype(BF16)
            acc_ref[...] = jnp.zeros_like(acc_ref)

        h = hbuf[...]
        gate = _dot(h, wg_ref[0])
        up = _dot(h, wu_ref[0])
        acc_ref[...] += _dot((jax.nn.silu(gate) * up).astype(BF16), wd_ref[0])

        @pl.when(f == pl.num_programs(1) - 1)
        def _():
            def start(r, c):
                row_out(r, dst_ref[0, 0, r]).start()
                return c
            lax.fori_loop(0, rows, start, 0, unroll=8)

            def wait(r, c):
                row_out(r, 0).wait()
                return c
            lax.fori_loop(0, rows, wait, 0, unroll=8)


def _expert_call(x2, g, wg, wu, wd, bexp, brows, row_tok, row_dst, n_out_rows):
    t, d = x2.shape
    n_blocks = bexp.shape[0]
    d_ff = wg.shape[2]
    tf = TF_EXP if d_ff % TF_EXP == 0 else d_ff
    grid_spec = pltpu.PrefetchScalarGridSpec(
        num_scalar_prefetch=2,
        grid=(n_blocks, d_ff // tf),
        in_specs=[
            pl.BlockSpec((1, 1, TM_EXP), lambda i, f, be, br: (i, 0, 0), memory_space=pltpu.SMEM),
            pl.BlockSpec((1, 1, TM_EXP), lambda i, f, be, br: (i, 0, 0), memory_space=pltpu.SMEM),
            pl.BlockSpec(memory_space=pl.ANY),
            pl.BlockSpec((1, d), lambda i, f, be, br: (0, 0)),
            pl.BlockSpec((1, d, tf), lambda i, f, be, br: (be[i], 0, f)),
            pl.BlockSpec((1, d, tf), lambda i, f, be, br: (be[i], 0, f)),
            pl.BlockSpec((1, tf, d), lambda i, f, be, br: (be[i], f, 0)),
        ],
        out_specs=pl.BlockSpec(memory_space=pl.ANY),
        scratch_shapes=[pltpu.VMEM((TM_EXP, d), F32), pltpu.VMEM((TM_EXP, d), BF16),
                        pltpu.VMEM((TM_EXP, d), F32), pltpu.SemaphoreType.DMA((2,))],
    )
    return pl.pallas_call(
        _expert_body, grid_spec=grid_spec,
        out_shape=jax.ShapeDtypeStruct((n_out_rows, d), F32),
        compiler_params=_cparams(("arbitrary", "arbitrary")), name="experts",
    )(bexp, brows, row_tok, row_dst, x2, g, wg, wu, wd)


def _combine_body(x_ref, y0_ref, y1_ref, gate_ref, gf_ref, o_ref, *, final):
    g0 = gate_ref[:, 0:1]
    g1 = gate_ref[:, 1:2]
    y = x_ref[...] + (y0_ref[...] * g0 + y1_ref[...] * g1)
    o_ref[...] = _rms(y, gf_ref[...]) if final else y


def _combine_call(x2, y2, gates_t, gfinal, final):
    t, d = x2.shape
    tm = min(TM_ROUTER, t)
    nt = t // tm
    return pl.pallas_call(
        functools.partial(_combine_body, final=final), grid=(nt,),
        in_specs=[pl.BlockSpec((tm, d), lambda i: (i, 0)),
                  pl.BlockSpec((tm, d), lambda i: (i, 0)),
                  pl.BlockSpec((tm, d), lambda i: (i + nt, 0)),
                  pl.BlockSpec((tm, TOP_K), lambda i: (i, 0)),
                  pl.BlockSpec((1, d), lambda i: (0, 0))],
        out_specs=pl.BlockSpec((tm, d), lambda i: (i, 0)),
        out_shape=jax.ShapeDtypeStruct((t, d), F32),
        compiler_params=_cparams(("arbitrary",)), name="moe_combine",
    )(x2, y2, y2, gates_t, gfinal)


def _final_norm_body(x_ref, g_ref, o_ref):
    o_ref[...] = _rms(x_ref[...], g_ref[...])


def _final_norm_call(x2, g):
    t, d = x2.shape
    tm = min(TM_ROUTER, t)
    return pl.pallas_call(
        _final_norm_body, grid=(t // tm,),
        in_specs=[pl.BlockSpec((tm, d), lambda i: (i, 0)), pl.BlockSpec((1, d), lambda i: (0, 0))],
        out_specs=pl.BlockSpec((tm, d), lambda i: (i, 0)),
        out_shape=jax.ShapeDtypeStruct((t, d), F32),
        compiler_params=_cparams(("arbitrary",)), name="final_norm",
    )(x2, g)


def _route_plan(idx, t):
    n_assign = TOP_K * t
    flat_e = idx.reshape(-1)
    order = jnp.argsort(flat_e).astype(jnp.int32)
    counts = jnp.sum(flat_e[:, None] == jnp.arange(N_EXPERTS, dtype=jnp.int32)[None, :], axis=0,
                     dtype=jnp.int32)
    padded = ((counts + TM_EXP - 1) // TM_EXP) * TM_EXP
    start = jnp.cumsum(counts) - counts
    cum_padded = jnp.cumsum(padded)
    start_padded = cum_padded - padded
    n_blocks = -(-n_assign // TM_EXP) + N_EXPERTS
    block_start = jnp.arange(n_blocks, dtype=jnp.int32) * TM_EXP
    bexp = jnp.minimum(jnp.sum(block_start[:, None] >= cum_padded[None, :], axis=1),
                       N_EXPERTS - 1).astype(jnp.int32)
    off = block_start[:, None] + jnp.arange(TM_EXP, dtype=jnp.int32)[None, :] - start_padded[bexp][:, None]
    valid = off < counts[bexp][:, None]
    a = order[jnp.clip(start[bexp][:, None] + off, 0, n_assign - 1)]
    brows = jnp.sum(valid, axis=1, dtype=jnp.int32)
    pad_rank = (jnp.cumsum((~valid).reshape(-1).astype(jnp.int32)) - 1).reshape(valid.shape)
    row_tok = jnp.where(valid, a % t, 0).astype(jnp.int32)
    row_dst = jnp.where(valid, a, n_assign + pad_rank).astype(jnp.int32)
    n_out_rows = n_blocks * TM_EXP
    return (bexp, brows, row_tok.reshape(n_blocks, 1, TM_EXP), row_dst.reshape(n_blocks, 1, TM_EXP),
            n_out_rows)


def _rope_tables(seq):
    pos = jnp.arange(seq, dtype=F32)[:, None]
    half_a = A_ROT_DIM // 2
    ang_a = pos * (ROPE_THETA ** (-jnp.arange(half_a, dtype=F32) / half_a))[None, :]
    one = jnp.ones((seq, HEAD_DIM - A_ROT_DIM), F32)
    cos_head = jnp.concatenate([jnp.cos(ang_a), jnp.cos(ang_a), one], axis=1)
    sin_head = jnp.concatenate([jnp.sin(ang_a), jnp.sin(ang_a), 0 * one], axis=1)
    cosa = jnp.tile(cos_head, (1, A_HEADS))
    sina = jnp.tile(sin_head, (1, A_HEADS))
    half_b = B_ROPE // 2
    ang_b = pos * (ROPE_THETA ** (-jnp.arange(half_b, dtype=F32) / half_b))[None, :]
    zl = jnp.zeros((seq, B_NOPE), F32)
    zr = jnp.zeros((seq, HEAD_PAD - B_NOPE - B_ROPE), F32)
    cosk = jnp.concatenate([zl, jnp.cos(ang_b), jnp.cos(ang_b), zr], axis=1)
    sink = jnp.concatenate([zl, jnp.sin(ang_b), jnp.sin(ang_b), zr], axis=1)
    return cosa, sina, cosk, sink, jnp.cos(ang_b).T, jnp.sin(ang_b).T


def _rotate_half_matrix():
    r = np.zeros((A_WIDTH, A_WIDTH), np.float32)
    half = A_ROT_DIM // 2
    for hd in range(A_HEADS):
        for i in range(half):
            r[hd * HEAD_DIM + half + i, hd * HEAD_DIM + i] = -1.0
            r[hd * HEAD_DIM + i, hd * HEAD_DIM + half + i] = 1.0
    return jnp.asarray(r, BF16)


def _dft_constants(seq):
    n = jnp.arange(seq, dtype=jnp.int32)
    ang = ((n[:, None] * n[None, :]) % seq).astype(F32) * (2.0 * np.pi / seq)
    scale = seq ** -0.5
    cmat = (jnp.cos(ang) * scale).astype(BF16)
    nsmat = (-jnp.sin(ang) * scale).astype(BF16)
    c = np.arange(C_GROUP_DIM)
    angc = 2.0 * np.pi * ((c[:, None] * c[None, :]) % C_GROUP_DIM) / C_GROUP_DIM
    eye = np.eye(C_GROUPS)
    cs = np.concatenate([np.kron(eye, np.cos(angc)), np.kron(eye, np.sin(angc))], axis=1)
    return cmat, nsmat, jnp.asarray(cs * C_GROUP_DIM ** -0.5, BF16)


def _layer_weights(w_in, w_uq, w_ukv, w_out):
    d = w_in.shape[0]
    c_kpe = 3 * A_WIDTH + Q_LORA + KV_LORA
    w_kpe = w_in[:, c_kpe:c_kpe + B_ROPE]
    half = B_ROPE // 2
    w_kpe_rot = jnp.concatenate([-w_kpe[:, half:], w_kpe[:, :half]], axis=1)
    zl = jnp.zeros((d, B_NOPE), w_in.dtype)
    zr = jnp.zeros((d, HEAD_PAD - B_NOPE - B_ROPE), w_in.dtype)
    wm = jnp.concatenate([w_in[:, :c_kpe], zl, w_kpe, zr, zl, w_kpe_rot, zr,
                          w_in[:, c_kpe + B_ROPE:]], axis=1).astype(BF16)
    uq = w_uq.reshape(Q_LORA, B_HEADS, B_NOPE + B_ROPE)
    uq = jnp.pad(uq, ((0, 0), (0, 0), (0, HEAD_PAD - B_NOPE - B_ROPE)))
    wuq_t = uq.reshape(Q_LORA, B_HEADS * HEAD_PAD).T.astype(BF16)
    ukv = w_ukv.reshape(KV_LORA, B_HEADS, B_NOPE + B_V)
    wkn = jnp.pad(ukv[:, :, :B_NOPE], ((0, 0), (0, 0), (0, HEAD_PAD - B_NOPE)))
    wkn = wkn.reshape(KV_LORA, B_HEADS * HEAD_PAD).astype(BF16)
    wv_t = ukv[:, :, B_NOPE:].reshape(KV_LORA, B_WIDTH).T.astype(BF16)
    wo = w_out.astype(BF16)
    return wm, wuq_t, wkn, wv_t, wo[:A_WIDTH], wo[A_WIDTH:A_WIDTH + B_WIDTH], wo[A_WIDTH + B_WIDTH:]


def kernel(x, attn_norm, w_in, q_norm, w_uq, kv_norm, w_ukv, mix_gain, w_out, ffn_norm,
           w_ffn_gate, w_ffn_up, w_ffn_down, w_router, w_exp_gate, w_exp_up, w_exp_down,
           final_norm):
    batch, seq, d = x.shape
    depth = w_in.shape[0]
    t = batch * seq
    assert seq % (max(DILATIONS) * 2 * WIN_SUB) == 0 and t % TM_EXP == 0
    tabs = _rope_tables(seq)
    rot = _rotate_half_matrix()
    cmat, nsmat, cs = _dft_constants(seq)
    row = lambda v: v.reshape(1, -1)

    x2 = x.reshape(t, d)
    for l in range(depth):
        wm, wuq_t, wkn, wv_t, wa, wb, wc = _layer_weights(w_in[l], w_uq[l], w_ukv[l], w_out[l])
        qa, ka, va, qbt, kb, vbt, xcs = _proj_call(
            x2, row(attn_norm[l]), wm, rot, tabs, row(q_norm[l]), wuq_t, row(kv_norm[l]), wkn, wv_t,
            cs, batch, seq)
        win = [_win_call(qa, ka, va, batch, seq, dil) for dil in DILATIONS]
        ob = _mla_call(qbt, kb, vbt, batch, seq)
        oc = _dft_call(cmat, nsmat, xcs, batch, seq)
        g = mix_gain[l]
        x2 = _out_call(x2, [w[0] for w in win], [w[1] for w in win], ob, oc,
                       row(g[:A_WIDTH]), row(g[A_WIDTH:A_WIDTH + B_WIDTH]), row(g[A_WIDTH + B_WIDTH:]),
                       wa, wb, wc)
        i = l // 2
        if l % 2 == 0:
            x2 = _ffn_call(x2, row(ffn_norm[l]), w_ffn_gate[i].astype(BF16), w_ffn_up[i].astype(BF16),
                           w_ffn_down[i].astype(BF16))
            if l == depth - 1:
                x2 = _final_norm_call(x2, row(final_norm))
        else:
            idx, gates = _router_call(x2, row(ffn_norm[l]), w_router[i].T)
            bexp, brows, row_tok, row_dst, n_out_rows = _route_plan(idx, t)
            y2 = _expert_call(x2, row(ffn_norm[l]), w_exp_gate[i].astype(BF16), w_exp_up[i].astype(BF16),
                              w_exp_down[i].astype(BF16), bexp, brows, row_tok, row_dst, n_out_rows)
            x2 = _combine_call(x2, y2, gates.T, row(final_norm), final=(l == depth - 1))
    return x2.reshape(batch, seq, d)
```

```python
import functools
import math

import jax
import jax.numpy as jnp
import numpy as np
from jax import lax
from jax.experimental import pallas as pl
from jax.experimental.pallas import tpu as pltpu

F32 = jnp.float32
BF16 = jnp.bfloat16

HEAD_DIM = 64
A_HEADS = 6
A_WIDTH = A_HEADS * HEAD_DIM
A_ROT_DIM = HEAD_DIM // 4
DILATIONS = (1, 4, 16)
WINDOW_RADIUS = 64
B_HEADS = 6
B_NOPE = 64
B_ROPE = 32
B_V = 64
B_WIDTH = B_HEADS * B_V
Q_LORA = 384
KV_LORA = 128
C_GROUPS = 4
C_GROUP_DIM = 64
C_WIDTH = C_GROUPS * C_GROUP_DIM
ROPE_THETA = 500000.0
N_EXPERTS = 8
TOP_K = 2
RMS_EPS = 1e-6
NEG_INF = -1e30

LANES = 128
BF16_SUBLANES = 16
HEAD_PAD = 128
VMEM_LIMIT_BYTES = 56 * 1024 * 1024

TM_PROJ = 512
TQ_WIN = 512
WIN_SUB = 128
TQ_MLA = 512
TK_MLA = 256
TM_DFT = 1024
TK_DFT = 1024
TM_FFN = 512
TM_ROUTER = 512
EXP_STEPS = 7
TM_EXP = 128 * EXP_STEPS


def _cparams(sem):
    return pltpu.CompilerParams(dimension_semantics=sem, vmem_limit_bytes=VMEM_LIMIT_BYTES)


def _rms(x, g):
    return x * lax.rsqrt(jnp.mean(x * x, axis=-1, keepdims=True) + RMS_EPS) * g


def _dot(a, b):
    return jnp.dot(a, b, preferred_element_type=F32)


def _dot_nt(a, b):
    return lax.dot_general(a, b, (((1,), (1,)), ((), ())), preferred_element_type=F32)


def _split_residues(scr, val, nat_ref, view_refs):
    tm = val.shape[0]
    nat_ref[...] = val.astype(BF16)
    for c in range(A_WIDTH // LANES):
        scr[c] = val[:, c * LANES:(c + 1) * LANES]
    for dil, ref in zip(DILATIONS[1:], view_refs):
        for r in range(dil):
            for c in range(A_WIDTH // LANES):
                col = r * A_WIDTH + c * LANES
                ref[0, :, col:col + LANES] = scr[c, pl.ds(r, tm // dil, stride=dil), :].astype(BF16)


def _proj_body(x_ref, g_ref, wm_ref, rot_ref, cosa_ref, sina_ref, cosk_ref, sink_ref,
               cosq_ref, sinq_ref, qn_ref, wuq_ref, kvn_ref, wkn_ref, wv_ref, cs_ref,
               qa_ref, qa4_ref, qa16_ref, ka_ref, ka4_ref, ka16_ref, va_ref, va4_ref, va16_ref,
               qbt_ref, kb_ref, vbt_ref, xcs_ref, scr):
    h = _rms(x_ref[...], g_ref[...]).astype(BF16)
    z = _dot(h, wm_ref[...])
    o = 0
    qa = z[:, o:o + A_WIDTH]; o += A_WIDTH
    ka = z[:, o:o + A_WIDTH]; o += A_WIDTH
    va = z[:, o:o + A_WIDTH]; o += A_WIDTH
    cq = z[:, o:o + Q_LORA]; o += Q_LORA
    ckv = z[:, o:o + KV_LORA]; o += KV_LORA
    kpe = z[:, o:o + HEAD_PAD]; o += HEAD_PAD
    kpe_rot = z[:, o:o + HEAD_PAD]; o += HEAD_PAD
    fc = z[:, o:o + C_WIDTH]

    cosa = cosa_ref[...]
    sina = sina_ref[...]
    rot = rot_ref[...]
    _split_residues(scr, qa * cosa + _dot(qa.astype(BF16), rot) * sina, qa_ref, (qa4_ref, qa16_ref))
    _split_residues(scr, ka * cosa + _dot(ka.astype(BF16), rot) * sina, ka_ref, (ka4_ref, ka16_ref))
    _split_residues(scr, va, va_ref, (va4_ref, va16_ref))

    cqn = _rms(cq, qn_ref[...]).astype(BF16)
    qbt = _dot_nt(wuq_ref[...], cqn) * ((B_NOPE + B_ROPE) ** -0.5 * math.log2(math.e))
    cosq = cosq_ref[...]
    sinq = sinq_ref[...]
    half = B_ROPE // 2
    for hd in range(B_HEADS):
        r0 = hd * HEAD_PAD
        x1 = qbt[r0 + B_NOPE:r0 + B_NOPE + half]
        x2 = qbt[r0 + B_NOPE + half:r0 + B_NOPE + B_ROPE]
        qbt_ref[0, r0:r0 + B_NOPE, :] = qbt[r0:r0 + B_NOPE].astype(BF16)
        qbt_ref[0, r0 + B_NOPE:r0 + B_NOPE + half, :] = (x1 * cosq - x2 * sinq).astype(BF16)
        qbt_ref[0, r0 + B_NOPE + half:r0 + B_NOPE + B_ROPE, :] = (x2 * cosq + x1 * sinq).astype(BF16)
        qbt_ref[0, r0 + B_NOPE + B_ROPE:r0 + HEAD_PAD, :] = jnp.zeros(
            (HEAD_PAD - B_NOPE - B_ROPE, qbt.shape[1]), BF16)

    ckvn = _rms(ckv, kvn_ref[...]).astype(BF16)
    kn = _dot(ckvn, wkn_ref[...])
    kpe_r = kpe * cosk_ref[...] + kpe_rot * sink_ref[...]
    for hd in range(B_HEADS):
        kb_ref[0, hd] = (kn[:, hd * HEAD_PAD:(hd + 1) * HEAD_PAD] + kpe_r).astype(BF16)
    vbt_ref[0] = _dot_nt(wv_ref[...], ckvn).astype(BF16)

    xcs_ref[...] = _dot(fc.astype(BF16), cs_ref[...]).astype(BF16)


def _proj_call(x2, g, wm, rot, tabs, qn, wuq, kvn, wkn, wv, cs, batch, seq):
    t, d = x2.shape
    tm = min(TM_PROJ, seq)
    ns = seq // tm
    nm = wm.shape[1]
    cosa, sina, cosk, sink, cosq, sinq = tabs
    row = lambda i: (i, 0)
    pos = lambda i: (i % ns, 0)
    post = lambda i: (0, i % ns)
    const = lambda i: (0, 0)
    in_specs = [
        pl.BlockSpec((tm, d), row),
        pl.BlockSpec((1, d), const),
        pl.BlockSpec((d, nm), const),
        pl.BlockSpec((A_WIDTH, A_WIDTH), const),
        pl.BlockSpec((tm, A_WIDTH), pos),
        pl.BlockSpec((tm, A_WIDTH), pos),
        pl.BlockSpec((tm, HEAD_PAD), pos),
        pl.BlockSpec((tm, HEAD_PAD), pos),
        pl.BlockSpec((B_ROPE // 2, tm), post),
        pl.BlockSpec((B_ROPE // 2, tm), post),
        pl.BlockSpec((1, Q_LORA), const),
        pl.BlockSpec((B_HEADS * HEAD_PAD, Q_LORA), const),
        pl.BlockSpec((1, KV_LORA), const),
        pl.BlockSpec((KV_LORA, B_HEADS * HEAD_PAD), const),
        pl.BlockSpec((B_WIDTH, KV_LORA), const),
        pl.BlockSpec((C_WIDTH, 2 * C_WIDTH), const),
    ]
    bpos = lambda i: (i // ns, i % ns, 0)
    a_shapes, a_specs = [], []
    for _ in range(3):
        a_shapes.append(jax.ShapeDtypeStruct((t, A_WIDTH), BF16))
        a_specs.append(pl.BlockSpec((tm, A_WIDTH), row))
        for dil in DILATIONS[1:]:
            a_shapes.append(jax.ShapeDtypeStruct((batch, seq // dil, dil * A_WIDTH), BF16))
            a_specs.append(pl.BlockSpec((1, tm // dil, dil * A_WIDTH), bpos))
    out_shape = (
        *a_shapes,
        jax.ShapeDtypeStruct((batch, B_HEADS * HEAD_PAD, seq), BF16),
        jax.ShapeDtypeStruct((batch, B_HEADS, seq, HEAD_PAD), BF16),
        jax.ShapeDtypeStruct((batch, B_WIDTH, seq), BF16),
        jax.ShapeDtypeStruct((t, 2 * C_WIDTH), BF16),
    )
    out_specs = (
        *a_specs,
        pl.BlockSpec((1, B_HEADS * HEAD_PAD, tm), lambda i: (i // ns, 0, i % ns)),
        pl.BlockSpec((1, B_HEADS, tm, HEAD_PAD), lambda i: (i // ns, 0, i % ns, 0)),
        pl.BlockSpec((1, B_WIDTH, tm), lambda i: (i // ns, 0, i % ns)),
        pl.BlockSpec((tm, 2 * C_WIDTH), row),
    )
    outs = pl.pallas_call(
        _proj_body, grid=(t // tm,), in_specs=in_specs, out_specs=out_specs, out_shape=out_shape,
        scratch_shapes=[pltpu.VMEM((A_WIDTH // LANES, tm, LANES), F32)],
        compiler_params=_cparams(("arbitrary",)), name="proj",
    )(x2, g, wm, rot, cosa, sina, cosk, sink, cosq, sinq, qn, wuq, kvn, wkn, wv, cs)
    nd = len(DILATIONS)
    qkv_views = [tuple(outs[a * nd + j] for a in range(3)) for j in range(nd)]
    return qkv_views, outs[3 * nd:]


def _win_body(q_ref, kp_ref, km_ref, kn_ref, vp_ref, vm_ref, vn_ref, o_ref, l_ref, *, seg_len, tq):
    j = pl.program_id(2)
    span = WIN_SUB + 2 * WINDOW_RADIUS
    lane = lax.broadcasted_iota(jnp.int32, (1, LANES), 1)
    lo_half = lane < HEAD_DIM
    rel = (lax.broadcasted_iota(jnp.int32, (WIN_SUB, span), 1) - WINDOW_RADIUS
           - lax.broadcasted_iota(jnp.int32, (WIN_SUB, span), 0))
    band = jnp.abs(rel) <= WINDOW_RADIUS
    kcol = lax.broadcasted_iota(jnp.int32, (1, span), 1)
    for pair in range(A_HEADS // 2):
        cs = slice(pair * LANES, (pair + 1) * LANES)
        q2 = q_ref[0, :, cs] * (HEAD_DIM ** -0.5)
        qh = (jnp.where(lo_half, q2, 0).astype(BF16), jnp.where(lo_half, 0, q2).astype(BF16))
        kcat = jnp.concatenate([kp_ref[0, :, cs], km_ref[0, :, cs], kn_ref[0, :, cs]], axis=0)
        vcat = jnp.concatenate([vp_ref[0, :, cs], vm_ref[0, :, cs], vn_ref[0, :, cs]], axis=0)
        for sb in range(tq // WIN_SUB):
            r0 = sb * WIN_SUB
            ks = kcat[r0:r0 + span]
            vs = vcat[r0:r0 + span]
            kpos = j * tq + r0 - WINDOW_RADIUS + kcol
            valid = band & (kpos >= 0) & (kpos < seg_len)
            outs, lses = [], []
            for hh in range(2):
                s = _dot_nt(qh[hh][r0:r0 + WIN_SUB], ks)
                s = jnp.where(valid, s, NEG_INF)
                m = jnp.max(s, axis=-1, keepdims=True)
                p = jnp.exp(s - m)
                den = jnp.sum(p, axis=-1, keepdims=True)
                outs.append(_dot(p.astype(BF16), vs) / den)
                lses.append(m + jnp.log(den))
            o_ref[0, r0:r0 + WIN_SUB, cs] = jnp.where(lo_half, outs[0], outs[1]).astype(BF16)
            l_ref[0, r0:r0 + WIN_SUB, cs] = jnp.where(lo_half, lses[0], lses[1])


def _win_call(q, k, v, batch, seq, dil):
    seg = seq // dil
    tq = min(TQ_WIN, seg)
    nb = tq // WINDOW_RADIUS
    last = seg // WINDOW_RADIUS - 1
    main = pl.BlockSpec((1, tq, A_WIDTH), lambda b, r, j: (b, j, r))
    prev = pl.BlockSpec((1, WINDOW_RADIUS, A_WIDTH), lambda b, r, j: (b, jnp.maximum(j * nb - 1, 0), r))
    nxt = pl.BlockSpec((1, WINDOW_RADIUS, A_WIDTH), lambda b, r, j: (b, jnp.minimum((j + 1) * nb, last), r))
    return pl.pallas_call(
        functools.partial(_win_body, seg_len=seg, tq=tq),
        grid=(batch, dil, seg // tq),
        in_specs=[main, prev, main, nxt, prev, main, nxt],
        out_specs=(main, main),
        out_shape=(jax.ShapeDtypeStruct((batch, seg, dil * A_WIDTH), BF16),
                   jax.ShapeDtypeStruct((batch, seg, dil * A_WIDTH), F32)),
        compiler_params=_cparams(("arbitrary", "arbitrary", "arbitrary")), name=f"win_d{dil}",
    )(q, k, k, k, v, v, v)


def _mla_body(qt_ref, k_ref, vt_ref, o_ref, sa, sb, pa, pb, acc_ref, *, seq, tk):
    tq = qt_ref.shape[2]
    nk = seq // tk
    qts = [qt_ref[0, h * HEAD_PAD:(h + 1) * HEAD_PAD, :] for h in range(B_HEADS)]
    ones = jnp.ones((BF16_SUBLANES, tk), BF16)

    def scores(h, kt, dst):
        start = pl.multiple_of(kt * tk, tk)
        dst[h] = _dot(k_ref[0, h, pl.ds(start, tk), :], qts[h])

    def values(h, kt, src, alpha):
        start = pl.multiple_of(kt * tk, tk)
        vt = jnp.concatenate([vt_ref[0, h * B_V:(h + 1) * B_V, pl.ds(start, tk)], ones], axis=0)
        acc_ref[h] = alpha * acc_ref[h] + _dot(vt, src[h])

    def softmax(h, src, dst, m):
        st = src[h]
        m_new = jnp.maximum(m, jnp.max(st, axis=0, keepdims=True))
        dst[h] = jnp.exp2(st - m_new).astype(BF16)
        return m_new, jnp.exp2(m - m_new)

    for h in range(B_HEADS):
        scores(h, 0, sa)
        acc_ref[h] = jnp.zeros((B_V + BF16_SUBLANES, tq), F32)
    pb[...] = jnp.zeros_like(pb)

    def step(i, carry):
        out = []
        for h in range(B_HEADS):
            m, alpha_prev = carry[h]
            t0 = 2 * i
            scores(h, t0 + 1, sb)
            values(h, jnp.maximum(t0 - 1, 0), pb, alpha_prev)
            m, alpha0 = softmax(h, sa, pa, m)
            scores(h, jnp.minimum(t0 + 2, nk - 1), sa)
            values(h, t0, pa, alpha0)
            m, alpha1 = softmax(h, sb, pb, m)
            out.append((m, alpha1))
        return tuple(out)

    init = tuple((jnp.full((1, tq), NEG_INF, F32), jnp.ones((1, tq), F32)) for _ in range(B_HEADS))
    res = lax.fori_loop(0, nk // 2, step, init)
    outs = []
    for h in range(B_HEADS):
        values(h, nk - 1, pb, res[h][1])
        acc = acc_ref[h]
        outs.append(acc[:B_V] / acc[B_V:B_V + 1])
    o_ref[...] = jnp.concatenate(outs, axis=0).T.astype(BF16)


def _mla_call(qbt, kb, vbt, batch, seq):
    tq = min(TQ_MLA, seq)
    tk = min(TK_MLA, seq)
    nq = seq // tq
    assert (seq // tk) % 2 == 0
    return pl.pallas_call(
        functools.partial(_mla_body, seq=seq, tk=tk),
        grid=(batch, nq),
        in_specs=[
            pl.BlockSpec((1, B_HEADS * HEAD_PAD, tq), lambda b, qi: (b, 0, qi)),
            pl.BlockSpec((1, B_HEADS, seq, HEAD_PAD), lambda b, qi: (b, 0, 0, 0)),
            pl.BlockSpec((1, B_WIDTH, seq), lambda b, qi: (b, 0, 0)),
        ],
        out_specs=pl.BlockSpec((tq, B_WIDTH), lambda b, qi: (b * nq + qi, 0)),
        out_shape=jax.ShapeDtypeStruct((batch * seq, B_WIDTH), BF16),
        scratch_shapes=[pltpu.VMEM((B_HEADS, tk, tq), F32), pltpu.VMEM((B_HEADS, tk, tq), F32),
                        pltpu.VMEM((B_HEADS, tk, tq), BF16), pltpu.VMEM((B_HEADS, tk, tq), BF16),
                        pltpu.VMEM((B_HEADS, B_V + BF16_SUBLANES, tq), F32)],
        compiler_params=_cparams(("arbitrary", "arbitrary")), name="mla_attn",
    )(qbt, kb, vbt)


def _dft_body(c_ref, ns_ref, x_ref, o_ref, acc_ref):
    k = pl.program_id(1)

    @pl.when(k == 0)
    def _():
        acc_ref[...] = jnp.zeros_like(acc_ref)

    c = c_ref[...]
    ns = ns_ref[...]
    for b in range(x_ref.shape[0]):
        acc_ref[b] += _dot(c, x_ref[b, :, 0:C_WIDTH]) + _dot(ns, x_ref[b, :, C_WIDTH:2 * C_WIDTH])

    @pl.when(k == pl.num_programs(1) - 1)
    def _():
        o_ref[...] = acc_ref[...].astype(o_ref.dtype)


def _dft_call(cmat, nsmat, xcs, batch, seq):
    tm = min(TM_DFT, seq)
    tk = min(TK_DFT, seq)
    x3 = xcs.reshape(batch, seq, 2 * C_WIDTH)
    y = pl.pallas_call(
        _dft_body, grid=(seq // tm, seq // tk),
        in_specs=[pl.BlockSpec((tm, tk), lambda i, k: (i, k)),
                  pl.BlockSpec((tm, tk), lambda i, k: (i, k)),
                  pl.BlockSpec((batch, tk, 2 * C_WIDTH), lambda i, k: (0, k, 0))],
        out_specs=pl.BlockSpec((batch, tm, C_WIDTH), lambda i, k: (0, i, 0)),
        out_shape=jax.ShapeDtypeStruct((batch, seq, C_WIDTH), BF16),
        scratch_shapes=[pltpu.VMEM((batch, tm, C_WIDTH), F32)],
        compiler_params=_cparams(("arbitrary", "arbitrary")), name="seq_dft",
    )(cmat, nsmat, x3)
    return y.reshape(batch * seq, C_WIDTH)


def _merge_residues(scr, ref, dil):
    tm = scr.shape[1]
    for r in range(dil):
        for c in range(A_WIDTH // LANES):
            col = r * A_WIDTH + c * LANES
            scr[c, pl.ds(r, tm // dil, stride=dil), :] = ref[0, :, col:col + LANES].astype(F32)
    return jnp.concatenate([scr[c] for c in range(A_WIDTH // LANES)], axis=1)


def _out_body(x_ref, o1_ref, o4_ref, o16_ref, l1_ref, l4_ref, l16_ref, ob_ref, oc_ref,
              ga_ref, gb_ref, gc_ref, wa_ref, wb_ref, wc_ref, y_ref, so4, sl4, so16, sl16):
    l1 = l1_ref[0]
    l4 = _merge_residues(sl4, l4_ref, DILATIONS[1])
    l16 = _merge_residues(sl16, l16_ref, DILATIONS[2])
    o4 = _merge_residues(so4, o4_ref, DILATIONS[1])
    o16 = _merge_residues(so16, o16_ref, DILATIONS[2])
    m = jnp.maximum(jnp.maximum(l1, l4), l16)
    e1, e4, e16 = jnp.exp(l1 - m), jnp.exp(l4 - m), jnp.exp(l16 - m)
    oa = (o1_ref[0].astype(F32) * e1 + o4 * e4 + o16 * e16) / (e1 + e4 + e16)
    na = _rms(oa, ga_ref[...]).astype(BF16)
    nb = _rms(ob_ref[...].astype(F32), gb_ref[...]).astype(BF16)
    nc = _rms(oc_ref[...].astype(F32), gc_ref[...]).astype(BF16)
    y_ref[...] = (x_ref[...] + _dot(na, wa_ref[...]) + _dot(nb, wb_ref[...]) + _dot(nc, wc_ref[...]))


def _out_call(x2, oa, la, ob, oc, ga, gb, gc, wa, wb, wc, batch, seq):
    t, d = x2.shape
    tm = min(TM_PROJ, seq)
    ns = seq // tm
    row = lambda i: (i, 0)
    const = lambda i: (0, 0)
    wide = lambda w: pl.BlockSpec((tm, w), row)
    view = lambda dil: pl.BlockSpec((1, tm // dil, dil * A_WIDTH), lambda i: (i // ns, i % ns, 0))
    views = [view(dil) for dil in DILATIONS]
    slab = pltpu.VMEM((A_WIDTH // LANES, tm, LANES), F32)
    return pl.pallas_call(
        _out_body, grid=(t // tm,),
        in_specs=[wide(d)] + views + views + [wide(B_WIDTH), wide(C_WIDTH),
                  pl.BlockSpec((1, A_WIDTH), const), pl.BlockSpec((1, B_WIDTH), const),
                  pl.BlockSpec((1, C_WIDTH), const),
                  pl.BlockSpec((A_WIDTH, d), const), pl.BlockSpec((B_WIDTH, d), const),
                  pl.BlockSpec((C_WIDTH, d), const)],
        out_specs=wide(d),
        out_shape=jax.ShapeDtypeStruct((t, d), F32),
        scratch_shapes=[slab, slab, slab, slab],
        compiler_params=_cparams(("arbitrary",)), name="out_proj",
    )(x2, oa[0], oa[1], oa[2], la[0], la[1], la[2], ob, oc, ga, gb, gc, wa, wb, wc)


def _ffn_body(x_ref, g_ref, wg_ref, wu_ref, wd_ref, y_ref, h_ref, acc_ref):
    f = pl.program_id(1)

    @pl.when(f == 0)
    def _():
        h_ref[...] = _rms(x_ref[...], g_ref[...]).astype(BF16)
        acc_ref[...] = jnp.zeros_like(acc_ref)

    h = h_ref[...]
    gate = _dot(h, wg_ref[...])
    up = _dot(h, wu_ref[...])
    acc_ref[...] += _dot((jax.nn.silu(gate) * up).astype(BF16), wd_ref[...])

    @pl.when(f == pl.num_programs(1) - 1)
    def _():
        y_ref[...] = x_ref[...] + acc_ref[...]


def _ffn_tile(d_ff):
    for steps in (1, 2, 4, 11, 22):
        if d_ff % steps == 0 and (d_ff // steps) % LANES == 0 and d_ff // steps <= 1536:
            return d_ff // steps
    return d_ff


def _ffn_call(x2, g, wg, wu, wd):
    t, d = x2.shape
    d_ff = wg.shape[1]
    tm = min(TM_FFN, t)
    tf = _ffn_tile(d_ff)
    return pl.pallas_call(
        _ffn_body, grid=(t // tm, d_ff // tf),
        in_specs=[pl.BlockSpec((tm, d), lambda i, f: (i, 0)),
                  pl.BlockSpec((1, d), lambda i, f: (0, 0)),
                  pl.BlockSpec((d, tf), lambda i, f: (0, f)),
                  pl.BlockSpec((d, tf), lambda i, f: (0, f)),
                  pl.BlockSpec((tf, d), lambda i, f: (f, 0))],
        out_specs=pl.BlockSpec((tm, d), lambda i, f: (i, 0)),
        out_shape=jax.ShapeDtypeStruct((t, d), F32),
        scratch_shapes=[pltpu.VMEM((tm, d), BF16), pltpu.VMEM((tm, d), F32)],
        compiler_params=_cparams(("arbitrary", "arbitrary")), name="dense_ffn",
    )(x2, g, wg, wu, wd)


def _router_body(x_ref, g_ref, wr_ref, idx_ref, gate_ref):
    h = _rms(x_ref[...], g_ref[...])
    logits = lax.dot_general(wr_ref[...], h, (((1,), (1,)), ((), ())),
                             precision=lax.Precision.HIGHEST, preferred_element_type=F32)
    eid = lax.broadcasted_iota(jnp.int32, logits.shape, 0)
    m1 = jnp.max(logits, axis=0, keepdims=True)
    i1 = jnp.min(jnp.where(logits == m1, eid, N_EXPERTS), axis=0, keepdims=True)
    rest = jnp.where(eid == i1, -jnp.inf, logits)
    m2 = jnp.max(rest, axis=0, keepdims=True)
    i2 = jnp.min(jnp.where(rest == m2, eid, N_EXPERTS), axis=0, keepdims=True)
    e2 = jnp.exp(m2 - m1)
    idx_ref[...] = jnp.concatenate([i1, i2], axis=0)
    gate_ref[...] = jnp.concatenate([1.0 / (1.0 + e2), e2 / (1.0 + e2)], axis=0)


def _router_call(x2, g, wr_t):
    t, d = x2.shape
    tm = min(TM_ROUTER, t)
    return pl.pallas_call(
        _router_body, grid=(t // tm,),
        in_specs=[pl.BlockSpec((tm, d), lambda i: (i, 0)),
                  pl.BlockSpec((1, d), lambda i: (0, 0)),
                  pl.BlockSpec((N_EXPERTS, d), lambda i: (0, 0))],
        out_specs=(pl.BlockSpec((TOP_K, tm), lambda i: (0, i)),
                   pl.BlockSpec((TOP_K, tm), lambda i: (0, i))),
        out_shape=(jax.ShapeDtypeStruct((TOP_K, t), jnp.int32),
                   jax.ShapeDtypeStruct((TOP_K, t), F32)),
        compiler_params=_cparams(("arbitrary",)), name="router",
    )(x2, g, wr_t)


def _expert_body(bexp_ref, tok_cur, tok_next, dst_prev, dst_cur, x_hbm, g_ref, wg_ref, wu_ref, wd_ref,
                 y_hbm, xb0, xb1, hbuf, ac0, ac1, gsem, ssem):
    i = pl.program_id(0)
    f = pl.program_id(1)
    nb = pl.num_programs(0)
    nf = pl.num_programs(1)
    rows = xb0.shape[0]
    chunk = rows // EXP_STEPS

    def variant(xcur, xnxt, acur, aprv, s):
        def row_in(r, src_row, buf, sem):
            return pltpu.make_async_copy(x_hbm.at[pl.ds(src_row, 1)], buf.at[pl.ds(r, 1)], sem)

        def row_out(r, dst_row, buf, sem):
            return pltpu.make_async_copy(buf.at[pl.ds(r, 1)], y_hbm.at[pl.ds(dst_row, 1)], sem)

        def all_rows(fn):
            def body(r, c):
                fn(r)
                return c
            lax.fori_loop(0, rows, body, 0, unroll=8)

        @pl.when(f == 0)
        def _():
            @pl.when(i == 0)
            def _():
                all_rows(lambda r: row_in(r, tok_cur[0, 0, r], xcur, gsem.at[s]).start())
                aprv[...] = jnp.zeros_like(aprv)

            all_rows(lambda r: row_in(r, 0, xcur, gsem.at[s]).wait())
            hbuf[...] = _rms(xcur[...], g_ref[...]).astype(BF16)
            acur[...] = jnp.zeros_like(acur)

        base = pl.multiple_of(f * chunk, chunk)
        for k in range(chunk):
            r = base + k
            row_in(r, tok_next[0, 0, r], xnxt, gsem.at[1 - s]).start()
            row_out(r, dst_prev[0, 0, r], aprv, ssem.at[1 - s]).start()
        h = hbuf[...]
        gate = _dot(h, wg_ref[0])
        up = _dot(h, wu_ref[0])
        acur[...] += _dot((jax.nn.silu(gate) * up).astype(BF16), wd_ref[0])

        @pl.when(f == nf - 1)
        def _():
            all_rows(lambda r: row_out(r, 0, aprv, ssem.at[1 - s]).wait())

            @pl.when(i == nb - 1)
            def _():
                all_rows(lambda r: row_out(r, dst_cur[0, 0, r], acur, ssem.at[s]).start())
                all_rows(lambda r: row_out(r, 0, acur, ssem.at[s]).wait())
                all_rows(lambda r: row_in(r, 0, xnxt, gsem.at[1 - s]).wait())

    @pl.when(i % 2 == 0)
    def _():
        variant(xb0, xb1, ac0, ac1, 0)

    @pl.when(i % 2 == 1)
    def _():
        variant(xb1, xb0, ac1, ac0, 1)


def _expert_call(x2, g, wg, wu, wd, bexp, row_tok, row_dst, n_out_rows):
    t, d = x2.shape
    n_blocks = bexp.shape[0]
    d_ff = wg.shape[2]
    assert d_ff % EXP_STEPS == 0 and (d_ff // EXP_STEPS) % LANES == 0
    tf = d_ff // EXP_STEPS
    last = n_blocks - 1
    rows_of = lambda which: pl.BlockSpec((1, 1, TM_EXP), which, memory_space=pltpu.SMEM)
    grid_spec = pltpu.PrefetchScalarGridSpec(
        num_scalar_prefetch=1,
        grid=(n_blocks, EXP_STEPS),
        in_specs=[
            rows_of(lambda i, f, be: (i, 0, 0)),
            rows_of(lambda i, f, be: (jnp.minimum(i + 1, last), 0, 0)),
            rows_of(lambda i, f, be: (jnp.maximum(i - 1, 0), 0, 0)),
            rows_of(lambda i, f, be: (i, 0, 0)),
            pl.BlockSpec(memory_space=pl.ANY),
            pl.BlockSpec((1, d), lambda i, f, be: (0, 0)),
            pl.BlockSpec((1, d, tf), lambda i, f, be: (be[i], 0, f)),
            pl.BlockSpec((1, d, tf), lambda i, f, be: (be[i], 0, f)),
            pl.BlockSpec((1, tf, d), lambda i, f, be: (be[i], f, 0)),
        ],
        out_specs=pl.BlockSpec(memory_space=pl.ANY),
        scratch_shapes=[pltpu.VMEM((TM_EXP, d), F32), pltpu.VMEM((TM_EXP, d), F32),
                        pltpu.VMEM((TM_EXP, d), BF16),
                        pltpu.VMEM((TM_EXP, d), F32), pltpu.VMEM((TM_EXP, d), F32),
                        pltpu.SemaphoreType.DMA((2,)), pltpu.SemaphoreType.DMA((2,))],
    )
    return pl.pallas_call(
        _expert_body, grid_spec=grid_spec,
        out_shape=jax.ShapeDtypeStruct((n_out_rows, d), F32),
        compiler_params=_cparams(("arbitrary", "arbitrary")), name="experts",
    )(bexp, row_tok, row_tok, row_dst, row_dst, x2, g, wg, wu, wd)


def _combine_body(x_ref, y0_ref, y1_ref, gate_ref, gf_ref, o_ref, *, final):
    g0 = gate_ref[:, 0:1]
    g1 = gate_ref[:, 1:2]
    y = x_ref[...] + (y0_ref[...] * g0 + y1_ref[...] * g1)
    o_ref[...] = _rms(y, gf_ref[...]) if final else y


def _combine_call(x2, y2, gates_t, gfinal, final):
    t, d = x2.shape
    tm = min(TM_ROUTER, t)
    nt = t // tm
    return pl.pallas_call(
        functools.partial(_combine_body, final=final), grid=(nt,),
        in_specs=[pl.BlockSpec((tm, d), lambda i: (i, 0)),
                  pl.BlockSpec((tm, d), lambda i: (i, 0)),
                  pl.BlockSpec((tm, d), lambda i: (i + nt, 0)),
                  pl.BlockSpec((tm, TOP_K), lambda i: (i, 0)),
                  pl.BlockSpec((1, d), lambda i: (0, 0))],
        out_specs=pl.BlockSpec((tm, d), lambda i: (i, 0)),
        out_shape=jax.ShapeDtypeStruct((t, d), F32),
        compiler_params=_cparams(("arbitrary",)), name="moe_combine",
    )(x2, y2, y2, gates_t, gfinal)


def _final_norm_body(x_ref, g_ref, o_ref):
    o_ref[...] = _rms(x_ref[...], g_ref[...])


def _final_norm_call(x2, g):
    t, d = x2.shape
    tm = min(TM_ROUTER, t)
    return pl.pallas_call(
        _final_norm_body, grid=(t // tm,),
        in_specs=[pl.BlockSpec((tm, d), lambda i: (i, 0)), pl.BlockSpec((1, d), lambda i: (0, 0))],
        out_specs=pl.BlockSpec((tm, d), lambda i: (i, 0)),
        out_shape=jax.ShapeDtypeStruct((t, d), F32),
        compiler_params=_cparams(("arbitrary",)), name="final_norm",
    )(x2, g)


def _route_plan(idx, t):
    n_assign = TOP_K * t
    flat_e = idx.reshape(-1)
    order = jnp.argsort(flat_e).astype(jnp.int32)
    counts = jnp.sum(flat_e[:, None] == jnp.arange(N_EXPERTS, dtype=jnp.int32)[None, :], axis=0,
                     dtype=jnp.int32)
    padded = ((counts + TM_EXP - 1) // TM_EXP) * TM_EXP
    start = jnp.cumsum(counts) - counts
    cum_padded = jnp.cumsum(padded)
    start_padded = cum_padded - padded
    n_blocks = -(-n_assign // TM_EXP) + N_EXPERTS
    block_start = jnp.arange(n_blocks, dtype=jnp.int32) * TM_EXP
    bexp = jnp.minimum(jnp.sum(block_start[:, None] >= cum_padded[None, :], axis=1),
                       N_EXPERTS - 1).astype(jnp.int32)
    off = block_start[:, None] + jnp.arange(TM_EXP, dtype=jnp.int32)[None, :] - start_padded[bexp][:, None]
    valid = off < counts[bexp][:, None]
    a = order[jnp.clip(start[bexp][:, None] + off, 0, n_assign - 1)]
    pad_rank = (jnp.cumsum((~valid).reshape(-1).astype(jnp.int32)) - 1).reshape(valid.shape)
    row_tok = jnp.where(valid, a % t, 0).astype(jnp.int32)
    row_dst = jnp.where(valid, a, n_assign + pad_rank).astype(jnp.int32)
    n_out_rows = n_blocks * TM_EXP
    return bexp, row_tok.reshape(n_blocks, 1, TM_EXP), row_dst.reshape(n_blocks, 1, TM_EXP), n_out_rows


def _rope_tables(seq):
    pos = jnp.arange(seq, dtype=F32)[:, None]
    half_a = A_ROT_DIM // 2
    ang_a = pos * (ROPE_THETA ** (-jnp.arange(half_a, dtype=F32) / half_a))[None, :]
    one = jnp.ones((seq, HEAD_DIM - A_ROT_DIM), F32)
    cos_head = jnp.concatenate([jnp.cos(ang_a), jnp.cos(ang_a), one], axis=1)
    sin_head = jnp.concatenate([jnp.sin(ang_a), jnp.sin(ang_a), 0 * one], axis=1)
    cosa = jnp.tile(cos_head, (1, A_HEADS))
    sina = jnp.tile(sin_head, (1, A_HEADS))
    half_b = B_ROPE // 2
    ang_b = pos * (ROPE_THETA ** (-jnp.arange(half_b, dtype=F32) / half_b))[None, :]
    zl = jnp.zeros((seq, B_NOPE), F32)
    zr = jnp.zeros((seq, HEAD_PAD - B_NOPE - B_ROPE), F32)
    cosk = jnp.concatenate([zl, jnp.cos(ang_b), jnp.cos(ang_b), zr], axis=1)
    sink = jnp.concatenate([zl, jnp.sin(ang_b), jnp.sin(ang_b), zr], axis=1)
    return cosa, sina, cosk, sink, jnp.cos(ang_b).T, jnp.sin(ang_b).T


def _rotate_half_matrix():
    r = np.zeros((A_WIDTH, A_WIDTH), np.float32)
    half = A_ROT_DIM // 2
    for hd in range(A_HEADS):
        for i in range(half):
            r[hd * HEAD_DIM + half + i, hd * HEAD_DIM + i] = -1.0
            r[hd * HEAD_DIM + i, hd * HEAD_DIM + half + i] = 1.0
    return jnp.asarray(r, BF16)


def _dft_constants(seq):
    lo_n = 64 if seq % 64 == 0 else 1
    hi_n = seq // lo_n
    k = jnp.arange(seq, dtype=jnp.int32)[:, None]
    ang_hi = ((k * jnp.arange(hi_n, dtype=jnp.int32)[None, :]) % hi_n).astype(F32) * (2.0 * np.pi / hi_n)
    ang_lo = ((k * jnp.arange(lo_n, dtype=jnp.int32)[None, :]) % seq).astype(F32) * (2.0 * np.pi / seq)
    ch, sh = jnp.cos(ang_hi)[:, :, None], jnp.sin(ang_hi)[:, :, None]
    cl, sl = jnp.cos(ang_lo)[:, None, :], jnp.sin(ang_lo)[:, None, :]
    scale = seq ** -0.5
    cmat = ((ch * cl - sh * sl) * scale).astype(BF16).reshape(seq, seq)
    nsmat = ((sh * cl + ch * sl) * -scale).astype(BF16).reshape(seq, seq)
    c = np.arange(C_GROUP_DIM)
    angc = 2.0 * np.pi * ((c[:, None] * c[None, :]) % C_GROUP_DIM) / C_GROUP_DIM
    eye = np.eye(C_GROUPS)
    cs = np.concatenate([np.kron(eye, np.cos(angc)), np.kron(eye, np.sin(angc))], axis=1)
    return cmat, nsmat, jnp.asarray(cs * C_GROUP_DIM ** -0.5, BF16)


def _layer_weights(w_in, w_uq, w_ukv, w_out):
    d = w_in.shape[0]
    c_kpe = 3 * A_WIDTH + Q_LORA + KV_LORA
    w_kpe = w_in[:, c_kpe:c_kpe + B_ROPE]
    half = B_ROPE // 2
    w_kpe_rot = jnp.concatenate([-w_kpe[:, half:], w_kpe[:, :half]], axis=1)
    zl = jnp.zeros((d, B_NOPE), w_in.dtype)
    zr = jnp.zeros((d, HEAD_PAD - B_NOPE - B_ROPE), w_in.dtype)
    wm = jnp.concatenate([w_in[:, :c_kpe], zl, w_kpe, zr, zl, w_kpe_rot, zr,
                          w_in[:, c_kpe + B_ROPE:]], axis=1).astype(BF16)
    uq = w_uq.reshape(Q_LORA, B_HEADS, B_NOPE + B_ROPE)
    uq = jnp.pad(uq, ((0, 0), (0, 0), (0, HEAD_PAD - B_NOPE - B_ROPE)))
    wuq_t = uq.reshape(Q_LORA, B_HEADS * HEAD_PAD).T.astype(BF16)
    ukv = w_ukv.reshape(KV_LORA, B_HEADS, B_NOPE + B_V)
    wkn = jnp.pad(ukv[:, :, :B_NOPE], ((0, 0), (0, 0), (0, HEAD_PAD - B_NOPE)))
    wkn = wkn.reshape(KV_LORA, B_HEADS * HEAD_PAD).astype(BF16)
    wv_t = ukv[:, :, B_NOPE:].reshape(KV_LORA, B_WIDTH).T.astype(BF16)
    wo = w_out.astype(BF16)
    return wm, wuq_t, wkn, wv_t, wo[:A_WIDTH], wo[A_WIDTH:A_WIDTH + B_WIDTH], wo[A_WIDTH + B_WIDTH:]


def kernel(x, attn_norm, w_in, q_norm, w_uq, kv_norm, w_ukv, mix_gain, w_out, ffn_norm,
           w_ffn_gate, w_ffn_up, w_ffn_down, w_router, w_exp_gate, w_exp_up, w_exp_down,
           final_norm):
    batch, seq, d = x.shape
    depth = w_in.shape[0]
    t = batch * seq
    assert seq % (max(DILATIONS) * 2 * WIN_SUB) == 0
    tabs = _rope_tables(seq)
    rot = _rotate_half_matrix()
    cmat, nsmat, cs = _dft_constants(seq)
    row = lambda v: v.reshape(1, -1)

    x2 = x.reshape(t, d)
    for l in range(depth):
        wm, wuq_t, wkn, wv_t, wa, wb, wc = _layer_weights(w_in[l], w_uq[l], w_ukv[l], w_out[l])
        qkv_views, (qbt, kb, vbt, xcs) = _proj_call(
            x2, row(attn_norm[l]), wm, rot, tabs, row(q_norm[l]), wuq_t, row(kv_norm[l]), wkn, wv_t,
            cs, batch, seq)
        win = [_win_call(*(a.reshape(batch, seq // dil, dil * A_WIDTH) for a in qkv), batch, seq, dil)
               for dil, qkv in zip(DILATIONS, qkv_views)]
        ob = _mla_call(qbt, kb, vbt, batch, seq)
        oc = _dft_call(cmat, nsmat, xcs, batch, seq)
        g = mix_gain[l]
        x2 = _out_call(x2, [w[0] for w in win], [w[1] for w in win], ob, oc,
                       row(g[:A_WIDTH]), row(g[A_WIDTH:A_WIDTH + B_WIDTH]), row(g[A_WIDTH + B_WIDTH:]),
                       wa, wb, wc, batch, seq)
        i = l // 2
        if l % 2 == 0:
            x2 = _ffn_call(x2, row(ffn_norm[l]), w_ffn_gate[i].astype(BF16), w_ffn_up[i].astype(BF16),
                           w_ffn_down[i].astype(BF16))
            if l == depth - 1:
                x2 = _final_norm_call(x2, row(final_norm))
        else:
            idx, gates = _router_call(x2, row(ffn_norm[l]), w_router[i].T)
            bexp, row_tok, row_dst, n_out_rows = _route_plan(idx, t)
            y2 = _expert_call(x2, row(ffn_norm[l]), w_exp_gate[i].astype(BF16), w_exp_up[i].astype(BF16),
                              w_exp_down[i].astype(BF16), bexp, row_tok, row_dst, n_out_rows)
            x2 = _combine_call(x2, y2, gates.T, row(final_norm), final=(l == depth - 1))
    return x2.reshape(batch, seq, d)
```

```python
import functools
import math

import jax
import jax.numpy as jnp
import numpy as np
from jax import lax
from jax.experimental import pallas as pl
from jax.experimental.pallas import tpu as pltpu

F32 = jnp.float32
BF16 = jnp.bfloat16

HEAD_DIM = 64
A_HEADS = 6
A_WIDTH = A_HEADS * HEAD_DIM
A_ROT_DIM = HEAD_DIM // 4
DILATIONS = (1, 4, 16)
WINDOW_RADIUS = 64
B_HEADS = 6
B_NOPE = 64
B_ROPE = 32
B_V = 64
B_WIDTH = B_HEADS * B_V
Q_LORA = 384
KV_LORA = 128
C_GROUPS = 4
C_GROUP_DIM = 64
C_WIDTH = C_GROUPS * C_GROUP_DIM
ROPE_THETA = 500000.0
N_EXPERTS = 8
TOP_K = 2
RMS_EPS = 1e-6
NEG_INF = -1e30

LANES = 128
BF16_SUBLANES = 16
HEAD_PAD = 128
VMEM_LIMIT_BYTES = 56 * 1024 * 1024

TM_PROJ = 512
TQ_WIN = 512
WIN_SUB = 128
TQ_MLA = 512
TK_MLA = 256
TM_DFT = 1024
TK_DFT = 512
TM_FFN = 512
TM_ROUTER = 512
EXP_STEPS = 7
TM_EXP = 128 * EXP_STEPS


def _cparams(sem):
    return pltpu.CompilerParams(dimension_semantics=sem, vmem_limit_bytes=VMEM_LIMIT_BYTES)


def _rms(x, g):
    return x * lax.rsqrt(jnp.mean(x * x, axis=-1, keepdims=True) + RMS_EPS) * g


def _dot(a, b):
    return jnp.dot(a, b, preferred_element_type=F32)


def _dot_nt(a, b):
    return lax.dot_general(a, b, (((1,), (1,)), ((), ())), preferred_element_type=F32)


def _split_residues(scr, val, nat_ref, view_refs):
    tm = val.shape[0]
    nat_ref[...] = val.astype(BF16)
    for c in range(A_WIDTH // LANES):
        scr[c] = val[:, c * LANES:(c + 1) * LANES]
    for dil, ref in zip(DILATIONS[1:], view_refs):
        for r in range(dil):
            for c in range(A_WIDTH // LANES):
                col = r * A_WIDTH + c * LANES
                ref[0, :, col:col + LANES] = scr[c, pl.ds(r, tm // dil, stride=dil), :].astype(BF16)


def _proj_body(x_ref, g_ref, wm_ref, rot_ref, cosa_ref, sina_ref, cosk_ref, sink_ref,
               cosq_ref, sinq_ref, qn_ref, wuq_ref, kvn_ref, wkn_ref, wv_ref, cs_ref,
               qa_ref, qa4_ref, qa16_ref, ka_ref, ka4_ref, ka16_ref, va_ref, va4_ref, va16_ref,
               qbt_ref, kb_ref, vbt_ref, xcs_ref, scr):
    h = _rms(x_ref[...], g_ref[...]).astype(BF16)
    z = _dot(h, wm_ref[...])
    o = 0
    qa = z[:, o:o + A_WIDTH]; o += A_WIDTH
    ka = z[:, o:o + A_WIDTH]; o += A_WIDTH
    va = z[:, o:o + A_WIDTH]; o += A_WIDTH
    cq = z[:, o:o + Q_LORA]; o += Q_LORA
    ckv = z[:, o:o + KV_LORA]; o += KV_LORA
    kpe = z[:, o:o + HEAD_PAD]; o += HEAD_PAD
    kpe_rot = z[:, o:o + HEAD_PAD]; o += HEAD_PAD
    fc = z[:, o:o + C_WIDTH]

    cosa = cosa_ref[...]
    sina = sina_ref[...]
    rot = rot_ref[...]
    _split_residues(scr, qa * cosa + _dot(qa.astype(BF16), rot) * sina, qa_ref, (qa4_ref, qa16_ref))
    _split_residues(scr, ka * cosa + _dot(ka.astype(BF16), rot) * sina, ka_ref, (ka4_ref, ka16_ref))
    _split_residues(scr, va, va_ref, (va4_ref, va16_ref))

    cqn = _rms(cq, qn_ref[...]).astype(BF16)
    qbt = _dot_nt(wuq_ref[...], cqn) * ((B_NOPE + B_ROPE) ** -0.5 * math.log2(math.e))
    cosq = cosq_ref[...]
    sinq = sinq_ref[...]
    half = B_ROPE // 2
    for hd in range(B_HEADS):
        r0 = hd * HEAD_PAD
        x1 = qbt[r0 + B_NOPE:r0 + B_NOPE + half]
        x2 = qbt[r0 + B_NOPE + half:r0 + B_NOPE + B_ROPE]
        qbt_ref[0, r0:r0 + B_NOPE, :] = qbt[r0:r0 + B_NOPE].astype(BF16)
        qbt_ref[0, r0 + B_NOPE:r0 + B_NOPE + half, :] = (x1 * cosq - x2 * sinq).astype(BF16)
        qbt_ref[0, r0 + B_NOPE + half:r0 + B_NOPE + B_ROPE, :] = (x2 * cosq + x1 * sinq).astype(BF16)
        qbt_ref[0, r0 + B_NOPE + B_ROPE:r0 + HEAD_PAD, :] = jnp.zeros(
            (HEAD_PAD - B_NOPE - B_ROPE, qbt.shape[1]), BF16)

    ckvn = _rms(ckv, kvn_ref[...]).astype(BF16)
    kn = _dot(ckvn, wkn_ref[...])
    kpe_r = kpe * cosk_ref[...] + kpe_rot * sink_ref[...]
    for hd in range(B_HEADS):
        kb_ref[0, hd] = (kn[:, hd * HEAD_PAD:(hd + 1) * HEAD_PAD] + kpe_r).astype(BF16)
    vbt_ref[0] = _dot_nt(wv_ref[...], ckvn).astype(BF16)

    xcs_ref[...] = _dot(fc.astype(BF16), cs_ref[...]).astype(BF16)


def _proj_call(x2, g, wm, rot, tabs, qn, wuq, kvn, wkn, wv, cs, batch, seq):
    t, d = x2.shape
    tm = min(TM_PROJ, seq)
    ns = seq // tm
    nm = wm.shape[1]
    cosa, sina, cosk, sink, cosq, sinq = tabs
    row = lambda i: (i, 0)
    pos = lambda i: (i % ns, 0)
    post = lambda i: (0, i % ns)
    const = lambda i: (0, 0)
    in_specs = [
        pl.BlockSpec((tm, d), row),
        pl.BlockSpec((1, d), const),
        pl.BlockSpec((d, nm), const),
        pl.BlockSpec((A_WIDTH, A_WIDTH), const),
        pl.BlockSpec((tm, A_WIDTH), pos),
        pl.BlockSpec((tm, A_WIDTH), pos),
        pl.BlockSpec((tm, HEAD_PAD), pos),
        pl.BlockSpec((tm, HEAD_PAD), pos),
        pl.BlockSpec((B_ROPE // 2, tm), post),
        pl.BlockSpec((B_ROPE // 2, tm), post),
        pl.BlockSpec((1, Q_LORA), const),
        pl.BlockSpec((B_HEADS * HEAD_PAD, Q_LORA), const),
        pl.BlockSpec((1, KV_LORA), const),
        pl.BlockSpec((KV_LORA, B_HEADS * HEAD_PAD), const),
        pl.BlockSpec((B_WIDTH, KV_LORA), const),
        pl.BlockSpec((C_WIDTH, 2 * C_WIDTH), const),
    ]
    bpos = lambda i: (i // ns, i % ns, 0)
    a_shapes, a_specs = [], []
    for _ in range(3):
        a_shapes.append(jax.ShapeDtypeStruct((t, A_WIDTH), BF16))
        a_specs.append(pl.BlockSpec((tm, A_WIDTH), row))
        for dil in DILATIONS[1:]:
            a_shapes.append(jax.ShapeDtypeStruct((batch, seq // dil, dil * A_WIDTH), BF16))
            a_specs.append(pl.BlockSpec((1, tm // dil, dil * A_WIDTH), bpos))
    out_shape = (
        *a_shapes,
        jax.ShapeDtypeStruct((batch, B_HEADS * HEAD_PAD, seq), BF16),
        jax.ShapeDtypeStruct((batch, B_HEADS, seq, HEAD_PAD), BF16),
        jax.ShapeDtypeStruct((batch, B_WIDTH, seq), BF16),
        jax.ShapeDtypeStruct((t, 2 * C_WIDTH), BF16),
    )
    out_specs = (
        *a_specs,
        pl.BlockSpec((1, B_HEADS * HEAD_PAD, tm), lambda i: (i // ns, 0, i % ns)),
        pl.BlockSpec((1, B_HEADS, tm, HEAD_PAD), lambda i: (i // ns, 0, i % ns, 0)),
        pl.BlockSpec((1, B_WIDTH, tm), lambda i: (i // ns, 0, i % ns)),
        pl.BlockSpec((tm, 2 * C_WIDTH), row),
    )
    outs = pl.pallas_call(
        _proj_body, grid=(t // tm,), in_specs=in_specs, out_specs=out_specs, out_shape=out_shape,
        scratch_shapes=[pltpu.VMEM((A_WIDTH // LANES, tm, LANES), F32)],
        compiler_params=_cparams(("arbitrary",)), name="proj",
    )(x2, g, wm, rot, cosa, sina, cosk, sink, cosq, sinq, qn, wuq, kvn, wkn, wv, cs)
    nd = len(DILATIONS)
    qkv_views = [tuple(outs[a * nd + j] for a in range(3)) for j in range(nd)]
    return qkv_views, outs[3 * nd:]


def _win_body(q_ref, kp_ref, km_ref, kn_ref, vp_ref, vm_ref, vn_ref, o_ref, l_ref, *, seg_len, tq):
    j = pl.program_id(2)
    span = WIN_SUB + 2 * WINDOW_RADIUS
    lane = lax.broadcasted_iota(jnp.int32, (1, LANES), 1)
    lo_half = lane < HEAD_DIM
    rel = (lax.broadcasted_iota(jnp.int32, (WIN_SUB, span), 1) - WINDOW_RADIUS
           - lax.broadcasted_iota(jnp.int32, (WIN_SUB, span), 0))
    band = jnp.abs(rel) <= WINDOW_RADIUS
    kcol = lax.broadcasted_iota(jnp.int32, (1, span), 1)
    n_sub = tq // WIN_SUB
    biases = []
    for sb in range(n_sub):
        kpos = j * tq + sb * WIN_SUB - WINDOW_RADIUS + kcol
        biases.append(jnp.where(band & (kpos >= 0) & (kpos < seg_len), 0.0, NEG_INF))
    for pair in range(A_HEADS // 2):
        cs = slice(pair * LANES, (pair + 1) * LANES)
        q2 = q_ref[0, :, cs] * (HEAD_DIM ** -0.5 * math.log2(math.e))
        qh = (jnp.where(lo_half, q2, 0).astype(BF16), jnp.where(lo_half, 0, q2).astype(BF16))
        kcat = jnp.concatenate([kp_ref[0, :, cs], km_ref[0, :, cs], kn_ref[0, :, cs]], axis=0)
        vcat = jnp.concatenate([vp_ref[0, :, cs], vm_ref[0, :, cs], vn_ref[0, :, cs]], axis=0)
        for sb in range(n_sub):
            r0 = sb * WIN_SUB
            ks = kcat[r0:r0 + span]
            vs = vcat[r0:r0 + span]
            outs, lses = [], []
            for hh in range(2):
                s = _dot_nt(qh[hh][r0:r0 + WIN_SUB], ks) + biases[sb]
                m = jnp.max(s, axis=-1, keepdims=True)
                p = jnp.exp2(s - m)
                den = jnp.sum(p, axis=-1, keepdims=True)
                outs.append(_dot(p.astype(BF16), vs) / den)
                lses.append(m * math.log(2.0) + jnp.log(den))
            o_ref[0, r0:r0 + WIN_SUB, cs] = jnp.where(lo_half, outs[0], outs[1]).astype(BF16)
            l_ref[0, r0:r0 + WIN_SUB, cs] = jnp.where(lo_half, lses[0], lses[1])


def _win_call(q, k, v, batch, seq, dil):
    seg = seq // dil
    tq = min(TQ_WIN, seg)
    nb = tq // WINDOW_RADIUS
    last = seg // WINDOW_RADIUS - 1
    main = pl.BlockSpec((1, tq, A_WIDTH), lambda b, r, j: (b, j, r))
    prev = pl.BlockSpec((1, WINDOW_RADIUS, A_WIDTH), lambda b, r, j: (b, jnp.maximum(j * nb - 1, 0), r))
    nxt = pl.BlockSpec((1, WINDOW_RADIUS, A_WIDTH), lambda b, r, j: (b, jnp.minimum((j + 1) * nb, last), r))
    return pl.pallas_call(
        functools.partial(_win_body, seg_len=seg, tq=tq),
        grid=(batch, dil, seg // tq),
        in_specs=[main, prev, main, nxt, prev, main, nxt],
        out_specs=(main, main),
        out_shape=(jax.ShapeDtypeStruct((batch, seg, dil * A_WIDTH), BF16),
                   jax.ShapeDtypeStruct((batch, seg, dil * A_WIDTH), F32)),
        compiler_params=_cparams(("arbitrary", "arbitrary", "arbitrary")), name=f"win_d{dil}",
    )(q, k, k, k, v, v, v)


def _mla_body(qt_ref, k_ref, vt_ref, o_ref, sa, sb, pa, pb, acc_ref, *, seq, tk):
    tq = qt_ref.shape[2]
    nk = seq // tk
    qts = [qt_ref[0, h * HEAD_PAD:(h + 1) * HEAD_PAD, :] for h in range(B_HEADS)]
    ones = jnp.ones((BF16_SUBLANES, tk), BF16)

    def scores(h, kt, dst):
        start = pl.multiple_of(kt * tk, tk)
        dst[h] = _dot(k_ref[0, h, pl.ds(start, tk), :], qts[h])

    def values(h, kt, src, alpha):
        start = pl.multiple_of(kt * tk, tk)
        vt = jnp.concatenate([vt_ref[0, h * B_V:(h + 1) * B_V, pl.ds(start, tk)], ones], axis=0)
        acc_ref[h] = alpha * acc_ref[h] + _dot(vt, src[h])

    def softmax(h, src, dst, m):
        st = src[h]
        m_new = jnp.maximum(m, jnp.max(st, axis=0, keepdims=True))
        dst[h] = jnp.exp2(st - m_new).astype(BF16)
        return m_new, jnp.exp2(m - m_new)

    for h in range(B_HEADS):
        scores(h, 0, sa)
        acc_ref[h] = jnp.zeros((B_V + BF16_SUBLANES, tq), F32)
    pb[...] = jnp.zeros_like(pb)

    def step(i, carry):
        out = []
        for h in range(B_HEADS):
            m, alpha_prev = carry[h]
            t0 = 2 * i
            scores(h, t0 + 1, sb)
            values(h, jnp.maximum(t0 - 1, 0), pb, alpha_prev)
            m, alpha0 = softmax(h, sa, pa, m)
            scores(h, jnp.minimum(t0 + 2, nk - 1), sa)
            values(h, t0, pa, alpha0)
            m, alpha1 = softmax(h, sb, pb, m)
            out.append((m, alpha1))
        return tuple(out)

    init = tuple((jnp.full((1, tq), NEG_INF, F32), jnp.ones((1, tq), F32)) for _ in range(B_HEADS))
    res = lax.fori_loop(0, nk // 2, step, init)
    outs = []
    for h in range(B_HEADS):
        values(h, nk - 1, pb, res[h][1])
        acc = acc_ref[h]
        outs.append(acc[:B_V] / acc[B_V:B_V + 1])
    o_ref[...] = jnp.concatenate(outs, axis=0).T.astype(BF16)


def _mla_call(qbt, kb, vbt, batch, seq):
    tq = min(TQ_MLA, seq)
    tk = min(TK_MLA, seq)
    nq = seq // tq
    assert (seq // tk) % 2 == 0
    return pl.pallas_call(
        functools.partial(_mla_body, seq=seq, tk=tk),
        grid=(batch, nq),
        in_specs=[
            pl.BlockSpec((1, B_HEADS * HEAD_PAD, tq), lambda b, qi: (b, 0, qi)),
            pl.BlockSpec((1, B_HEADS, seq, HEAD_PAD), lambda b, qi: (b, 0, 0, 0)),
            pl.BlockSpec((1, B_WIDTH, seq), lambda b, qi: (b, 0, 0)),
        ],
        out_specs=pl.BlockSpec((tq, B_WIDTH), lambda b, qi: (b * nq + qi, 0)),
        out_shape=jax.ShapeDtypeStruct((batch * seq, B_WIDTH), BF16),
        scratch_shapes=[pltpu.VMEM((B_HEADS, tk, tq), F32), pltpu.VMEM((B_HEADS, tk, tq), F32),
                        pltpu.VMEM((B_HEADS, tk, tq), BF16), pltpu.VMEM((B_HEADS, tk, tq), BF16),
                        pltpu.VMEM((B_HEADS, B_V + BF16_SUBLANES, tq), F32)],
        compiler_params=_cparams(("arbitrary", "arbitrary")), name="mla_attn",
    )(qbt, kb, vbt)


def _dft_body(c_ref, ns_ref, x_ref, xr_ref, xh_ref, o_ref, acc_ref, *, seq):
    k = pl.program_id(1)
    tm = c_ref.shape[0]

    @pl.when(k == 0)
    def _():
        row = lax.broadcasted_iota(jnp.int32, (tm, 1), 0)
        sign = (1 - 2 * (row & 1)).astype(F32) * (seq ** -0.5)
        for b in range(x_ref.shape[0]):
            acc_ref[b] = sign * xh_ref[b, 0:1, 0:C_WIDTH].astype(F32)

    c = c_ref[...]
    ns = ns_ref[...]
    for b in range(x_ref.shape[0]):
        x = x_ref[b].astype(F32)
        xr = xr_ref[b].astype(F32)
        sym = (x[:, 0:C_WIDTH] + xr[:, 0:C_WIDTH]).astype(BF16)
        anti = (x[:, C_WIDTH:2 * C_WIDTH] - xr[:, C_WIDTH:2 * C_WIDTH]).astype(BF16)
        acc_ref[b] += _dot(c, sym) + _dot(ns, anti)

    @pl.when(k == pl.num_programs(1) - 1)
    def _():
        o_ref[...] = acc_ref[...].astype(o_ref.dtype)


def _dft_call(cmat, nsmat, xcs, batch, seq):
    half = seq // 2
    tm = min(TM_DFT, seq)
    tk = min(TK_DFT, half)
    x3 = xcs.reshape(batch, seq, 2 * C_WIDTH)
    xr = jnp.concatenate([jnp.zeros((batch, 1, 2 * C_WIDTH), x3.dtype), x3[:, :half:-1]], axis=1)
    xh = x3[:, half:half + 8]
    y = pl.pallas_call(
        functools.partial(_dft_body, seq=seq), grid=(seq // tm, half // tk),
        in_specs=[pl.BlockSpec((tm, tk), lambda i, k: (i, k)),
                  pl.BlockSpec((tm, tk), lambda i, k: (i, k)),
                  pl.BlockSpec((batch, tk, 2 * C_WIDTH), lambda i, k: (0, k, 0)),
                  pl.BlockSpec((batch, tk, 2 * C_WIDTH), lambda i, k: (0, k, 0)),
                  pl.BlockSpec((batch, 8, 2 * C_WIDTH), lambda i, k: (0, 0, 0))],
        out_specs=pl.BlockSpec((batch, tm, C_WIDTH), lambda i, k: (0, i, 0)),
        out_shape=jax.ShapeDtypeStruct((batch, seq, C_WIDTH), BF16),
        scratch_shapes=[pltpu.VMEM((batch, tm, C_WIDTH), F32)],
        compiler_params=_cparams(("arbitrary", "arbitrary")), name="seq_dft",
    )(cmat, nsmat, x3, xr, xh)
    return y.reshape(batch * seq, C_WIDTH)


def _merge_residues(scr, ref, dil):
    tm = scr.shape[1]
    for r in range(dil):
        for c in range(A_WIDTH // LANES):
            col = r * A_WIDTH + c * LANES
            scr[c, pl.ds(r, tm // dil, stride=dil), :] = ref[0, :, col:col + LANES].astype(F32)
    return jnp.concatenate([scr[c] for c in range(A_WIDTH // LANES)], axis=1)


def _out_body(x_ref, o1_ref, o4_ref, o16_ref, l1_ref, l4_ref, l16_ref, ob_ref, oc_ref,
              ga_ref, gb_ref, gc_ref, wa_ref, wb_ref, wc_ref, y_ref, so4, sl4, so16, sl16):
    l1 = l1_ref[0]
    l4 = _merge_residues(sl4, l4_ref, DILATIONS[1])
    l16 = _merge_residues(sl16, l16_ref, DILATIONS[2])
    o4 = _merge_residues(so4, o4_ref, DILATIONS[1])
    o16 = _merge_residues(so16, o16_ref, DILATIONS[2])
    m = jnp.maximum(jnp.maximum(l1, l4), l16)
    e1, e4, e16 = jnp.exp(l1 - m), jnp.exp(l4 - m), jnp.exp(l16 - m)
    oa = (o1_ref[0].astype(F32) * e1 + o4 * e4 + o16 * e16) / (e1 + e4 + e16)
    na = _rms(oa, ga_ref[...]).astype(BF16)
    nb = _rms(ob_ref[...].astype(F32), gb_ref[...]).astype(BF16)
    nc = _rms(oc_ref[...].astype(F32), gc_ref[...]).astype(BF16)
    y_ref[...] = (x_ref[...] + _dot(na, wa_ref[...]) + _dot(nb, wb_ref[...]) + _dot(nc, wc_ref[...]))


def _out_call(x2, oa, la, ob, oc, ga, gb, gc, wa, wb, wc, batch, seq):
    t, d = x2.shape
    tm = min(TM_PROJ, seq)
    ns = seq // tm
    row = lambda i: (i, 0)
    const = lambda i: (0, 0)
    wide = lambda w: pl.BlockSpec((tm, w), row)
    view = lambda dil: pl.BlockSpec((1, tm // dil, dil * A_WIDTH), lambda i: (i // ns, i % ns, 0))
    views = [view(dil) for dil in DILATIONS]
    slab = pltpu.VMEM((A_WIDTH // LANES, tm, LANES), F32)
    return pl.pallas_call(
        _out_body, grid=(t // tm,),
        in_specs=[wide(d)] + views + views + [wide(B_WIDTH), wide(C_WIDTH),
                  pl.BlockSpec((1, A_WIDTH), const), pl.BlockSpec((1, B_WIDTH), const),
                  pl.BlockSpec((1, C_WIDTH), const),
                  pl.BlockSpec((A_WIDTH, d), const), pl.BlockSpec((B_WIDTH, d), const),
                  pl.BlockSpec((C_WIDTH, d), const)],
        out_specs=wide(d),
        out_shape=jax.ShapeDtypeStruct((t, d), F32),
        scratch_shapes=[slab, slab, slab, slab],
        compiler_params=_cparams(("arbitrary",)), name="out_proj",
    )(x2, oa[0], oa[1], oa[2], la[0], la[1], la[2], ob, oc, ga, gb, gc, wa, wb, wc)


def _ffn_body(x_ref, g_ref, wg_ref, wu_ref, wd_ref, y_ref, h_ref, acc_ref):
    f = pl.program_id(1)

    @pl.when(f == 0)
    def _():
        h_ref[...] = _rms(x_ref[...], g_ref[...]).astype(BF16)
        acc_ref[...] = jnp.zeros_like(acc_ref)

    h = h_ref[...]
    gate = _dot(h, wg_ref[...])
    up = _dot(h, wu_ref[...])
    acc_ref[...] += _dot((jax.nn.silu(gate) * up).astype(BF16), wd_ref[...])

    @pl.when(f == pl.num_programs(1) - 1)
    def _():
        y_ref[...] = x_ref[...] + acc_ref[...]


def _ffn_tile(d_ff):
    for steps in (1, 2, 4, 11, 22):
        if d_ff % steps == 0 and (d_ff // steps) % LANES == 0 and d_ff // steps <= 1536:
            return d_ff // steps
    return d_ff


def _ffn_call(x2, g, wg, wu, wd):
    t, d = x2.shape
    d_ff = wg.shape[1]
    tm = min(TM_FFN, t)
    tf = _ffn_tile(d_ff)
    return pl.pallas_call(
        _ffn_body, grid=(t // tm, d_ff // tf),
        in_specs=[pl.BlockSpec((tm, d), lambda i, f: (i, 0)),
                  pl.BlockSpec((1, d), lambda i, f: (0, 0)),
                  pl.BlockSpec((d, tf), lambda i, f: (0, f)),
                  pl.BlockSpec((d, tf), lambda i, f: (0, f)),
                  pl.BlockSpec((tf, d), lambda i, f: (f, 0))],
        out_specs=pl.BlockSpec((tm, d), lambda i, f: (i, 0)),
        out_shape=jax.ShapeDtypeStruct((t, d), F32),
        scratch_shapes=[pltpu.VMEM((tm, d), BF16), pltpu.VMEM((tm, d), F32)],
        compiler_params=_cparams(("arbitrary", "arbitrary")), name="dense_ffn",
    )(x2, g, wg, wu, wd)


def _router_body(x_ref, g_ref, wr_ref, idx_ref, gate_ref):
    h = _rms(x_ref[...], g_ref[...])
    logits = lax.dot_general(wr_ref[...], h, (((1,), (1,)), ((), ())),
                             precision=lax.Precision.HIGHEST, preferred_element_type=F32)
    eid = lax.broadcasted_iota(jnp.int32, logits.shape, 0)
    m1 = jnp.max(logits, axis=0, keepdims=True)
    i1 = jnp.min(jnp.where(logits == m1, eid, N_EXPERTS), axis=0, keepdims=True)
    rest = jnp.where(eid == i1, -jnp.inf, logits)
    m2 = jnp.max(rest, axis=0, keepdims=True)
    i2 = jnp.min(jnp.where(rest == m2, eid, N_EXPERTS), axis=0, keepdims=True)
    e2 = jnp.exp(m2 - m1)
    idx_ref[...] = jnp.concatenate([i1, i2], axis=0)
    gate_ref[...] = jnp.concatenate([1.0 / (1.0 + e2), e2 / (1.0 + e2)], axis=0)


def _router_call(x2, g, wr_t):
    t, d = x2.shape
    tm = min(TM_ROUTER, t)
    return pl.pallas_call(
        _router_body, grid=(t // tm,),
        in_specs=[pl.BlockSpec((tm, d), lambda i: (i, 0)),
                  pl.BlockSpec((1, d), lambda i: (0, 0)),
                  pl.BlockSpec((N_EXPERTS, d), lambda i: (0, 0))],
        out_specs=(pl.BlockSpec((TOP_K, tm), lambda i: (0, i)),
                   pl.BlockSpec((TOP_K, tm), lambda i: (0, i))),
        out_shape=(jax.ShapeDtypeStruct((TOP_K, t), jnp.int32),
                   jax.ShapeDtypeStruct((TOP_K, t), F32)),
        compiler_params=_cparams(("arbitrary",)), name="router",
    )(x2, g, wr_t)


def _expert_body(bexp_ref, tok_cur, tok_next, dst_prev, dst_cur, x_hbm, g_ref, wg_ref, wu_ref, wd_ref,
                 y_hbm, xb0, xb1, hbuf, ac0, ac1, gsem, ssem):
    i = pl.program_id(0)
    f = pl.program_id(1)
    nb = pl.num_programs(0)
    nf = pl.num_programs(1)
    rows = xb0.shape[0]
    chunk = rows // EXP_STEPS

    def variant(xcur, xnxt, acur, aprv, s, compute):
        def row_in(r, src_row, buf, sem):
            return pltpu.make_async_copy(x_hbm.at[pl.ds(src_row, 1)], buf.at[pl.ds(r, 1)], sem)

        def row_out(r, dst_row, buf, sem):
            return pltpu.make_async_copy(buf.at[pl.ds(r, 1)], y_hbm.at[pl.ds(dst_row, 1)], sem)

        def all_rows(fn):
            def body(r, c):
                fn(r)
                return c
            lax.fori_loop(0, rows, body, 0, unroll=8)

        @pl.when(f == 0)
        def _():
            @pl.when(i == 0)
            def _():
                all_rows(lambda r: row_in(r, tok_cur[0, 0, r], xcur, gsem.at[s]).start())
                aprv[...] = jnp.zeros_like(aprv)

            all_rows(lambda r: row_in(r, 0, xcur, gsem.at[s]).wait())
            if compute:
                hbuf[...] = _rms(xcur[...], g_ref[...]).astype(BF16)
            acur[...] = jnp.zeros_like(acur)

        base = pl.multiple_of(f * chunk, chunk)
        for k in range(chunk):
            r = base + k
            row_in(r, tok_next[0, 0, r], xnxt, gsem.at[1 - s]).start()
            row_out(r, dst_prev[0, 0, r], aprv, ssem.at[1 - s]).start()
        if compute:
            h = hbuf[...]
            gate = _dot(h, wg_ref[0])
            up = _dot(h, wu_ref[0])
            acur[...] += _dot((jax.nn.silu(gate) * up).astype(BF16), wd_ref[0])

        @pl.when(f == nf - 1)
        def _():
            all_rows(lambda r: row_out(r, 0, aprv, ssem.at[1 - s]).wait())

            @pl.when(i == nb - 1)
            def _():
                all_rows(lambda r: row_out(r, dst_cur[0, 0, r], acur, ssem.at[s]).start())
                all_rows(lambda r: row_out(r, 0, acur, ssem.at[s]).wait())
                all_rows(lambda r: row_in(r, 0, xnxt, gsem.at[1 - s]).wait())

    even = i % 2 == 0
    padding_only = bexp_ref[i] >= N_EXPERTS
    for is_even, bufs in ((True, (xb0, xb1, ac0, ac1, 0)), (False, (xb1, xb0, ac1, ac0, 1))):
        for compute in (True, False):
            @pl.when((even == is_even) & (padding_only != compute))
            def _(bufs=bufs, compute=compute):
                variant(*bufs, compute)


def _expert_call(x2, g, wg, wu, wd, bexp, row_tok, row_dst, n_out_rows):
    t, d = x2.shape
    n_blocks = bexp.shape[0]
    d_ff = wg.shape[2]
    assert d_ff % EXP_STEPS == 0 and (d_ff // EXP_STEPS) % LANES == 0
    tf = d_ff // EXP_STEPS
    last = n_blocks - 1
    rows_of = lambda which: pl.BlockSpec((1, 1, TM_EXP), which, memory_space=pltpu.SMEM)
    grid_spec = pltpu.PrefetchScalarGridSpec(
        num_scalar_prefetch=1,
        grid=(n_blocks, EXP_STEPS),
        in_specs=[
            rows_of(lambda i, f, be: (i, 0, 0)),
            rows_of(lambda i, f, be: (jnp.minimum(i + 1, last), 0, 0)),
            rows_of(lambda i, f, be: (jnp.maximum(i - 1, 0), 0, 0)),
            rows_of(lambda i, f, be: (i, 0, 0)),
            pl.BlockSpec(memory_space=pl.ANY),
            pl.BlockSpec((1, d), lambda i, f, be: (0, 0)),
            pl.BlockSpec((1, d, tf), lambda i, f, be: (be[i] % N_EXPERTS, 0, f)),
            pl.BlockSpec((1, d, tf), lambda i, f, be: (be[i] % N_EXPERTS, 0, f)),
            pl.BlockSpec((1, tf, d), lambda i, f, be: (be[i] % N_EXPERTS, f, 0)),
        ],
        out_specs=pl.BlockSpec(memory_space=pl.ANY),
        scratch_shapes=[pltpu.VMEM((TM_EXP, d), F32), pltpu.VMEM((TM_EXP, d), F32),
                        pltpu.VMEM((TM_EXP, d), BF16),
                        pltpu.VMEM((TM_EXP, d), F32), pltpu.VMEM((TM_EXP, d), F32),
                        pltpu.SemaphoreType.DMA((2,)), pltpu.SemaphoreType.DMA((2,))],
    )
    return pl.pallas_call(
        _expert_body, grid_spec=grid_spec,
        out_shape=jax.ShapeDtypeStruct((n_out_rows, d), F32),
        compiler_params=_cparams(("arbitrary", "arbitrary")), name="experts",
    )(bexp, row_tok, row_tok, row_dst, row_dst, x2, g, wg, wu, wd)


def _combine_body(x_ref, y0_ref, y1_ref, gate_ref, gf_ref, o_ref, *, final):
    g0 = gate_ref[:, 0:1]
    g1 = gate_ref[:, 1:2]
    y = x_ref[...] + (y0_ref[...] * g0 + y1_ref[...] * g1)
    o_ref[...] = _rms(y, gf_ref[...]) if final else y


def _combine_call(x2, y2, gates_t, gfinal, final):
    t, d = x2.shape
    tm = min(TM_ROUTER, t)
    nt = t // tm
    return pl.pallas_call(
        functools.partial(_combine_body, final=final), grid=(nt,),
        in_specs=[pl.BlockSpec((tm, d), lambda i: (i, 0)),
                  pl.BlockSpec((tm, d), lambda i: (i, 0)),
                  pl.BlockSpec((tm, d), lambda i: (i + nt, 0)),
                  pl.BlockSpec((tm, TOP_K), lambda i: (i, 0)),
                  pl.BlockSpec((1, d), lambda i: (0, 0))],
        out_specs=pl.BlockSpec((tm, d), lambda i: (i, 0)),
        out_shape=jax.ShapeDtypeStruct((t, d), F32),
        compiler_params=_cparams(("arbitrary",)), name="moe_combine",
    )(x2, y2, y2, gates_t, gfinal)


def _final_norm_body(x_ref, g_ref, o_ref):
    o_ref[...] = _rms(x_ref[...], g_ref[...])


def _final_norm_call(x2, g):
    t, d = x2.shape
    tm = min(TM_ROUTER, t)
    return pl.pallas_call(
        _final_norm_body, grid=(t // tm,),
        in_specs=[pl.BlockSpec((tm, d), lambda i: (i, 0)), pl.BlockSpec((1, d), lambda i: (0, 0))],
        out_specs=pl.BlockSpec((tm, d), lambda i: (i, 0)),
        out_shape=jax.ShapeDtypeStruct((t, d), F32),
        compiler_params=_cparams(("arbitrary",)), name="final_norm",
    )(x2, g)


def _route_plan(idx, t):
    n_assign = TOP_K * t
    flat_e = idx.reshape(-1)
    order = jnp.argsort(flat_e).astype(jnp.int32)
    counts = jnp.sum(flat_e[:, None] == jnp.arange(N_EXPERTS, dtype=jnp.int32)[None, :], axis=0,
                     dtype=jnp.int32)
    padded = ((counts + TM_EXP - 1) // TM_EXP) * TM_EXP
    start = jnp.cumsum(counts) - counts
    cum_padded = jnp.cumsum(padded)
    start_padded = cum_padded - padded
    n_blocks = -(-n_assign // TM_EXP) + N_EXPERTS
    block_start = jnp.arange(n_blocks, dtype=jnp.int32) * TM_EXP
    bexp = jnp.minimum(jnp.sum(block_start[:, None] >= cum_padded[None, :], axis=1),
                       N_EXPERTS - 1).astype(jnp.int32)
    off = block_start[:, None] + jnp.arange(TM_EXP, dtype=jnp.int32)[None, :] - start_padded[bexp][:, None]
    valid = off < counts[bexp][:, None]
    a = order[jnp.clip(start[bexp][:, None] + off, 0, n_assign - 1)]
    bexp = bexp + N_EXPERTS * (~jnp.any(valid, axis=1)).astype(jnp.int32)
    pad_rank = (jnp.cumsum((~valid).reshape(-1).astype(jnp.int32)) - 1).reshape(valid.shape)
    row_tok = jnp.where(valid, a % t, 0).astype(jnp.int32)
    row_dst = jnp.where(valid, a, n_assign + pad_rank).astype(jnp.int32)
    n_out_rows = n_blocks * TM_EXP
    return bexp, row_tok.reshape(n_blocks, 1, TM_EXP), row_dst.reshape(n_blocks, 1, TM_EXP), n_out_rows


def _rope_tables(seq):
    pos = jnp.arange(seq, dtype=F32)[:, None]
    half_a = A_ROT_DIM // 2
    ang_a = pos * (ROPE_THETA ** (-jnp.arange(half_a, dtype=F32) / half_a))[None, :]
    one = jnp.ones((seq, HEAD_DIM - A_ROT_DIM), F32)
    cos_head = jnp.concatenate([jnp.cos(ang_a), jnp.cos(ang_a), one], axis=1)
    sin_head = jnp.concatenate([jnp.sin(ang_a), jnp.sin(ang_a), 0 * one], axis=1)
    cosa = jnp.tile(cos_head, (1, A_HEADS))
    sina = jnp.tile(sin_head, (1, A_HEADS))
    half_b = B_ROPE // 2
    ang_b = pos * (ROPE_THETA ** (-jnp.arange(half_b, dtype=F32) / half_b))[None, :]
    zl = jnp.zeros((seq, B_NOPE), F32)
    zr = jnp.zeros((seq, HEAD_PAD - B_NOPE - B_ROPE), F32)
    cosk = jnp.concatenate([zl, jnp.cos(ang_b), jnp.cos(ang_b), zr], axis=1)
    sink = jnp.concatenate([zl, jnp.sin(ang_b), jnp.sin(ang_b), zr], axis=1)
    return cosa, sina, cosk, sink, jnp.cos(ang_b).T, jnp.sin(ang_b).T


def _rotate_half_matrix():
    r = np.zeros((A_WIDTH, A_WIDTH), np.float32)
    half = A_ROT_DIM // 2
    for hd in range(A_HEADS):
        for i in range(half):
            r[hd * HEAD_DIM + half + i, hd * HEAD_DIM + i] = -1.0
            r[hd * HEAD_DIM + i, hd * HEAD_DIM + half + i] = 1.0
    return jnp.asarray(r, BF16)


def _dft_constants(seq):
    lo_n = 64
    hi_n = seq // lo_n
    k = jnp.arange(seq, dtype=jnp.int32)[:, None]
    ang_hi = ((k * jnp.arange(hi_n // 2, dtype=jnp.int32)[None, :]) % hi_n).astype(F32) * (2.0 * np.pi / hi_n)
    ang_lo = ((k * jnp.arange(lo_n, dtype=jnp.int32)[None, :]) % seq).astype(F32) * (2.0 * np.pi / seq)
    ch, sh = jnp.cos(ang_hi)[:, :, None], jnp.sin(ang_hi)[:, :, None]
    cl, sl = jnp.cos(ang_lo)[:, None, :], jnp.sin(ang_lo)[:, None, :]
    scale = seq ** -0.5
    cmat = ((ch * cl - sh * sl) * scale).astype(BF16).reshape(seq, seq // 2)
    nsmat = ((sh * cl + ch * sl) * -scale).astype(BF16).reshape(seq, seq // 2)
    c = np.arange(C_GROUP_DIM)
    angc = 2.0 * np.pi * ((c[:, None] * c[None, :]) % C_GROUP_DIM) / C_GROUP_DIM
    eye = np.eye(C_GROUPS)
    cs = np.concatenate([np.kron(eye, np.cos(angc)), np.kron(eye, np.sin(angc))], axis=1)
    return cmat, nsmat, jnp.asarray(cs * C_GROUP_DIM ** -0.5, BF16)


def _layer_weights(w_in, w_uq, w_ukv, w_out):
    d = w_in.shape[0]
    c_kpe = 3 * A_WIDTH + Q_LORA + KV_LORA
    w_kpe = w_in[:, c_kpe:c_kpe + B_ROPE]
    half = B_ROPE // 2
    w_kpe_rot = jnp.concatenate([-w_kpe[:, half:], w_kpe[:, :half]], axis=1)
    zl = jnp.zeros((d, B_NOPE), w_in.dtype)
    zr = jnp.zeros((d, HEAD_PAD - B_NOPE - B_ROPE), w_in.dtype)
    wm = jnp.concatenate([w_in[:, :c_kpe], zl, w_kpe, zr, zl, w_kpe_rot, zr,
                          w_in[:, c_kpe + B_ROPE:]], axis=1).astype(BF16)
    uq = w_uq.reshape(Q_LORA, B_HEADS, B_NOPE + B_ROPE)
    uq = jnp.pad(uq, ((0, 0), (0, 0), (0, HEAD_PAD - B_NOPE - B_ROPE)))
    wuq_t = uq.reshape(Q_LORA, B_HEADS * HEAD_PAD).T.astype(BF16)
    ukv = w_ukv.reshape(KV_LORA, B_HEADS, B_NOPE + B_V)
    wkn = jnp.pad(ukv[:, :, :B_NOPE], ((0, 0), (0, 0), (0, HEAD_PAD - B_NOPE)))
    wkn = wkn.reshape(KV_LORA, B_HEADS * HEAD_PAD).astype(BF16)
    wv_t = ukv[:, :, B_NOPE:].reshape(KV_LORA, B_WIDTH).T.astype(BF16)
    wo = w_out.astype(BF16)
    return wm, wuq_t, wkn, wv_t, wo[:A_WIDTH], wo[A_WIDTH:A_WIDTH + B_WIDTH], wo[A_WIDTH + B_WIDTH:]


def kernel(x, attn_norm, w_in, q_norm, w_uq, kv_norm, w_ukv, mix_gain, w_out, ffn_norm,
           w_ffn_gate, w_ffn_up, w_ffn_down, w_router, w_exp_gate, w_exp_up, w_exp_down,
           final_norm):
    batch, seq, d = x.shape
    depth = w_in.shape[0]
    t = batch * seq
    assert seq % (max(DILATIONS) * 2 * WIN_SUB) == 0
    tabs = _rope_tables(seq)
    rot = _rotate_half_matrix()
    cmat, nsmat, cs = _dft_constants(seq)
    row = lambda v: v.reshape(1, -1)

    x2 = x.reshape(t, d)
    for l in range(depth):
        wm, wuq_t, wkn, wv_t, wa, wb, wc = _layer_weights(w_in[l], w_uq[l], w_ukv[l], w_out[l])
        qkv_views, (qbt, kb, vbt, xcs) = _proj_call(
            x2, row(attn_norm[l]), wm, rot, tabs, row(q_norm[l]), wuq_t, row(kv_norm[l]), wkn, wv_t,
            cs, batch, seq)
        win = [_win_call(*(a.reshape(batch, seq // dil, dil * A_WIDTH) for a in qkv), batch, seq, dil)
               for dil, qkv in zip(DILATIONS, qkv_views)]
        ob = _mla_call(qbt, kb, vbt, batch, seq)
        oc = _dft_call(cmat, nsmat, xcs, batch, seq)
        g = mix_gain[l]
        x2 = _out_call(x2, [w[0] for w in win], [w[1] for w in win], ob, oc,
                       row(g[:A_WIDTH]), row(g[A_WIDTH:A_WIDTH + B_WIDTH]), row(g[A_WIDTH + B_WIDTH:]),
                       wa, wb, wc, batch, seq)
        i = l // 2
        if l % 2 == 0:
            x2 = _ffn_call(x2, row(ffn_norm[l]), w_ffn_gate[i].astype(BF16), w_ffn_up[i].astype(BF16),
                           w_ffn_down[i].astype(BF16))
            if l == depth - 1:
                x2 = _final_norm_call(x2, row(final_norm))
        else:
            idx, gates = _router_call(x2, row(ffn_norm[l]), w_router[i].T)
            bexp, row_tok, row_dst, n_out_rows = _route_plan(idx, t)
            y2 = _expert_call(x2, row(ffn_norm[l]), w_exp_gate[i].astype(BF16), w_exp_up[i].astype(BF16),
                              w_exp_down[i].astype(BF16), bexp, row_tok, row_dst, n_out_rows)
            x2 = _combine_call(x2, y2, gates.T, row(final_norm), final=(l == depth - 1))
    return x2.reshape(batch, seq, d)
```

```python
import functools
import math

import jax
import jax.numpy as jnp
import numpy as np
from jax import lax
from jax.experimental import pallas as pl
from jax.experimental.pallas import tpu as pltpu

F32 = jnp.float32
BF16 = jnp.bfloat16

HEAD_DIM = 64
A_HEADS = 6
A_WIDTH = A_HEADS * HEAD_DIM
A_ROT_DIM = HEAD_DIM // 4
DILATIONS = (1, 4, 16)
WINDOW_RADIUS = 64
B_HEADS = 6
B_NOPE = 64
B_ROPE = 32
B_V = 64
B_WIDTH = B_HEADS * B_V
Q_LORA = 384
KV_LORA = 128
C_GROUPS = 4
C_GROUP_DIM = 64
C_WIDTH = C_GROUPS * C_GROUP_DIM
ROPE_THETA = 500000.0
N_EXPERTS = 8
TOP_K = 2
RMS_EPS = 1e-6
NEG_INF = -1e30

LANES = 128
BF16_SUBLANES = 16
HEAD_PAD = 128
VMEM_LIMIT_BYTES = 56 * 1024 * 1024

TM_PROJ = 512
TQ_WIN = 512
WIN_SUB = 128
TQ_MLA = 512
TK_MLA = 256
TM_DFT = 1024
TK_DFT = 512
TM_FFN = 512
TM_ROUTER = 512
EXP_STEPS = 7
TM_EXP = 128 * EXP_STEPS


def _cparams(sem):
    return pltpu.CompilerParams(dimension_semantics=sem, vmem_limit_bytes=VMEM_LIMIT_BYTES)


def _rms(x, g):
    return x * lax.rsqrt(jnp.mean(x * x, axis=-1, keepdims=True) + RMS_EPS) * g


def _dot(a, b):
    return jnp.dot(a, b, preferred_element_type=F32)


def _dot_nt(a, b):
    return lax.dot_general(a, b, (((1,), (1,)), ((), ())), preferred_element_type=F32)


def _split_residues(scr, val, nat_ref, view_refs):
    tm = val.shape[0]
    nat_ref[...] = val.astype(BF16)
    for c in range(A_WIDTH // LANES):
        scr[c] = val[:, c * LANES:(c + 1) * LANES]
    for dil, ref in zip(DILATIONS[1:], view_refs):
        for r in range(dil):
            for c in range(A_WIDTH // LANES):
                col = r * A_WIDTH + c * LANES
                ref[0, :, col:col + LANES] = scr[c, pl.ds(r, tm // dil, stride=dil), :].astype(BF16)


def _proj_body(x_ref, g_ref, wm_ref, rot_ref, cosa_ref, sina_ref, cosk_ref, sink_ref,
               cosq_ref, sinq_ref, qn_ref, wuq_ref, kvn_ref, wkn_ref, wv_ref, cs_ref, flip_ref,
               qa_ref, qa4_ref, qa16_ref, ka_ref, ka4_ref, ka16_ref, va_ref, va4_ref, va16_ref,
               qbt_ref, kb_ref, vbt_ref, xcs_ref, xcs_flip_ref, scr):
    h = _rms(x_ref[...], g_ref[...]).astype(BF16)
    z = _dot(h, wm_ref[...])
    o = 0
    qa = z[:, o:o + A_WIDTH]; o += A_WIDTH
    ka = z[:, o:o + A_WIDTH]; o += A_WIDTH
    va = z[:, o:o + A_WIDTH]; o += A_WIDTH
    cq = z[:, o:o + Q_LORA]; o += Q_LORA
    ckv = z[:, o:o + KV_LORA]; o += KV_LORA
    kpe = z[:, o:o + HEAD_PAD]; o += HEAD_PAD
    kpe_rot = z[:, o:o + HEAD_PAD]; o += HEAD_PAD
    fc = z[:, o:o + C_WIDTH]

    cosa = cosa_ref[...]
    sina = sina_ref[...]
    rot = rot_ref[...]
    _split_residues(scr, qa * cosa + _dot(qa.astype(BF16), rot) * sina, qa_ref, (qa4_ref, qa16_ref))
    _split_residues(scr, ka * cosa + _dot(ka.astype(BF16), rot) * sina, ka_ref, (ka4_ref, ka16_ref))
    _split_residues(scr, va, va_ref, (va4_ref, va16_ref))

    cqn = _rms(cq, qn_ref[...]).astype(BF16)
    qbt = _dot_nt(wuq_ref[...], cqn) * ((B_NOPE + B_ROPE) ** -0.5 * math.log2(math.e))
    cosq = cosq_ref[...]
    sinq = sinq_ref[...]
    half = B_ROPE // 2
    for hd in range(B_HEADS):
        r0 = hd * HEAD_PAD
        x1 = qbt[r0 + B_NOPE:r0 + B_NOPE + half]
        x2 = qbt[r0 + B_NOPE + half:r0 + B_NOPE + B_ROPE]
        qbt_ref[0, r0:r0 + B_NOPE, :] = qbt[r0:r0 + B_NOPE].astype(BF16)
        qbt_ref[0, r0 + B_NOPE:r0 + B_NOPE + half, :] = (x1 * cosq - x2 * sinq).astype(BF16)
        qbt_ref[0, r0 + B_NOPE + half:r0 + B_NOPE + B_ROPE, :] = (x2 * cosq + x1 * sinq).astype(BF16)
        qbt_ref[0, r0 + B_NOPE + B_ROPE:r0 + HEAD_PAD, :] = jnp.zeros(
            (HEAD_PAD - B_NOPE - B_ROPE, qbt.shape[1]), BF16)

    ckvn = _rms(ckv, kvn_ref[...]).astype(BF16)
    kn = _dot(ckvn, wkn_ref[...])
    kpe_r = kpe * cosk_ref[...] + kpe_rot * sink_ref[...]
    for hd in range(B_HEADS):
        kb_ref[0, hd] = (kn[:, hd * HEAD_PAD:(hd + 1) * HEAD_PAD] + kpe_r).astype(BF16)
    vbt_ref[0] = _dot_nt(wv_ref[...], ckvn).astype(BF16)

    xcs = _dot(fc.astype(BF16), cs_ref[...]).astype(BF16)
    xcs_ref[0] = xcs
    xcs_flip_ref[0] = _dot(flip_ref[...], xcs).astype(BF16)


def _proj_call(x2, g, wm, rot, tabs, qn, wuq, kvn, wkn, wv, cs, batch, seq):
    t, d = x2.shape
    tm = min(TM_PROJ, seq)
    ns = seq // tm
    nm = wm.shape[1]
    cosa, sina, cosk, sink, cosq, sinq = tabs
    row = lambda i: (i, 0)
    pos = lambda i: (i % ns, 0)
    post = lambda i: (0, i % ns)
    const = lambda i: (0, 0)
    in_specs = [
        pl.BlockSpec((tm, d), row),
        pl.BlockSpec((1, d), const),
        pl.BlockSpec((d, nm), const),
        pl.BlockSpec((A_WIDTH, A_WIDTH), const),
        pl.BlockSpec((tm, A_WIDTH), pos),
        pl.BlockSpec((tm, A_WIDTH), pos),
        pl.BlockSpec((tm, HEAD_PAD), pos),
        pl.BlockSpec((tm, HEAD_PAD), pos),
        pl.BlockSpec((B_ROPE // 2, tm), post),
        pl.BlockSpec((B_ROPE // 2, tm), post),
        pl.BlockSpec((1, Q_LORA), const),
        pl.BlockSpec((B_HEADS * HEAD_PAD, Q_LORA), const),
        pl.BlockSpec((1, KV_LORA), const),
        pl.BlockSpec((KV_LORA, B_HEADS * HEAD_PAD), const),
        pl.BlockSpec((B_WIDTH, KV_LORA), const),
        pl.BlockSpec((C_WIDTH, 2 * C_WIDTH), const),
        pl.BlockSpec((tm, tm), const),
    ]
    bpos = lambda i: (i // ns, i % ns, 0)
    a_shapes, a_specs = [], []
    for _ in range(3):
        a_shapes.append(jax.ShapeDtypeStruct((t, A_WIDTH), BF16))
        a_specs.append(pl.BlockSpec((tm, A_WIDTH), row))
        for dil in DILATIONS[1:]:
            a_shapes.append(jax.ShapeDtypeStruct((batch, seq // dil, dil * A_WIDTH), BF16))
            a_specs.append(pl.BlockSpec((1, tm // dil, dil * A_WIDTH), bpos))
    out_shape = (
        *a_shapes,
        jax.ShapeDtypeStruct((batch, B_HEADS * HEAD_PAD, seq), BF16),
        jax.ShapeDtypeStruct((batch, B_HEADS, seq, HEAD_PAD), BF16),
        jax.ShapeDtypeStruct((batch, B_WIDTH, seq), BF16),
        jax.ShapeDtypeStruct((batch, seq, 2 * C_WIDTH), BF16),
        jax.ShapeDtypeStruct((batch, seq, 2 * C_WIDTH), BF16),
    )
    out_specs = (
        *a_specs,
        pl.BlockSpec((1, B_HEADS * HEAD_PAD, tm), lambda i: (i // ns, 0, i % ns)),
        pl.BlockSpec((1, B_HEADS, tm, HEAD_PAD), lambda i: (i // ns, 0, i % ns, 0)),
        pl.BlockSpec((1, B_WIDTH, tm), lambda i: (i // ns, 0, i % ns)),
        pl.BlockSpec((1, tm, 2 * C_WIDTH), bpos),
        pl.BlockSpec((1, tm, 2 * C_WIDTH), lambda i: (i // ns, ns - 1 - i % ns, 0)),
    )
    flip = jnp.asarray(np.eye(tm, dtype=np.float32)[::-1], BF16)
    outs = pl.pallas_call(
        _proj_body, grid=(t // tm,), in_specs=in_specs, out_specs=out_specs, out_shape=out_shape,
        scratch_shapes=[pltpu.VMEM((A_WIDTH // LANES, tm, LANES), F32)],
        compiler_params=_cparams(("arbitrary",)), name="proj",
    )(x2, g, wm, rot, cosa, sina, cosk, sink, cosq, sinq, qn, wuq, kvn, wkn, wv, cs, flip)
    nd = len(DILATIONS)
    qkv_views = [tuple(outs[a * nd + j] for a in range(3)) for j in range(nd)]
    return qkv_views, outs[3 * nd:]


def _win_body(q_ref, kp_ref, km_ref, kn_ref, vp_ref, vm_ref, vn_ref, o_ref, l_ref, *, seg_len, tq):
    j = pl.program_id(2)
    span = WIN_SUB + 2 * WINDOW_RADIUS
    lane = lax.broadcasted_iota(jnp.int32, (1, LANES), 1)
    lo_half = lane < HEAD_DIM
    rel = (lax.broadcasted_iota(jnp.int32, (WIN_SUB, span), 1) - WINDOW_RADIUS
           - lax.broadcasted_iota(jnp.int32, (WIN_SUB, span), 0))
    band = jnp.abs(rel) <= WINDOW_RADIUS
    kcol = lax.broadcasted_iota(jnp.int32, (1, span), 1)
    n_sub = tq // WIN_SUB
    biases = []
    for sb in range(n_sub):
        kpos = j * tq + sb * WIN_SUB - WINDOW_RADIUS + kcol
        biases.append(jnp.where(band & (kpos >= 0) & (kpos < seg_len), 0.0, NEG_INF))
    for pair in range(A_HEADS // 2):
        cs = slice(pair * LANES, (pair + 1) * LANES)
        q2 = q_ref[0, :, cs] * (HEAD_DIM ** -0.5 * math.log2(math.e))
        qh = (jnp.where(lo_half, q2, 0).astype(BF16), jnp.where(lo_half, 0, q2).astype(BF16))
        kcat = jnp.concatenate([kp_ref[0, :, cs], km_ref[0, :, cs], kn_ref[0, :, cs]], axis=0)
        vcat = jnp.concatenate([vp_ref[0, :, cs], vm_ref[0, :, cs], vn_ref[0, :, cs]], axis=0)
        for sb in range(n_sub):
            r0 = sb * WIN_SUB
            ks = kcat[r0:r0 + span]
            vs = vcat[r0:r0 + span]
            outs, lses = [], []
            for hh in range(2):
                s = _dot_nt(qh[hh][r0:r0 + WIN_SUB], ks) + biases[sb]
                m = jnp.max(s, axis=-1, keepdims=True)
                p = jnp.exp2(s - m)
                den = jnp.sum(p, axis=-1, keepdims=True)
                outs.append(_dot(p.astype(BF16), vs) / den)
                lses.append(m * math.log(2.0) + jnp.log(den))
            o_ref[0, r0:r0 + WIN_SUB, cs] = jnp.where(lo_half, outs[0], outs[1]).astype(BF16)
            l_ref[0, r0:r0 + WIN_SUB, cs] = jnp.where(lo_half, lses[0], lses[1])


def _win_call(q, k, v, batch, seq, dil):
    seg = seq // dil
    tq = min(TQ_WIN, seg)
    nb = tq // WINDOW_RADIUS
    last = seg // WINDOW_RADIUS - 1
    main = pl.BlockSpec((1, tq, A_WIDTH), lambda b, r, j: (b, j, r))
    prev = pl.BlockSpec((1, WINDOW_RADIUS, A_WIDTH), lambda b, r, j: (b, jnp.maximum(j * nb - 1, 0), r))
    nxt = pl.BlockSpec((1, WINDOW_RADIUS, A_WIDTH), lambda b, r, j: (b, jnp.minimum((j + 1) * nb, last), r))
    return pl.pallas_call(
        functools.partial(_win_body, seg_len=seg, tq=tq),
        grid=(batch, dil, seg // tq),
        in_specs=[main, prev, main, nxt, prev, main, nxt],
        out_specs=(main, main),
        out_shape=(jax.ShapeDtypeStruct((batch, seg, dil * A_WIDTH), BF16),
                   jax.ShapeDtypeStruct((batch, seg, dil * A_WIDTH), F32)),
        compiler_params=_cparams(("arbitrary", "arbitrary", "arbitrary")), name=f"win_d{dil}",
    )(q, k, k, k, v, v, v)


def _mla_body(qt_ref, k_ref, vt_ref, o_ref, sa, sb, pa, pb, acc_ref, *, seq, tk):
    tq = qt_ref.shape[2]
    nk = seq // tk
    qts = [qt_ref[0, h * HEAD_PAD:(h + 1) * HEAD_PAD, :] for h in range(B_HEADS)]
    ones = jnp.ones((BF16_SUBLANES, tk), BF16)

    def scores(h, kt, dst):
        start = kt * tk if isinstance(kt, int) else pl.multiple_of(kt * tk, tk)
        dst[h] = _dot(k_ref[0, h, pl.ds(start, tk), :], qts[h])

    def values(h, kt, src, alpha):
        start = kt * tk if isinstance(kt, int) else pl.multiple_of(kt * tk, tk)
        vt = jnp.concatenate([vt_ref[0, h * B_V:(h + 1) * B_V, pl.ds(start, tk)], ones], axis=0)
        acc_ref[h] = alpha * acc_ref[h] + _dot(vt, src[h])

    def softmax(h, src, dst, m):
        st = src[h]
        m_new = jnp.maximum(m, jnp.max(st, axis=0, keepdims=True))
        dst[h] = jnp.exp2(st - m_new).astype(BF16)
        return m_new, jnp.exp2(m - m_new)

    for h in range(B_HEADS):
        scores(h, 0, sa)
        acc_ref[h] = jnp.zeros((B_V + BF16_SUBLANES, tq), F32)
    pb[...] = jnp.zeros_like(pb)

    def step(i, carry):
        out = []
        for h in range(B_HEADS):
            m, alpha_prev = carry[h]
            t0 = 2 * i
            scores(h, t0 + 1, sb)
            values(h, jnp.maximum(t0 - 1, 0), pb, alpha_prev)
            m, alpha0 = softmax(h, sa, pa, m)
            scores(h, jnp.minimum(t0 + 2, nk - 1), sa)
            values(h, t0, pa, alpha0)
            m, alpha1 = softmax(h, sb, pb, m)
            out.append((m, alpha1))
        return tuple(out)

    init = tuple((jnp.full((1, tq), NEG_INF, F32), jnp.ones((1, tq), F32)) for _ in range(B_HEADS))
    res = lax.fori_loop(0, nk // 2, step, init)
    outs = []
    for h in range(B_HEADS):
        values(h, nk - 1, pb, res[h][1])
        acc = acc_ref[h]
        outs.append(acc[:B_V] / acc[B_V:B_V + 1])
    o_ref[...] = jnp.concatenate(outs, axis=0).T.astype(BF16)


def _mla_call(qbt, kb, vbt, batch, seq):
    tq = min(TQ_MLA, seq)
    tk = min(TK_MLA, seq)
    nq = seq // tq
    assert (seq // tk) % 2 == 0
    return pl.pallas_call(
        functools.partial(_mla_body, seq=seq, tk=tk),
        grid=(batch, nq),
        in_specs=[
            pl.BlockSpec((1, B_HEADS * HEAD_PAD, tq), lambda b, qi: (b, 0, qi)),
            pl.BlockSpec((1, B_HEADS, seq, HEAD_PAD), lambda b, qi: (b, 0, 0, 0)),
            pl.BlockSpec((1, B_WIDTH, seq), lambda b, qi: (b, 0, 0)),
        ],
        out_specs=pl.BlockSpec((tq, B_WIDTH), lambda b, qi: (b * nq + qi, 0)),
        out_shape=jax.ShapeDtypeStruct((batch * seq, B_WIDTH), BF16),
        scratch_shapes=[pltpu.VMEM((B_HEADS, tk, tq), F32), pltpu.VMEM((B_HEADS, tk, tq), F32),
                        pltpu.VMEM((B_HEADS, tk, tq), BF16), pltpu.VMEM((B_HEADS, tk, tq), BF16),
                        pltpu.VMEM((B_HEADS, B_V + BF16_SUBLANES, tq), F32)],
        compiler_params=_cparams(("arbitrary", "arbitrary")), name="mla_attn",
    )(qbt, kb, vbt)


def _dft_body(c_ref, ns_ref, x_ref, xf_ref, xfp_ref, xh_ref, o_ref, acc_ref, *, seq):
    k = pl.program_id(1)
    tm = c_ref.shape[0]
    tk = x_ref.shape[1]

    @pl.when(k == 0)
    def _():
        row = lax.broadcasted_iota(jnp.int32, (tm, 1), 0)
        sign = (1 - 2 * (row & 1)).astype(F32) * (seq ** -0.5)
        for b in range(x_ref.shape[0]):
            acc_ref[b] = sign * xh_ref[b, 0:1, 0:C_WIDTH].astype(F32)

    c = c_ref[...]
    ns = ns_ref[...]
    first_row = lax.broadcasted_iota(jnp.int32, (tk, 1), 0) == 0
    has_prev = (k > 0).astype(F32)
    for b in range(x_ref.shape[0]):
        x = x_ref[b].astype(F32)
        carry_in = xfp_ref[b, 7:8, :].astype(F32) * has_prev
        xr = jnp.where(first_row, carry_in, pltpu.roll(xf_ref[b].astype(F32), 1, axis=0))
        sym = (x[:, 0:C_WIDTH] + xr[:, 0:C_WIDTH]).astype(BF16)
        anti = (x[:, C_WIDTH:2 * C_WIDTH] - xr[:, C_WIDTH:2 * C_WIDTH]).astype(BF16)
        acc_ref[b] += _dot(c, sym) + _dot(ns, anti)

    @pl.when(k == pl.num_programs(1) - 1)
    def _():
        o_ref[...] = acc_ref[...].astype(o_ref.dtype)


def _dft_call(cmat, nsmat, x3, x3_flip, batch, seq):
    half = seq // 2
    tm = min(TM_DFT, seq)
    tk = min(TK_DFT, half)
    rows8 = tk // 8
    blk = lambda rows, index: pl.BlockSpec((batch, rows, 2 * C_WIDTH), index)
    y = pl.pallas_call(
        functools.partial(_dft_body, seq=seq), grid=(seq // tm, half // tk),
        in_specs=[pl.BlockSpec((tm, tk), lambda i, k: (i, k)),
                  pl.BlockSpec((tm, tk), lambda i, k: (i, k)),
                  blk(tk, lambda i, k: (0, k, 0)),
                  blk(tk, lambda i, k: (0, k, 0)),
                  blk(8, lambda i, k: (0, jnp.maximum(k * rows8 - 1, 0), 0)),
                  blk(8, lambda i, k: (0, half // 8, 0))],
        out_specs=pl.BlockSpec((batch, tm, C_WIDTH), lambda i, k: (0, i, 0)),
        out_shape=jax.ShapeDtypeStruct((batch, seq, C_WIDTH), BF16),
        scratch_shapes=[pltpu.VMEM((batch, tm, C_WIDTH), F32)],
        compiler_params=_cparams(("arbitrary", "arbitrary")), name="seq_dft",
    )(cmat, nsmat, x3, x3_flip, x3_flip, x3)
    return y.reshape(batch * seq, C_WIDTH)


def _merge_residues(scr, ref, dil):
    tm = scr.shape[1]
    for r in range(dil):
        for c in range(A_WIDTH // LANES):
            col = r * A_WIDTH + c * LANES
            scr[c, pl.ds(r, tm // dil, stride=dil), :] = ref[0, :, col:col + LANES].astype(F32)
    return jnp.concatenate([scr[c] for c in range(A_WIDTH // LANES)], axis=1)


def _out_body(x_ref, o1_ref, o4_ref, o16_ref, l1_ref, l4_ref, l16_ref, ob_ref, oc_ref,
              ga_ref, gb_ref, gc_ref, wa_ref, wb_ref, wc_ref, y_ref, so4, sl4, so16, sl16):
    l1 = l1_ref[0]
    l4 = _merge_residues(sl4, l4_ref, DILATIONS[1])
    l16 = _merge_residues(sl16, l16_ref, DILATIONS[2])
    o4 = _merge_residues(so4, o4_ref, DILATIONS[1])
    o16 = _merge_residues(so16, o16_ref, DILATIONS[2])
    m = jnp.maximum(jnp.maximum(l1, l4), l16)
    e1, e4, e16 = jnp.exp(l1 - m), jnp.exp(l4 - m), jnp.exp(l16 - m)
    oa = (o1_ref[0].astype(F32) * e1 + o4 * e4 + o16 * e16) / (e1 + e4 + e16)
    na = _rms(oa, ga_ref[...]).astype(BF16)
    nb = _rms(ob_ref[...].astype(F32), gb_ref[...]).astype(BF16)
    nc = _rms(oc_ref[...].astype(F32), gc_ref[...]).astype(BF16)
    y_ref[...] = (x_ref[...] + _dot(na, wa_ref[...]) + _dot(nb, wb_ref[...]) + _dot(nc, wc_ref[...]))


def _out_call(x2, oa, la, ob, oc, ga, gb, gc, wa, wb, wc, batch, seq):
    t, d = x2.shape
    tm = min(TM_PROJ, seq)
    ns = seq // tm
    row = lambda i: (i, 0)
    const = lambda i: (0, 0)
    wide = lambda w: pl.BlockSpec((tm, w), row)
    view = lambda dil: pl.BlockSpec((1, tm // dil, dil * A_WIDTH), lambda i: (i // ns, i % ns, 0))
    views = [view(dil) for dil in DILATIONS]
    slab = pltpu.VMEM((A_WIDTH // LANES, tm, LANES), F32)
    return pl.pallas_call(
        _out_body, grid=(t // tm,),
        in_specs=[wide(d)] + views + views + [wide(B_WIDTH), wide(C_WIDTH),
                  pl.BlockSpec((1, A_WIDTH), const), pl.BlockSpec((1, B_WIDTH), const),
                  pl.BlockSpec((1, C_WIDTH), const),
                  pl.BlockSpec((A_WIDTH, d), const), pl.BlockSpec((B_WIDTH, d), const),
                  pl.BlockSpec((C_WIDTH, d), const)],
        out_specs=wide(d),
        out_shape=jax.ShapeDtypeStruct((t, d), F32),
        scratch_shapes=[slab, slab, slab, slab],
        compiler_params=_cparams(("arbitrary",)), name="out_proj",
    )(x2, oa[0], oa[1], oa[2], la[0], la[1], la[2], ob, oc, ga, gb, gc, wa, wb, wc)


def _ffn_body(x_ref, g_ref, wg_ref, wu_ref, wd_ref, y_ref, h_ref, acc_ref):
    f = pl.program_id(1)

    @pl.when(f == 0)
    def _():
        h_ref[...] = _rms(x_ref[...], g_ref[...]).astype(BF16)
        acc_ref[...] = jnp.zeros_like(acc_ref)

    h = h_ref[...]
    gate = _dot(h, wg_ref[...])
    up = _dot(h, wu_ref[...])
    acc_ref[...] += _dot((jax.nn.silu(gate) * up).astype(BF16), wd_ref[...])

    @pl.when(f == pl.num_programs(1) - 1)
    def _():
        y_ref[...] = x_ref[...] + acc_ref[...]


def _ffn_tile(d_ff):
    for steps in (1, 2, 4, 11, 22):
        if d_ff % steps == 0 and (d_ff // steps) % LANES == 0 and d_ff // steps <= 1536:
            return d_ff // steps
    return d_ff


def _ffn_call(x2, g, wg, wu, wd):
    t, d = x2.shape
    d_ff = wg.shape[1]
    tm = min(TM_FFN, t)
    tf = _ffn_tile(d_ff)
    return pl.pallas_call(
        _ffn_body, grid=(t // tm, d_ff // tf),
        in_specs=[pl.BlockSpec((tm, d), lambda i, f: (i, 0)),
                  pl.BlockSpec((1, d), lambda i, f: (0, 0)),
                  pl.BlockSpec((d, tf), lambda i, f: (0, f)),
                  pl.BlockSpec((d, tf), lambda i, f: (0, f)),
                  pl.BlockSpec((tf, d), lambda i, f: (f, 0))],
        out_specs=pl.BlockSpec((tm, d), lambda i, f: (i, 0)),
        out_shape=jax.ShapeDtypeStruct((t, d), F32),
        scratch_shapes=[pltpu.VMEM((tm, d), BF16), pltpu.VMEM((tm, d), F32)],
        compiler_params=_cparams(("arbitrary", "arbitrary")), name="dense_ffn",
    )(x2, g, wg, wu, wd)


def _router_body(x_ref, g_ref, wr_ref, idx_ref, gate_ref):
    h = _rms(x_ref[...], g_ref[...])
    logits = lax.dot_general(wr_ref[...], h, (((1,), (1,)), ((), ())),
                             precision=lax.Precision.HIGHEST, preferred_element_type=F32)
    eid = lax.broadcasted_iota(jnp.int32, logits.shape, 0)
    m1 = jnp.max(logits, axis=0, keepdims=True)
    i1 = jnp.min(jnp.where(logits == m1, eid, N_EXPERTS), axis=0, keepdims=True)
    rest = jnp.where(eid == i1, -jnp.inf, logits)
    m2 = jnp.max(rest, axis=0, keepdims=True)
    i2 = jnp.min(jnp.where(rest == m2, eid, N_EXPERTS), axis=0, keepdims=True)
    e2 = jnp.exp(m2 - m1)
    idx_ref[...] = jnp.concatenate([i1, i2], axis=0)
    gate_ref[...] = jnp.concatenate([1.0 / (1.0 + e2), e2 / (1.0 + e2)], axis=0)


def _router_call(x2, g, wr_t):
    t, d = x2.shape
    tm = min(TM_ROUTER, t)
    return pl.pallas_call(
        _router_body, grid=(t // tm,),
        in_specs=[pl.BlockSpec((tm, d), lambda i: (i, 0)),
                  pl.BlockSpec((1, d), lambda i: (0, 0)),
                  pl.BlockSpec((N_EXPERTS, d), lambda i: (0, 0))],
        out_specs=(pl.BlockSpec((TOP_K, tm), lambda i: (0, i)),
                   pl.BlockSpec((TOP_K, tm), lambda i: (0, i))),
        out_shape=(jax.ShapeDtypeStruct((TOP_K, t), jnp.int32),
                   jax.ShapeDtypeStruct((TOP_K, t), F32)),
        compiler_params=_cparams(("arbitrary",)), name="router",
    )(x2, g, wr_t)


def _expert_body(bexp_ref, tok_cur, tok_next, dst_prev, dst_cur, x_hbm, g_ref, wg_ref, wu_ref, wd_ref,
                 y_hbm, xb0, xb1, hbuf, ac0, ac1, gsem, ssem):
    i = pl.program_id(0)
    f = pl.program_id(1)
    nb = pl.num_programs(0)
    nf = pl.num_programs(1)
    rows = xb0.shape[0]
    chunk = rows // EXP_STEPS

    def variant(xcur, xnxt, acur, aprv, s, compute):
        def row_in(r, src_row, buf, sem):
            return pltpu.make_async_copy(x_hbm.at[pl.ds(src_row, 1)], buf.at[pl.ds(r, 1)], sem)

        def row_out(r, dst_row, buf, sem):
            return pltpu.make_async_copy(buf.at[pl.ds(r, 1)], y_hbm.at[pl.ds(dst_row, 1)], sem)

        def all_rows(fn):
            def body(r, c):
                fn(r)
                return c
            lax.fori_loop(0, rows, body, 0, unroll=8)

        @pl.when(f == 0)
        def _():
            @pl.when(i == 0)
            def _():
                all_rows(lambda r: row_in(r, tok_cur[0, 0, r], xcur, gsem.at[s]).start())
                aprv[...] = jnp.zeros_like(aprv)

            all_rows(lambda r: row_in(r, 0, xcur, gsem.at[s]).wait())
            if compute:
                hbuf[...] = _rms(xcur[...], g_ref[...]).astype(BF16)
            acur[...] = jnp.zeros_like(acur)

        base = pl.multiple_of(f * chunk, chunk)
        for k in range(chunk):
            r = base + k
            row_in(r, tok_next[0, 0, r], xnxt, gsem.at[1 - s]).start()
            row_out(r, dst_prev[0, 0, r], aprv, ssem.at[1 - s]).start()
        if compute:
            h = hbuf[...]
            gate = _dot(h, wg_ref[0])
            up = _dot(h, wu_ref[0])
            acur[...] += _dot((jax.nn.silu(gate) * up).astype(BF16), wd_ref[0])

        @pl.when(f == nf - 1)
        def _():
            all_rows(lambda r: row_out(r, 0, aprv, ssem.at[1 - s]).wait())

            @pl.when(i == nb - 1)
            def _():
                all_rows(lambda r: row_out(r, dst_cur[0, 0, r], acur, ssem.at[s]).start())
                all_rows(lambda r: row_out(r, 0, acur, ssem.at[s]).wait())
                all_rows(lambda r: row_in(r, 0, xnxt, gsem.at[1 - s]).wait())

    even = i % 2 == 0
    padding_only = bexp_ref[i] >= N_EXPERTS
    for is_even, bufs in ((True, (xb0, xb1, ac0, ac1, 0)), (False, (xb1, xb0, ac1, ac0, 1))):
        for compute in (True, False):
            @pl.when((even == is_even) & (padding_only != compute))
            def _(bufs=bufs, compute=compute):
                variant(*bufs, compute)


def _expert_call(x2, g, wg, wu, wd, bexp, row_tok, row_dst, n_out_rows):
    t, d = x2.shape
    n_blocks = bexp.shape[0]
    d_ff = wg.shape[2]
    assert d_ff % EXP_STEPS == 0 and (d_ff // EXP_STEPS) % LANES == 0
    tf = d_ff // EXP_STEPS
    last = n_blocks - 1
    rows_of = lambda which: pl.BlockSpec((1, 1, TM_EXP), which, memory_space=pltpu.SMEM)
    grid_spec = pltpu.PrefetchScalarGridSpec(
        num_scalar_prefetch=1,
        grid=(n_blocks, EXP_STEPS),
        in_specs=[
            rows_of(lambda i, f, be: (i, 0, 0)),
            rows_of(lambda i, f, be: (jnp.minimum(i + 1, last), 0, 0)),
            rows_of(lambda i, f, be: (jnp.maximum(i - 1, 0), 0, 0)),
            rows_of(lambda i, f, be: (i, 0, 0)),
            pl.BlockSpec(memory_space=pl.ANY),
            pl.BlockSpec((1, d), lambda i, f, be: (0, 0)),
            pl.BlockSpec((1, d, tf), lambda i, f, be: (be[i] % N_EXPERTS, 0, f)),
            pl.BlockSpec((1, d, tf), lambda i, f, be: (be[i] % N_EXPERTS, 0, f)),
            pl.BlockSpec((1, tf, d), lambda i, f, be: (be[i] % N_EXPERTS, f, 0)),
        ],
        out_specs=pl.BlockSpec(memory_space=pl.ANY),
        scratch_shapes=[pltpu.VMEM((TM_EXP, d), F32), pltpu.VMEM((TM_EXP, d), F32),
                        pltpu.VMEM((TM_EXP, d), BF16),
                        pltpu.VMEM((TM_EXP, d), F32), pltpu.VMEM((TM_EXP, d), F32),
                        pltpu.SemaphoreType.DMA((2,)), pltpu.SemaphoreType.DMA((2,))],
    )
    return pl.pallas_call(
        _expert_body, grid_spec=grid_spec,
        out_shape=jax.ShapeDtypeStruct((n_out_rows, d), F32),
        compiler_params=_cparams(("arbitrary", "arbitrary")), name="experts",
    )(bexp, row_tok, row_tok, row_dst, row_dst, x2, g, wg, wu, wd)


def _combine_body(x_ref, y0_ref, y1_ref, gate_ref, gf_ref, o_ref, *, final):
    g0 = gate_ref[:, 0:1]
    g1 = gate_ref[:, 1:2]
    y = x_ref[...] + (y0_ref[...] * g0 + y1_ref[...] * g1)
    o_ref[...] = _rms(y, gf_ref[...]) if final else y


def _combine_call(x2, y2, gates_t, gfinal, final):
    t, d = x2.shape
    tm = min(TM_ROUTER, t)
    nt = t // tm
    return pl.pallas_call(
        functools.partial(_combine_body, final=final), grid=(nt,),
        in_specs=[pl.BlockSpec((tm, d), lambda i: (i, 0)),
                  pl.BlockSpec((tm, d), lambda i: (i, 0)),
                  pl.BlockSpec((tm, d), lambda i: (i + nt, 0)),
                  pl.BlockSpec((tm, TOP_K), lambda i: (i, 0)),
                  pl.BlockSpec((1, d), lambda i: (0, 0))],
        out_specs=pl.BlockSpec((tm, d), lambda i: (i, 0)),
        out_shape=jax.ShapeDtypeStruct((t, d), F32),
        compiler_params=_cparams(("arbitrary",)), name="moe_combine",
    )(x2, y2, y2, gates_t, gfinal)


def _final_norm_body(x_ref, g_ref, o_ref):
    o_ref[...] = _rms(x_ref[...], g_ref[...])


def _final_norm_call(x2, g):
    t, d = x2.shape
    tm = min(TM_ROUTER, t)
    return pl.pallas_call(
        _final_norm_body, grid=(t // tm,),
        in_specs=[pl.BlockSpec((tm, d), lambda i: (i, 0)), pl.BlockSpec((1, d), lambda i: (0, 0))],
        out_specs=pl.BlockSpec((tm, d), lambda i: (i, 0)),
        out_shape=jax.ShapeDtypeStruct((t, d), F32),
        compiler_params=_cparams(("arbitrary",)), name="final_norm",
    )(x2, g)


def _route_plan(idx, t):
    n_assign = TOP_K * t
    flat_e = idx.reshape(-1)
    order = jnp.argsort(flat_e).astype(jnp.int32)
    counts = jnp.sum(flat_e[:, None] == jnp.arange(N_EXPERTS, dtype=jnp.int32)[None, :], axis=0,
                     dtype=jnp.int32)
    padded = ((counts + TM_EXP - 1) // TM_EXP) * TM_EXP
    start = jnp.cumsum(counts) - counts
    cum_padded = jnp.cumsum(padded)
    start_padded = cum_padded - padded
    n_blocks = -(-n_assign // TM_EXP) + N_EXPERTS
    block_start = jnp.arange(n_blocks, dtype=jnp.int32) * TM_EXP
    bexp = jnp.minimum(jnp.sum(block_start[:, None] >= cum_padded[None, :], axis=1),
                       N_EXPERTS - 1).astype(jnp.int32)
    off = block_start[:, None] + jnp.arange(TM_EXP, dtype=jnp.int32)[None, :] - start_padded[bexp][:, None]
    valid = off < counts[bexp][:, None]
    a = order[jnp.clip(start[bexp][:, None] + off, 0, n_assign - 1)]
    bexp = bexp + N_EXPERTS * (~jnp.any(valid, axis=1)).astype(jnp.int32)
    pad_rank = (jnp.cumsum((~valid).reshape(-1).astype(jnp.int32)) - 1).reshape(valid.shape)
    row_tok = jnp.where(valid, a % t, 0).astype(jnp.int32)
    row_dst = jnp.where(valid, a, n_assign + pad_rank).astype(jnp.int32)
    n_out_rows = n_blocks * TM_EXP
    return bexp, row_tok.reshape(n_blocks, 1, TM_EXP), row_dst.reshape(n_blocks, 1, TM_EXP), n_out_rows


def _rope_tables(seq):
    pos = jnp.arange(seq, dtype=F32)[:, None]
    half_a = A_ROT_DIM // 2
    ang_a = pos * (ROPE_THETA ** (-jnp.arange(half_a, dtype=F32) / half_a))[None, :]
    one = jnp.ones((seq, HEAD_DIM - A_ROT_DIM), F32)
    cos_head = jnp.concatenate([jnp.cos(ang_a), jnp.cos(ang_a), one], axis=1)
    sin_head = jnp.concatenate([jnp.sin(ang_a), jnp.sin(ang_a), 0 * one], axis=1)
    cosa = jnp.tile(cos_head, (1, A_HEADS))
    sina = jnp.tile(sin_head, (1, A_HEADS))
    half_b = B_ROPE // 2
    ang_b = pos * (ROPE_THETA ** (-jnp.arange(half_b, dtype=F32) / half_b))[None, :]
    zl = jnp.zeros((seq, B_NOPE), F32)
    zr = jnp.zeros((seq, HEAD_PAD - B_NOPE - B_ROPE), F32)
    cosk = jnp.concatenate([zl, jnp.cos(ang_b), jnp.cos(ang_b), zr], axis=1)
    sink = jnp.concatenate([zl, jnp.sin(ang_b), jnp.sin(ang_b), zr], axis=1)
    return cosa, sina, cosk, sink, jnp.cos(ang_b).T, jnp.sin(ang_b).T


def _rotate_half_matrix():
    r = np.zeros((A_WIDTH, A_WIDTH), np.float32)
    half = A_ROT_DIM // 2
    for hd in range(A_HEADS):
        for i in range(half):
            r[hd * HEAD_DIM + half + i, hd * HEAD_DIM + i] = -1.0
            r[hd * HEAD_DIM + i, hd * HEAD_DIM + half + i] = 1.0
    return jnp.asarray(r, BF16)


def _dft_constants(seq):
    lo_n = 64
    hi_n = seq // lo_n
    k = jnp.arange(seq, dtype=jnp.int32)[:, None]
    ang_hi = ((k * jnp.arange(hi_n // 2, dtype=jnp.int32)[None, :]) % hi_n).astype(F32) * (2.0 * np.pi / hi_n)
    ang_lo = ((k * jnp.arange(lo_n, dtype=jnp.int32)[None, :]) % seq).astype(F32) * (2.0 * np.pi / seq)
    ch, sh = jnp.cos(ang_hi)[:, :, None], jnp.sin(ang_hi)[:, :, None]
    cl, sl = jnp.cos(ang_lo)[:, None, :], jnp.sin(ang_lo)[:, None, :]
    scale = seq ** -0.5
    cmat = ((ch * cl - sh * sl) * scale).astype(BF16).reshape(seq, seq // 2)
    nsmat = ((sh * cl + ch * sl) * -scale).astype(BF16).reshape(seq, seq // 2)
    c = np.arange(C_GROUP_DIM)
    angc = 2.0 * np.pi * ((c[:, None] * c[None, :]) % C_GROUP_DIM) / C_GROUP_DIM
    eye = np.eye(C_GROUPS)
    cs = np.concatenate([np.kron(eye, np.cos(angc)), np.kron(eye, np.sin(angc))], axis=1)
    return cmat, nsmat, jnp.asarray(cs * C_GROUP_DIM ** -0.5, BF16)


def _layer_weights(w_in, w_uq, w_ukv, w_out):
    d = w_in.shape[0]
    c_kpe = 3 * A_WIDTH + Q_LORA + KV_LORA
    w_kpe = w_in[:, c_kpe:c_kpe + B_ROPE]
    half = B_ROPE // 2
    w_kpe_rot = jnp.concatenate([-w_kpe[:, half:], w_kpe[:, :half]], axis=1)
    zl = jnp.zeros((d, B_NOPE), w_in.dtype)
    zr = jnp.zeros((d, HEAD_PAD - B_NOPE - B_ROPE), w_in.dtype)
    wm = jnp.concatenate([w_in[:, :c_kpe], zl, w_kpe, zr, zl, w_kpe_rot, zr,
                          w_in[:, c_kpe + B_ROPE:]], axis=1).astype(BF16)
    uq = w_uq.reshape(Q_LORA, B_HEADS, B_NOPE + B_ROPE)
    uq = jnp.pad(uq, ((0, 0), (0, 0), (0, HEAD_PAD - B_NOPE - B_ROPE)))
    wuq_t = uq.reshape(Q_LORA, B_HEADS * HEAD_PAD).T.astype(BF16)
    ukv = w_ukv.reshape(KV_LORA, B_HEADS, B_NOPE + B_V)
    wkn = jnp.pad(ukv[:, :, :B_NOPE], ((0, 0), (0, 0), (0, HEAD_PAD - B_NOPE)))
    wkn = wkn.reshape(KV_LORA, B_HEADS * HEAD_PAD).astype(BF16)
    wv_t = ukv[:, :, B_NOPE:].reshape(KV_LORA, B_WIDTH).T.astype(BF16)
    wo = w_out.astype(BF16)
    return wm, wuq_t, wkn, wv_t, wo[:A_WIDTH], wo[A_WIDTH:A_WIDTH + B_WIDTH], wo[A_WIDTH + B_WIDTH:]


def kernel(x, attn_norm, w_in, q_norm, w_uq, kv_norm, w_ukv, mix_gain, w_out, ffn_norm,
           w_ffn_gate, w_ffn_up, w_ffn_down, w_router, w_exp_gate, w_exp_up, w_exp_down,
           final_norm):
    batch, seq, d = x.shape
    depth = w_in.shape[0]
    t = batch * seq
    assert seq % (max(DILATIONS) * 2 * WIN_SUB) == 0
    tabs = _rope_tables(seq)
    rot = _rotate_half_matrix()
    cmat, nsmat, cs = _dft_constants(seq)
    row = lambda v: v.reshape(1, -1)

    x2 = x.reshape(t, d)
    for l in range(depth):
        wm, wuq_t, wkn, wv_t, wa, wb, wc = _layer_weights(w_in[l], w_uq[l], w_ukv[l], w_out[l])
        qkv_views, (qbt, kb, vbt, xcs, xcs_flip) = _proj_call(
            x2, row(attn_norm[l]), wm, rot, tabs, row(q_norm[l]), wuq_t, row(kv_norm[l]), wkn, wv_t,
            cs, batch, seq)
        win = [_win_call(*(a.reshape(batch, seq // dil, dil * A_WIDTH) for a in qkv), batch, seq, dil)
               for dil, qkv in zip(DILATIONS, qkv_views)]
        ob = _mla_call(qbt, kb, vbt, batch, seq)
        oc = _dft_call(cmat, nsmat, xcs, xcs_flip, batch, seq)
        g = mix_gain[l]
        x2 = _out_call(x2, [w[0] for w in win], [w[1] for w in win], ob, oc,
                       row(g[:A_WIDTH]), row(g[A_WIDTH:A_WIDTH + B_WIDTH]), row(g[A_WIDTH + B_WIDTH:]),
                       wa, wb, wc, batch, seq)
        i = l // 2
        if l % 2 == 0:
            x2 = _ffn_call(x2, row(ffn_norm[l]), w_ffn_gate[i].astype(BF16), w_ffn_up[i].astype(BF16),
                           w_ffn_down[i].astype(BF16))
            if l == depth - 1:
                x2 = _final_norm_call(x2, row(final_norm))
        else:
            idx, gates = _router_call(x2, row(ffn_norm[l]), w_router[i].T)
            bexp, row_tok, row_dst, n_out_rows = _route_plan(idx, t)
            y2 = _expert_call(x2, row(ffn_norm[l]), w_exp_gate[i].astype(BF16), w_exp_up[i].astype(BF16),
                              w_exp_down[i].astype(BF16), bexp, row_tok, row_dst, n_out_rows)
            x2 = _combine_call(x2, y2, gates.T, row(final_norm), final=(l == depth - 1))
    return x2.reshape(batch, seq, d)
```

```python
import functools
import math

import jax
import jax.numpy as jnp
import numpy as np
from jax import lax
from jax.experimental import pallas as pl
from jax.experimental.pallas import tpu as pltpu

F32 = jnp.float32
BF16 = jnp.bfloat16

HEAD_DIM = 64
A_HEADS = 6
A_WIDTH = A_HEADS * HEAD_DIM
A_ROT_DIM = HEAD_DIM // 4
DILATIONS = (1, 4, 16)
WINDOW_RADIUS = 64
B_HEADS = 6
B_NOPE = 64
B_ROPE = 32
B_V = 64
B_WIDTH = B_HEADS * B_V
Q_LORA = 384
KV_LORA = 128
C_GROUPS = 4
C_GROUP_DIM = 64
C_WIDTH = C_GROUPS * C_GROUP_DIM
ROPE_THETA = 500000.0
N_EXPERTS = 8
TOP_K = 2
RMS_EPS = 1e-6
NEG_INF = -1e30

LANES = 128
BF16_SUBLANES = 16
HEAD_PAD = 128
VMEM_LIMIT_BYTES = 56 * 1024 * 1024

TM_PROJ = 512
TQ_WIN = 512
WIN_SUB = 128
TQ_MLA = 512
TK_MLA = 256
TM_DFT = 1024
TK_DFT = 512
TM_FFN = 512
TM_ROUTER = 512
EXP_STEPS = 7
TM_EXP = 128 * EXP_STEPS


def _cparams(sem):
    return pltpu.CompilerParams(dimension_semantics=sem, vmem_limit_bytes=VMEM_LIMIT_BYTES)


def _rms(x, g):
    return x * lax.rsqrt(jnp.mean(x * x, axis=-1, keepdims=True) + RMS_EPS) * g


def _dot(a, b):
    return jnp.dot(a, b, preferred_element_type=F32)


def _dot_nt(a, b):
    return lax.dot_general(a, b, (((1,), (1,)), ((), ())), preferred_element_type=F32)


def _split_residues(scr, val, nat_ref, view_refs):
    tm = val.shape[0]
    nat_ref[...] = val.astype(BF16)
    for c in range(A_WIDTH // LANES):
        scr[c] = val[:, c * LANES:(c + 1) * LANES]
    for dil, ref in zip(DILATIONS[1:], view_refs):
        for r in range(dil):
            for c in range(A_WIDTH // LANES):
                col = r * A_WIDTH + c * LANES
                ref[0, :, col:col + LANES] = scr[c, pl.ds(r, tm // dil, stride=dil), :].astype(BF16)


def _proj_body(x_ref, g_ref, wm_ref, rot_ref, cosa_ref, sina_ref, cosk_ref, sink_ref,
               cosq_ref, sinq_ref, qn_ref, wuq_ref, kvn_ref, wkn_ref, wv_ref, cs_ref, flip_ref,
               qa_ref, qa4_ref, qa16_ref, ka_ref, ka4_ref, ka16_ref, va_ref, va4_ref, va16_ref,
               qbt_ref, kb_ref, vbt_ref, xcs_ref, xcs_flip_ref, scr):
    h = _rms(x_ref[...], g_ref[...]).astype(BF16)
    z = _dot(h, wm_ref[...])
    o = 0
    qa = z[:, o:o + A_WIDTH]; o += A_WIDTH
    ka = z[:, o:o + A_WIDTH]; o += A_WIDTH
    va = z[:, o:o + A_WIDTH]; o += A_WIDTH
    cq = z[:, o:o + Q_LORA]; o += Q_LORA
    ckv = z[:, o:o + KV_LORA]; o += KV_LORA
    kpe = z[:, o:o + HEAD_PAD]; o += HEAD_PAD
    kpe_rot = z[:, o:o + HEAD_PAD]; o += HEAD_PAD
    fc = z[:, o:o + C_WIDTH]

    cosa = cosa_ref[...]
    sina = sina_ref[...]
    rot = rot_ref[...]
    _split_residues(scr, qa * cosa + _dot(qa.astype(BF16), rot) * sina, qa_ref, (qa4_ref, qa16_ref))
    _split_residues(scr, ka * cosa + _dot(ka.astype(BF16), rot) * sina, ka_ref, (ka4_ref, ka16_ref))
    _split_residues(scr, va, va_ref, (va4_ref, va16_ref))

    cqn = _rms(cq, qn_ref[...]).astype(BF16)
    qbt = _dot_nt(wuq_ref[...], cqn) * ((B_NOPE + B_ROPE) ** -0.5 * math.log2(math.e))
    cosq = cosq_ref[...]
    sinq = sinq_ref[...]
    half = B_ROPE // 2
    for hd in range(B_HEADS):
        r0 = hd * HEAD_PAD
        x1 = qbt[r0 + B_NOPE:r0 + B_NOPE + half]
        x2 = qbt[r0 + B_NOPE + half:r0 + B_NOPE + B_ROPE]
        qbt_ref[0, r0:r0 + B_NOPE, :] = qbt[r0:r0 + B_NOPE].astype(BF16)
        qbt_ref[0, r0 + B_NOPE:r0 + B_NOPE + half, :] = (x1 * cosq - x2 * sinq).astype(BF16)
        qbt_ref[0, r0 + B_NOPE + half:r0 + B_NOPE + B_ROPE, :] = (x2 * cosq + x1 * sinq).astype(BF16)
        qbt_ref[0, r0 + B_NOPE + B_ROPE:r0 + HEAD_PAD, :] = jnp.zeros(
            (HEAD_PAD - B_NOPE - B_ROPE, qbt.shape[1]), BF16)

    ckvn = _rms(ckv, kvn_ref[...]).astype(BF16)
    kn = _dot(ckvn, wkn_ref[...])
    kpe_r = kpe * cosk_ref[...] + kpe_rot * sink_ref[...]
    for hd in range(B_HEADS):
        kb_ref[0, hd] = (kn[:, hd * HEAD_PAD:(hd + 1) * HEAD_PAD] + kpe_r).astype(BF16)
    vbt_ref[0] = _dot_nt(wv_ref[...], ckvn).astype(BF16)

    xcs = _dot(fc.astype(BF16), cs_ref[...]).astype(BF16)
    xcs_ref[0] = xcs
    xcs_flip_ref[0] = _dot(flip_ref[...], xcs).astype(BF16)


def _proj_call(x2, g, wm, rot, tabs, qn, wuq, kvn, wkn, wv, cs, batch, seq):
    t, d = x2.shape
    tm = min(TM_PROJ, seq)
    ns = seq // tm
    nm = wm.shape[1]
    cosa, sina, cosk, sink, cosq, sinq = tabs
    row = lambda i: (i, 0)
    pos = lambda i: (i % ns, 0)
    post = lambda i: (0, i % ns)
    const = lambda i: (0, 0)
    in_specs = [
        pl.BlockSpec((tm, d), row),
        pl.BlockSpec((1, d), const),
        pl.BlockSpec((d, nm), const),
        pl.BlockSpec((A_WIDTH, A_WIDTH), const),
        pl.BlockSpec((tm, A_WIDTH), pos),
        pl.BlockSpec((tm, A_WIDTH), pos),
        pl.BlockSpec((tm, HEAD_PAD), pos),
        pl.BlockSpec((tm, HEAD_PAD), pos),
        pl.BlockSpec((B_ROPE // 2, tm), post),
        pl.BlockSpec((B_ROPE // 2, tm), post),
        pl.BlockSpec((1, Q_LORA), const),
        pl.BlockSpec((B_HEADS * HEAD_PAD, Q_LORA), const),
        pl.BlockSpec((1, KV_LORA), const),
        pl.BlockSpec((KV_LORA, B_HEADS * HEAD_PAD), const),
        pl.BlockSpec((B_WIDTH, KV_LORA), const),
        pl.BlockSpec((C_WIDTH, 2 * C_WIDTH), const),
        pl.BlockSpec((tm, tm), const),
    ]
    bpos = lambda i: (i // ns, i % ns, 0)
    a_shapes, a_specs = [], []
    for _ in range(3):
        a_shapes.append(jax.ShapeDtypeStruct((t, A_WIDTH), BF16))
        a_specs.append(pl.BlockSpec((tm, A_WIDTH), row))
        for dil in DILATIONS[1:]:
            a_shapes.append(jax.ShapeDtypeStruct((batch, seq // dil, dil * A_WIDTH), BF16))
            a_specs.append(pl.BlockSpec((1, tm // dil, dil * A_WIDTH), bpos))
    out_shape = (
        *a_shapes,
        jax.ShapeDtypeStruct((batch, B_HEADS * HEAD_PAD, seq), BF16),
        jax.ShapeDtypeStruct((batch, B_HEADS, seq, HEAD_PAD), BF16),
        jax.ShapeDtypeStruct((batch, B_WIDTH, seq), BF16),
        jax.ShapeDtypeStruct((batch, seq, 2 * C_WIDTH), BF16),
        jax.ShapeDtypeStruct((batch, seq, 2 * C_WIDTH), BF16),
    )
    out_specs = (
        *a_specs,
        pl.BlockSpec((1, B_HEADS * HEAD_PAD, tm), lambda i: (i // ns, 0, i % ns)),
        pl.BlockSpec((1, B_HEADS, tm, HEAD_PAD), lambda i: (i // ns, 0, i % ns, 0)),
        pl.BlockSpec((1, B_WIDTH, tm), lambda i: (i // ns, 0, i % ns)),
        pl.BlockSpec((1, tm, 2 * C_WIDTH), bpos),
        pl.BlockSpec((1, tm, 2 * C_WIDTH), lambda i: (i // ns, ns - 1 - i % ns, 0)),
    )
    flip = jnp.asarray(np.eye(tm, dtype=np.float32)[::-1], BF16)
    outs = pl.pallas_call(
        _proj_body, grid=(t // tm,), in_specs=in_specs, out_specs=out_specs, out_shape=out_shape,
        scratch_shapes=[pltpu.VMEM((A_WIDTH // LANES, tm, LANES), F32)],
        compiler_params=_cparams(("arbitrary",)), name="proj",
    )(x2, g, wm, rot, cosa, sina, cosk, sink, cosq, sinq, qn, wuq, kvn, wkn, wv, cs, flip)
    nd = len(DILATIONS)
    qkv_views = [tuple(outs[a * nd + j] for a in range(3)) for j in range(nd)]
    return qkv_views, outs[3 * nd:]


def _win_body(q_ref, kp_ref, km_ref, kn_ref, vp_ref, vm_ref, vn_ref, o_ref, l_ref, *, seg_len, tq):
    j = pl.program_id(2)
    span = WIN_SUB + 2 * WINDOW_RADIUS
    lane = lax.broadcasted_iota(jnp.int32, (1, LANES), 1)
    lo_half = lane < HEAD_DIM
    rel = (lax.broadcasted_iota(jnp.int32, (WIN_SUB, span), 1) - WINDOW_RADIUS
           - lax.broadcasted_iota(jnp.int32, (WIN_SUB, span), 0))
    band = jnp.abs(rel) <= WINDOW_RADIUS
    kcol = lax.broadcasted_iota(jnp.int32, (1, span), 1)
    n_sub = tq // WIN_SUB
    biases = []
    for sb in range(n_sub):
        kpos = j * tq + sb * WIN_SUB - WINDOW_RADIUS + kcol
        biases.append(jnp.where(band & (kpos >= 0) & (kpos < seg_len), 0.0, NEG_INF))
    for pair in range(A_HEADS // 2):
        cs = slice(pair * LANES, (pair + 1) * LANES)
        q2 = q_ref[0, :, cs] * (HEAD_DIM ** -0.5 * math.log2(math.e))
        qh = (jnp.where(lo_half, q2, 0).astype(BF16), jnp.where(lo_half, 0, q2).astype(BF16))
        kcat = jnp.concatenate([kp_ref[0, :, cs], km_ref[0, :, cs], kn_ref[0, :, cs]], axis=0)
        vcat = jnp.concatenate([vp_ref[0, :, cs], vm_ref[0, :, cs], vn_ref[0, :, cs]], axis=0)
        for sb in range(n_sub):
            r0 = sb * WIN_SUB
            ks = kcat[r0:r0 + span]
            vs = vcat[r0:r0 + span]
            outs, lses = [], []
            for hh in range(2):
                s = _dot_nt(qh[hh][r0:r0 + WIN_SUB], ks) + biases[sb]
                m = jnp.max(s, axis=-1, keepdims=True)
                p = jnp.exp2(s - m)
                den = jnp.sum(p, axis=-1, keepdims=True)
                outs.append(_dot(p.astype(BF16), vs) / den)
                lses.append(m * math.log(2.0) + jnp.log(den))
            o_ref[0, r0:r0 + WIN_SUB, cs] = jnp.where(lo_half, outs[0], outs[1]).astype(BF16)
            l_ref[0, r0:r0 + WIN_SUB, cs] = jnp.where(lo_half, lses[0], lses[1])


def _win_call(q, k, v, batch, seq, dil):
    seg = seq // dil
    tq = min(TQ_WIN, seg)
    nb = tq // WINDOW_RADIUS
    last = seg // WINDOW_RADIUS - 1
    main = pl.BlockSpec((1, tq, A_WIDTH), lambda b, r, j: (b, j, r))
    prev = pl.BlockSpec((1, WINDOW_RADIUS, A_WIDTH), lambda b, r, j: (b, jnp.maximum(j * nb - 1, 0), r))
    nxt = pl.BlockSpec((1, WINDOW_RADIUS, A_WIDTH), lambda b, r, j: (b, jnp.minimum((j + 1) * nb, last), r))
    return pl.pallas_call(
        functools.partial(_win_body, seg_len=seg, tq=tq),
        grid=(batch, dil, seg // tq),
        in_specs=[main, prev, main, nxt, prev, main, nxt],
        out_specs=(main, main),
        out_shape=(jax.ShapeDtypeStruct((batch, seg, dil * A_WIDTH), BF16),
                   jax.ShapeDtypeStruct((batch, seg, dil * A_WIDTH), F32)),
        compiler_params=_cparams(("arbitrary", "arbitrary", "arbitrary")), name=f"win_d{dil}",
    )(q, k, k, k, v, v, v)


def _mla_body(qt_ref, k_ref, vt_ref, o_ref, sa, sb, pa, pb, acc_ref, *, seq, tk):
    tq = qt_ref.shape[2]
    nk = seq // tk
    qts = [qt_ref[0, h * HEAD_PAD:(h + 1) * HEAD_PAD, :] for h in range(B_HEADS)]
    ones = jnp.ones((BF16_SUBLANES, tk), BF16)

    def scores(h, kt, dst):
        start = kt * tk if isinstance(kt, int) else pl.multiple_of(kt * tk, tk)
        dst[h] = _dot(k_ref[0, h, pl.ds(start, tk), :], qts[h])

    def values(h, kt, src, alpha):
        start = kt * tk if isinstance(kt, int) else pl.multiple_of(kt * tk, tk)
        vt = jnp.concatenate([vt_ref[0, h * B_V:(h + 1) * B_V, pl.ds(start, tk)], ones], axis=0)
        acc_ref[h] = alpha * acc_ref[h] + _dot(vt, src[h])

    def softmax(h, src, dst, m):
        st = src[h]
        m_new = jnp.maximum(m, jnp.max(st, axis=0, keepdims=True))
        dst[h] = jnp.exp2(st - m_new).astype(BF16)
        return m_new, jnp.exp2(m - m_new)

    for h in range(B_HEADS):
        scores(h, 0, sa)
        acc_ref[h] = jnp.zeros((B_V + BF16_SUBLANES, tq), F32)
    pb[...] = jnp.zeros_like(pb)

    def step(i, carry):
        out = []
        for h in range(B_HEADS):
            m, alpha_prev = carry[h]
            t0 = 2 * i
            scores(h, t0 + 1, sb)
            values(h, jnp.maximum(t0 - 1, 0), pb, alpha_prev)
            m, alpha0 = softmax(h, sa, pa, m)
            scores(h, jnp.minimum(t0 + 2, nk - 1), sa)
            values(h, t0, pa, alpha0)
            m, alpha1 = softmax(h, sb, pb, m)
            out.append((m, alpha1))
        return tuple(out)

    init = tuple((jnp.full((1, tq), NEG_INF, F32), jnp.ones((1, tq), F32)) for _ in range(B_HEADS))
    res = lax.fori_loop(0, nk // 2, step, init)
    outs = []
    for h in range(B_HEADS):
        values(h, nk - 1, pb, res[h][1])
        acc = acc_ref[h]
        outs.append(acc[:B_V] / acc[B_V:B_V + 1])
    o_ref[...] = jnp.concatenate(outs, axis=0).T.astype(BF16)


def _mla_call(qbt, kb, vbt, batch, seq):
    tq = min(TQ_MLA, seq)
    tk = min(TK_MLA, seq)
    nq = seq // tq
    assert (seq // tk) % 2 == 0
    return pl.pallas_call(
        functools.partial(_mla_body, seq=seq, tk=tk),
        grid=(batch, nq),
        in_specs=[
            pl.BlockSpec((1, B_HEADS * HEAD_PAD, tq), lambda b, qi: (b, 0, qi)),
            pl.BlockSpec((1, B_HEADS, seq, HEAD_PAD), lambda b, qi: (b, 0, 0, 0)),
            pl.BlockSpec((1, B_WIDTH, seq), lambda b, qi: (b, 0, 0)),
        ],
        out_specs=pl.BlockSpec((tq, B_WIDTH), lambda b, qi: (b * nq + qi, 0)),
        out_shape=jax.ShapeDtypeStruct((batch * seq, B_WIDTH), BF16),
        scratch_shapes=[pltpu.VMEM((B_HEADS, tk, tq), F32), pltpu.VMEM((B_HEADS, tk, tq), F32),
                        pltpu.VMEM((B_HEADS, tk, tq), BF16), pltpu.VMEM((B_HEADS, tk, tq), BF16),
                        pltpu.VMEM((B_HEADS, B_V + BF16_SUBLANES, tq), F32)],
        compiler_params=_cparams(("arbitrary", "arbitrary")), name="mla_attn",
    )(qbt, kb, vbt)


def _dft_body(c_ref, ns_ref, x_ref, xf_ref, xfp_ref, xh_ref, o_ref, acc_ref, *, seq):
    k = pl.program_id(1)
    tm = c_ref.shape[0]
    tk = x_ref.shape[1]

    @pl.when(k == 0)
    def _():
        row = lax.broadcasted_iota(jnp.int32, (tm, 1), 0)
        sign = (1 - 2 * (row & 1)).astype(F32) * (seq ** -0.5)
        for b in range(x_ref.shape[0]):
            acc_ref[b] = sign * xh_ref[b, 0:1, 0:C_WIDTH].astype(F32)

    c = c_ref[...]
    ns = ns_ref[...]
    first_row = lax.broadcasted_iota(jnp.int32, (tk, 1), 0) == 0
    has_prev = (k > 0).astype(F32)
    for b in range(x_ref.shape[0]):
        x = x_ref[b].astype(F32)
        carry_in = xfp_ref[b, 7:8, :].astype(F32) * has_prev
        xr = jnp.where(first_row, carry_in, pltpu.roll(xf_ref[b].astype(F32), 1, axis=0))
        sym = (x[:, 0:C_WIDTH] + xr[:, 0:C_WIDTH]).astype(BF16)
        anti = (x[:, C_WIDTH:2 * C_WIDTH] - xr[:, C_WIDTH:2 * C_WIDTH]).astype(BF16)
        acc_ref[b] += _dot(c, sym) + _dot(ns, anti)

    @pl.when(k == pl.num_programs(1) - 1)
    def _():
        o_ref[...] = acc_ref[...].astype(o_ref.dtype)


def _dft_call(cmat, nsmat, x3, x3_flip, batch, seq):
    half = seq // 2
    tm = min(TM_DFT, seq)
    tk = min(TK_DFT, half)
    rows8 = tk // 8
    blk = lambda rows, index: pl.BlockSpec((batch, rows, 2 * C_WIDTH), index)
    y = pl.pallas_call(
        functools.partial(_dft_body, seq=seq), grid=(seq // tm, half // tk),
        in_specs=[pl.BlockSpec((tm, tk), lambda i, k: (i, k)),
                  pl.BlockSpec((tm, tk), lambda i, k: (i, k)),
                  blk(tk, lambda i, k: (0, k, 0)),
                  blk(tk, lambda i, k: (0, k, 0)),
                  blk(8, lambda i, k: (0, jnp.maximum(k * rows8 - 1, 0), 0)),
                  blk(8, lambda i, k: (0, half // 8, 0))],
        out_specs=pl.BlockSpec((batch, tm, C_WIDTH), lambda i, k: (0, i, 0)),
        out_shape=jax.ShapeDtypeStruct((batch, seq, C_WIDTH), BF16),
        scratch_shapes=[pltpu.VMEM((batch, tm, C_WIDTH), F32)],
        compiler_params=_cparams(("arbitrary", "arbitrary")), name="seq_dft",
    )(cmat, nsmat, x3, x3_flip, x3_flip, x3)
    return y.reshape(batch * seq, C_WIDTH)


def _merge_residues(scr, ref, dil):
    tm = scr.shape[1]
    for r in range(dil):
        for c in range(A_WIDTH // LANES):
            col = r * A_WIDTH + c * LANES
            scr[c, pl.ds(r, tm // dil, stride=dil), :] = ref[0, :, col:col + LANES].astype(F32)
    return jnp.concatenate([scr[c] for c in range(A_WIDTH // LANES)], axis=1)


def _out_body(x_ref, o1_ref, o4_ref, o16_ref, l1_ref, l4_ref, l16_ref, ob_ref, oc_ref,
              ga_ref, gb_ref, gc_ref, wa_ref, wb_ref, wc_ref, y_ref, so4, sl4, so16, sl16):
    l1 = l1_ref[0]
    l4 = _merge_residues(sl4, l4_ref, DILATIONS[1])
    l16 = _merge_residues(sl16, l16_ref, DILATIONS[2])
    o4 = _merge_residues(so4, o4_ref, DILATIONS[1])
    o16 = _merge_residues(so16, o16_ref, DILATIONS[2])
    m = jnp.maximum(jnp.maximum(l1, l4), l16)
    e1, e4, e16 = jnp.exp(l1 - m), jnp.exp(l4 - m), jnp.exp(l16 - m)
    oa = (o1_ref[0].astype(F32) * e1 + o4 * e4 + o16 * e16) / (e1 + e4 + e16)
    na = _rms(oa, ga_ref[...]).astype(BF16)
    nb = _rms(ob_ref[...].astype(F32), gb_ref[...]).astype(BF16)
    nc = _rms(oc_ref[...].astype(F32), gc_ref[...]).astype(BF16)
    y_ref[...] = (x_ref[...] + _dot(na, wa_ref[...]) + _dot(nb, wb_ref[...]) + _dot(nc, wc_ref[...]))


def _out_call(x2, oa, la, ob, oc, ga, gb, gc, wa, wb, wc, batch, seq):
    t, d = x2.shape
    tm = min(TM_PROJ, seq)
    ns = seq // tm
    row = lambda i: (i, 0)
    const = lambda i: (0, 0)
    wide = lambda w: pl.BlockSpec((tm, w), row)
    view = lambda dil: pl.BlockSpec((1, tm // dil, dil * A_WIDTH), lambda i: (i // ns, i % ns, 0))
    views = [view(dil) for dil in DILATIONS]
    slab = pltpu.VMEM((A_WIDTH // LANES, tm, LANES), F32)
    return pl.pallas_call(
        _out_body, grid=(t // tm,),
        in_specs=[wide(d)] + views + views + [wide(B_WIDTH), wide(C_WIDTH),
                  pl.BlockSpec((1, A_WIDTH), const), pl.BlockSpec((1, B_WIDTH), const),
                  pl.BlockSpec((1, C_WIDTH), const),
                  pl.BlockSpec((A_WIDTH, d), const), pl.BlockSpec((B_WIDTH, d), const),
                  pl.BlockSpec((C_WIDTH, d), const)],
        out_specs=wide(d),
        out_shape=jax.ShapeDtypeStruct((t, d), F32),
        scratch_shapes=[slab, slab, slab, slab],
        compiler_params=_cparams(("arbitrary",)), name="out_proj",
    )(x2, oa[0], oa[1], oa[2], la[0], la[1], la[2], ob, oc, ga, gb, gc, wa, wb, wc)


def _ffn_body(x_ref, g_ref, wg_ref, wu_ref, wd_ref, y_ref, h_ref, acc_ref):
    f = pl.program_id(1)

    @pl.when(f == 0)
    def _():
        h_ref[...] = _rms(x_ref[...], g_ref[...]).astype(BF16)
        acc_ref[...] = jnp.zeros_like(acc_ref)

    h = h_ref[...]
    gate = _dot(h, wg_ref[...])
    up = _dot(h, wu_ref[...])
    acc_ref[...] += _dot((jax.nn.silu(gate) * up).astype(BF16), wd_ref[...])

    @pl.when(f == pl.num_programs(1) - 1)
    def _():
        y_ref[...] = x_ref[...] + acc_ref[...]


def _ffn_tile(d_ff):
    for steps in (1, 2, 4, 11, 22):
        if d_ff % steps == 0 and (d_ff // steps) % LANES == 0 and d_ff // steps <= 1536:
            return d_ff // steps
    return d_ff


def _ffn_call(x2, g, wg, wu, wd):
    t, d = x2.shape
    d_ff = wg.shape[1]
    tm = min(TM_FFN, t)
    tf = _ffn_tile(d_ff)
    return pl.pallas_call(
        _ffn_body, grid=(t // tm, d_ff // tf),
        in_specs=[pl.BlockSpec((tm, d), lambda i, f: (i, 0)),
                  pl.BlockSpec((1, d), lambda i, f: (0, 0)),
                  pl.BlockSpec((d, tf), lambda i, f: (0, f)),
                  pl.BlockSpec((d, tf), lambda i, f: (0, f)),
                  pl.BlockSpec((tf, d), lambda i, f: (f, 0))],
        out_specs=pl.BlockSpec((tm, d), lambda i, f: (i, 0)),
        out_shape=jax.ShapeDtypeStruct((t, d), F32),
        scratch_shapes=[pltpu.VMEM((tm, d), BF16), pltpu.VMEM((tm, d), F32)],
        compiler_params=_cparams(("arbitrary", "arbitrary")), name="dense_ffn",
    )(x2, g, wg, wu, wd)


def _to_token_tiles(ref, val):
    rows, d = val.shape
    parts = d // LANES
    for c in range(parts):
        ref[pl.ds(c, rows, stride=parts), :] = val[:, c * LANES:(c + 1) * LANES]


def _from_token_tiles(ref, rows):
    parts = ref.shape[0] // rows
    return jnp.concatenate([ref[pl.ds(c, rows, stride=parts), :] for c in range(parts)], axis=1)


def _router_body(x_ref, g_ref, wr_ref, idx_ref, gate_ref, htile_ref):
    h = _rms(x_ref[...], g_ref[...])
    _to_token_tiles(htile_ref, h)
    logits = lax.dot_general(wr_ref[...], h, (((1,), (1,)), ((), ())),
                             precision=lax.Precision.HIGHEST, preferred_element_type=F32)
    eid = lax.broadcasted_iota(jnp.int32, logits.shape, 0)
    m1 = jnp.max(logits, axis=0, keepdims=True)
    i1 = jnp.min(jnp.where(logits == m1, eid, N_EXPERTS), axis=0, keepdims=True)
    rest = jnp.where(eid == i1, -jnp.inf, logits)
    m2 = jnp.max(rest, axis=0, keepdims=True)
    i2 = jnp.min(jnp.where(rest == m2, eid, N_EXPERTS), axis=0, keepdims=True)
    e2 = jnp.exp(m2 - m1)
    idx_ref[...] = jnp.concatenate([i1, i2], axis=0)
    gate_ref[...] = jnp.concatenate([1.0 / (1.0 + e2), e2 / (1.0 + e2)], axis=0)


def _router_call(x2, g, wr_t):
    t, d = x2.shape
    tm = min(TM_ROUTER, t)
    return pl.pallas_call(
        _router_body, grid=(t // tm,),
        in_specs=[pl.BlockSpec((tm, d), lambda i: (i, 0)),
                  pl.BlockSpec((1, d), lambda i: (0, 0)),
                  pl.BlockSpec((N_EXPERTS, d), lambda i: (0, 0))],
        out_specs=(pl.BlockSpec((TOP_K, tm), lambda i: (0, i)),
                   pl.BlockSpec((TOP_K, tm), lambda i: (0, i)),
                   pl.BlockSpec((tm * d // LANES, LANES), lambda i: (i, 0))),
        out_shape=(jax.ShapeDtypeStruct((TOP_K, t), jnp.int32),
                   jax.ShapeDtypeStruct((TOP_K, t), F32),
                   jax.ShapeDtypeStruct((t * d // LANES, LANES), F32)),
        compiler_params=_cparams(("arbitrary",)), name="router",
    )(x2, g, wr_t)


def _expert_body(bexp_ref, tok_cur, tok_next, dst_prev, dst_cur, h_hbm, wg_ref, wu_ref, wd_ref,
                 y_hbm, xb0, xb1, hbuf, acc_ref, yb0, yb1, gsem, ssem):
    i = pl.program_id(0)
    f = pl.program_id(1)
    nb = pl.num_programs(0)
    nf = pl.num_programs(1)
    rows = hbuf.shape[0]
    parts = xb0.shape[0] // rows
    chunk = rows // EXP_STEPS

    def variant(xcur, xnxt, ycur, yprv, s, compute):
        def tile(ref, first_row):
            if not isinstance(first_row, int):
                first_row = pl.multiple_of(first_row, parts)
            return ref.at[pl.ds(first_row, parts)]

        def row_in(r, src, buf, sem):
            return pltpu.make_async_copy(tile(h_hbm, src), tile(buf, r * parts), sem)

        def row_out(r, dst, buf, sem):
            return pltpu.make_async_copy(tile(buf, r * parts), tile(y_hbm, dst), sem)

        def all_rows(fn):
            def body(r, c):
                fn(r)
                return c
            lax.fori_loop(0, rows, body, 0, unroll=8)

        @pl.when(f == 0)
        def _():
            @pl.when(i == 0)
            def _():
                all_rows(lambda r: row_in(r, tok_cur[0, 0, r], xcur, gsem.at[s]).start())
                yprv[...] = jnp.zeros_like(yprv)

            all_rows(lambda r: row_in(r, 0, xcur, gsem.at[s]).wait())
            if compute:
                hbuf[...] = _from_token_tiles(xcur, rows).astype(BF16)
                acc_ref[...] = jnp.zeros_like(acc_ref)

        base = pl.multiple_of(f * chunk, chunk)
        for k in range(chunk):
            r = base + k
            row_in(r, tok_next[0, 0, r], xnxt, gsem.at[1 - s]).start()
            row_out(r, dst_prev[0, 0, r], yprv, ssem.at[1 - s]).start()
        if compute:
            h = hbuf[...]
            gate = _dot(h, wg_ref[0])
            up = _dot(h, wu_ref[0])
            acc_ref[...] += _dot((jax.nn.silu(gate) * up).astype(BF16), wd_ref[0])

        @pl.when(f == nf - 1)
        def _():
            if compute:
                _to_token_tiles(ycur, acc_ref[...])
            else:
                ycur[...] = jnp.zeros_like(ycur)
            all_rows(lambda r: row_out(r, 0, yprv, ssem.at[1 - s]).wait())

            @pl.when(i == nb - 1)
            def _():
                all_rows(lambda r: row_out(r, dst_cur[0, 0, r], ycur, ssem.at[s]).start())
                all_rows(lambda r: row_out(r, 0, ycur, ssem.at[s]).wait())
                all_rows(lambda r: row_in(r, 0, xnxt, gsem.at[1 - s]).wait())

    even = i % 2 == 0
    padding_only = bexp_ref[i] >= N_EXPERTS
    for is_even, bufs in ((True, (xb0, xb1, yb0, yb1, 0)), (False, (xb1, xb0, yb1, yb0, 1))):
        for compute in (True, False):
            @pl.when((even == is_even) & (padding_only != compute))
            def _(bufs=bufs, compute=compute):
                variant(*bufs, compute)


def _expert_call(h_tiles, d, wg, wu, wd, bexp, row_tok, row_dst, n_out_rows):
    parts = d // LANES
    n_blocks = bexp.shape[0]
    d_ff = wg.shape[2]
    assert d_ff % EXP_STEPS == 0 and (d_ff // EXP_STEPS) % LANES == 0
    tf = d_ff // EXP_STEPS
    last = n_blocks - 1
    rows_of = lambda which: pl.BlockSpec((1, 1, TM_EXP), which, memory_space=pltpu.SMEM)
    tiles = pltpu.VMEM((TM_EXP * parts, LANES), F32)
    grid_spec = pltpu.PrefetchScalarGridSpec(
        num_scalar_prefetch=1,
        grid=(n_blocks, EXP_STEPS),
        in_specs=[
            rows_of(lambda i, f, be: (i, 0, 0)),
            rows_of(lambda i, f, be: (jnp.minimum(i + 1, last), 0, 0)),
            rows_of(lambda i, f, be: (jnp.maximum(i - 1, 0), 0, 0)),
            rows_of(lambda i, f, be: (i, 0, 0)),
            pl.BlockSpec(memory_space=pl.ANY),
            pl.BlockSpec((1, d, tf), lambda i, f, be: (be[i] % N_EXPERTS, 0, f)),
            pl.BlockSpec((1, d, tf), lambda i, f, be: (be[i] % N_EXPERTS, 0, f)),
            pl.BlockSpec((1, tf, d), lambda i, f, be: (be[i] % N_EXPERTS, f, 0)),
        ],
        out_specs=pl.BlockSpec(memory_space=pl.ANY),
        scratch_shapes=[tiles, tiles, pltpu.VMEM((TM_EXP, d), BF16), pltpu.VMEM((TM_EXP, d), F32),
                        tiles, tiles, pltpu.SemaphoreType.DMA((2,)), pltpu.SemaphoreType.DMA((2,))],
    )
    return pl.pallas_call(
        _expert_body, grid_spec=grid_spec,
        out_shape=jax.ShapeDtypeStruct((n_out_rows * parts, LANES), F32),
        compiler_params=_cparams(("arbitrary", "arbitrary")), name="experts",
    )(bexp, row_tok, row_tok, row_dst, row_dst, h_tiles, wg, wu, wd)


def _combine_body(x_ref, y0_ref, y1_ref, gate_ref, gf_ref, o_ref, *, final):
    g0 = gate_ref[:, 0:1]
    g1 = gate_ref[:, 1:2]
    rows = x_ref.shape[0]
    y = x_ref[...] + (_from_token_tiles(y0_ref, rows) * g0 + _from_token_tiles(y1_ref, rows) * g1)
    o_ref[...] = _rms(y, gf_ref[...]) if final else y


def _combine_call(x2, y_tiles, gates_t, gfinal, final):
    t, d = x2.shape
    tm = min(TM_ROUTER, t)
    nt = t // tm
    parts = d // LANES
    return pl.pallas_call(
        functools.partial(_combine_body, final=final), grid=(nt,),
        in_specs=[pl.BlockSpec((tm, d), lambda i: (i, 0)),
                  pl.BlockSpec((tm * parts, LANES), lambda i: (i, 0)),
                  pl.BlockSpec((tm * parts, LANES), lambda i: (i + nt, 0)),
                  pl.BlockSpec((tm, TOP_K), lambda i: (i, 0)),
                  pl.BlockSpec((1, d), lambda i: (0, 0))],
        out_specs=pl.BlockSpec((tm, d), lambda i: (i, 0)),
        out_shape=jax.ShapeDtypeStruct((t, d), F32),
        compiler_params=_cparams(("arbitrary",)), name="moe_combine",
    )(x2, y_tiles, y_tiles, gates_t, gfinal)


def _final_norm_body(x_ref, g_ref, o_ref):
    o_ref[...] = _rms(x_ref[...], g_ref[...])


def _final_norm_call(x2, g):
    t, d = x2.shape
    tm = min(TM_ROUTER, t)
    return pl.pallas_call(
        _final_norm_body, grid=(t // tm,),
        in_specs=[pl.BlockSpec((tm, d), lambda i: (i, 0)), pl.BlockSpec((1, d), lambda i: (0, 0))],
        out_specs=pl.BlockSpec((tm, d), lambda i: (i, 0)),
        out_shape=jax.ShapeDtypeStruct((t, d), F32),
        compiler_params=_cparams(("arbitrary",)), name="final_norm",
    )(x2, g)


def _route_plan(idx, t):
    n_assign = TOP_K * t
    flat_e = idx.reshape(-1)
    order = jnp.argsort(flat_e).astype(jnp.int32)
    counts = jnp.sum(flat_e[:, None] == jnp.arange(N_EXPERTS, dtype=jnp.int32)[None, :], axis=0,
                     dtype=jnp.int32)
    padded = ((counts + TM_EXP - 1) // TM_EXP) * TM_EXP
    start = jnp.cumsum(counts) - counts
    cum_padded = jnp.cumsum(padded)
    start_padded = cum_padded - padded
    n_blocks = -(-n_assign // TM_EXP) + N_EXPERTS
    block_start = jnp.arange(n_blocks, dtype=jnp.int32) * TM_EXP
    bexp = jnp.minimum(jnp.sum(block_start[:, None] >= cum_padded[None, :], axis=1),
                       N_EXPERTS - 1).astype(jnp.int32)
    off = block_start[:, None] + jnp.arange(TM_EXP, dtype=jnp.int32)[None, :] - start_padded[bexp][:, None]
    valid = off < counts[bexp][:, None]
    a = order[jnp.clip(start[bexp][:, None] + off, 0, n_assign - 1)]
    bexp = bexp + N_EXPERTS * (~jnp.any(valid, axis=1)).astype(jnp.int32)
    pad_rank = (jnp.cumsum((~valid).reshape(-1).astype(jnp.int32)) - 1).reshape(valid.shape)
    row_tok = jnp.where(valid, a % t, 0).astype(jnp.int32)
    row_dst = jnp.where(valid, a, n_assign + pad_rank).astype(jnp.int32)
    n_out_rows = n_blocks * TM_EXP
    return bexp, row_tok.reshape(n_blocks, 1, TM_EXP), row_dst.reshape(n_blocks, 1, TM_EXP), n_out_rows


def _rope_tables(seq):
    pos = jnp.arange(seq, dtype=F32)[:, None]
    half_a = A_ROT_DIM // 2
    ang_a = pos * (ROPE_THETA ** (-jnp.arange(half_a, dtype=F32) / half_a))[None, :]
    one = jnp.ones((seq, HEAD_DIM - A_ROT_DIM), F32)
    cos_head = jnp.concatenate([jnp.cos(ang_a), jnp.cos(ang_a), one], axis=1)
    sin_head = jnp.concatenate([jnp.sin(ang_a), jnp.sin(ang_a), 0 * one], axis=1)
    cosa = jnp.tile(cos_head, (1, A_HEADS))
    sina = jnp.tile(sin_head, (1, A_HEADS))
    half_b = B_ROPE // 2
    ang_b = pos * (ROPE_THETA ** (-jnp.arange(half_b, dtype=F32) / half_b))[None, :]
    zl = jnp.zeros((seq, B_NOPE), F32)
    zr = jnp.zeros((seq, HEAD_PAD - B_NOPE - B_ROPE), F32)
    cosk = jnp.concatenate([zl, jnp.cos(ang_b), jnp.cos(ang_b), zr], axis=1)
    sink = jnp.concatenate([zl, jnp.sin(ang_b), jnp.sin(ang_b), zr], axis=1)
    return cosa, sina, cosk, sink, jnp.cos(ang_b).T, jnp.sin(ang_b).T


def _rotate_half_matrix():
    r = np.zeros((A_WIDTH, A_WIDTH), np.float32)
    half = A_ROT_DIM // 2
    for hd in range(A_HEADS):
        for i in range(half):
            r[hd * HEAD_DIM + half + i, hd * HEAD_DIM + i] = -1.0
            r[hd * HEAD_DIM + i, hd * HEAD_DIM + half + i] = 1.0
    return jnp.asarray(r, BF16)


def _dft_constants(seq):
    lo_n = 64
    hi_n = seq // lo_n
    k = jnp.arange(seq, dtype=jnp.int32)[:, None]
    ang_hi = ((k * jnp.arange(hi_n // 2, dtype=jnp.int32)[None, :]) % hi_n).astype(F32) * (2.0 * np.pi / hi_n)
    ang_lo = ((k * jnp.arange(lo_n, dtype=jnp.int32)[None, :]) % seq).astype(F32) * (2.0 * np.pi / seq)
    ch, sh = jnp.cos(ang_hi)[:, :, None], jnp.sin(ang_hi)[:, :, None]
    cl, sl = jnp.cos(ang_lo)[:, None, :], jnp.sin(ang_lo)[:, None, :]
    scale = seq ** -0.5
    cmat = ((ch * cl - sh * sl) * scale).astype(BF16).reshape(seq, seq // 2)
    nsmat = ((sh * cl + ch * sl) * -scale).astype(BF16).reshape(seq, seq // 2)
    c = np.arange(C_GROUP_DIM)
    angc = 2.0 * np.pi * ((c[:, None] * c[None, :]) % C_GROUP_DIM) / C_GROUP_DIM
    eye = np.eye(C_GROUPS)
    cs = np.concatenate([np.kron(eye, np.cos(angc)), np.kron(eye, np.sin(angc))], axis=1)
    return cmat, nsmat, jnp.asarray(cs * C_GROUP_DIM ** -0.5, BF16)


def _layer_weights(w_in, w_uq, w_ukv, w_out):
    d = w_in.shape[0]
    c_kpe = 3 * A_WIDTH + Q_LORA + KV_LORA
    w_kpe = w_in[:, c_kpe:c_kpe + B_ROPE]
    half = B_ROPE // 2
    w_kpe_rot = jnp.concatenate([-w_kpe[:, half:], w_kpe[:, :half]], axis=1)
    zl = jnp.zeros((d, B_NOPE), w_in.dtype)
    zr = jnp.zeros((d, HEAD_PAD - B_NOPE - B_ROPE), w_in.dtype)
    wm = jnp.concatenate([w_in[:, :c_kpe], zl, w_kpe, zr, zl, w_kpe_rot, zr,
                          w_in[:, c_kpe + B_ROPE:]], axis=1).astype(BF16)
    uq = w_uq.reshape(Q_LORA, B_HEADS, B_NOPE + B_ROPE)
    uq = jnp.pad(uq, ((0, 0), (0, 0), (0, HEAD_PAD - B_NOPE - B_ROPE)))
    wuq_t = uq.reshape(Q_LORA, B_HEADS * HEAD_PAD).T.astype(BF16)
    ukv = w_ukv.reshape(KV_LORA, B_HEADS, B_NOPE + B_V)
    wkn = jnp.pad(ukv[:, :, :B_NOPE], ((0, 0), (0, 0), (0, HEAD_PAD - B_NOPE)))
    wkn = wkn.reshape(KV_LORA, B_HEADS * HEAD_PAD).astype(BF16)
    wv_t = ukv[:, :, B_NOPE:].reshape(KV_LORA, B_WIDTH).T.astype(BF16)
    wo = w_out.astype(BF16)
    return wm, wuq_t, wkn, wv_t, wo[:A_WIDTH], wo[A_WIDTH:A_WIDTH + B_WIDTH], wo[A_WIDTH + B_WIDTH:]


def kernel(x, attn_norm, w_in, q_norm, w_uq, kv_norm, w_ukv, mix_gain, w_out, ffn_norm,
           w_ffn_gate, w_ffn_up, w_ffn_down, w_router, w_exp_gate, w_exp_up, w_exp_down,
           final_norm):
    batch, seq, d = x.shape
    depth = w_in.shape[0]
    t = batch * seq
    assert seq % (max(DILATIONS) * 2 * WIN_SUB) == 0
    tabs = _rope_tables(seq)
    rot = _rotate_half_matrix()
    cmat, nsmat, cs = _dft_constants(seq)
    row = lambda v: v.reshape(1, -1)

    x2 = x.reshape(t, d)
    for l in range(depth):
        wm, wuq_t, wkn, wv_t, wa, wb, wc = _layer_weights(w_in[l], w_uq[l], w_ukv[l], w_out[l])
        qkv_views, (qbt, kb, vbt, xcs, xcs_flip) = _proj_call(
            x2, row(attn_norm[l]), wm, rot, tabs, row(q_norm[l]), wuq_t, row(kv_norm[l]), wkn, wv_t,
            cs, batch, seq)
        win = [_win_call(*(a.reshape(batch, seq // dil, dil * A_WIDTH) for a in qkv), batch, seq, dil)
               for dil, qkv in zip(DILATIONS, qkv_views)]
        ob = _mla_call(qbt, kb, vbt, batch, seq)
        oc = _dft_call(cmat, nsmat, xcs, xcs_flip, batch, seq)
        g = mix_gain[l]
        x2 = _out_call(x2, [w[0] for w in win], [w[1] for w in win], ob, oc,
                       row(g[:A_WIDTH]), row(g[A_WIDTH:A_WIDTH + B_WIDTH]), row(g[A_WIDTH + B_WIDTH:]),
                       wa, wb, wc, batch, seq)
        i = l // 2
        if l % 2 == 0:
            x2 = _ffn_call(x2, row(ffn_norm[l]), w_ffn_gate[i].astype(BF16), w_ffn_up[i].astype(BF16),
                           w_ffn_down[i].astype(BF16))
            if l == depth - 1:
                x2 = _final_norm_call(x2, row(final_norm))
        else:
            idx, gates, h_tiles = _router_call(x2, row(ffn_norm[l]), w_router[i].T)
            bexp, row_tok, row_dst, n_out_rows = _route_plan(idx, t)
            parts = d // LANES
            y_tiles = _expert_call(h_tiles, d, w_exp_gate[i].astype(BF16), w_exp_up[i].astype(BF16),
                                   w_exp_down[i].astype(BF16), bexp, row_tok * parts, row_dst * parts,
                                   n_out_rows)
            x2 = _combine_call(x2, y_tiles, gates.T, row(final_norm), final=(l == depth - 1))
    return x2.reshape(batch, seq, d)
```

```python
import functools
import math

import jax
import jax.numpy as jnp
import numpy as np
from jax import lax
from jax.experimental import pallas as pl
from jax.experimental.pallas import tpu as pltpu

F32 = jnp.float32
BF16 = jnp.bfloat16

HEAD_DIM = 64
A_HEADS = 6
A_WIDTH = A_HEADS * HEAD_DIM
A_ROT_DIM = HEAD_DIM // 4
DILATIONS = (1, 4, 16)
WINDOW_RADIUS = 64
B_HEADS = 6
B_NOPE = 64
B_ROPE = 32
B_V = 64
B_WIDTH = B_HEADS * B_V
Q_LORA = 384
KV_LORA = 128
C_GROUPS = 4
C_GROUP_DIM = 64
C_WIDTH = C_GROUPS * C_GROUP_DIM
ROPE_THETA = 500000.0
N_EXPERTS = 8
TOP_K = 2
RMS_EPS = 1e-6
NEG_INF = -1e30

LANES = 128
BF16_SUBLANES = 16
HEAD_PAD = 128
VMEM_LIMIT_BYTES = 56 * 1024 * 1024

TM_PROJ = 512
TQ_WIN = 512
WIN_SUB = 128
TQ_MLA = 512
TK_MLA = 256
TM_DFT = 1024
TK_DFT = 512
TM_FFN = 512
TM_ROUTER = 512
EXP_STEPS = 7
TM_EXP = 128 * EXP_STEPS


def _cparams(sem):
    return pltpu.CompilerParams(dimension_semantics=sem, vmem_limit_bytes=VMEM_LIMIT_BYTES)


def _rms(x, g):
    return x * lax.rsqrt(jnp.mean(x * x, axis=-1, keepdims=True) + RMS_EPS) * g


def _dot(a, b):
    return jnp.dot(a, b, preferred_element_type=F32)


def _dot_nt(a, b):
    return lax.dot_general(a, b, (((1,), (1,)), ((), ())), preferred_element_type=F32)


def _split_residues(scr, val, nat_ref, view_refs):
    tm = val.shape[0]
    nat_ref[...] = val.astype(BF16)
    for c in range(A_WIDTH // LANES):
        scr[c] = val[:, c * LANES:(c + 1) * LANES]
    for dil, ref in zip(DILATIONS[1:], view_refs):
        for r in range(dil):
            for c in range(A_WIDTH // LANES):
                col = r * A_WIDTH + c * LANES
                ref[0, :, col:col + LANES] = scr[c, pl.ds(r, tm // dil, stride=dil), :].astype(BF16)


def _proj_body(x_ref, g_ref, wm_ref, rot_ref, cosa_ref, sina_ref, cosk_ref, sink_ref,
               cosq_ref, sinq_ref, qn_ref, wuq_ref, kvn_ref, wkn_ref, wv_ref, cs_ref, flip_ref,
               qa_ref, qa4_ref, qa16_ref, ka_ref, ka4_ref, ka16_ref, va_ref, va4_ref, va16_ref,
               qbt_ref, kb_ref, vbt_ref, xcs_ref, xcs_flip_ref, scr):
    h = _rms(x_ref[...], g_ref[...]).astype(BF16)
    z = _dot(h, wm_ref[...])
    o = 0
    qa = z[:, o:o + A_WIDTH]; o += A_WIDTH
    ka = z[:, o:o + A_WIDTH]; o += A_WIDTH
    va = z[:, o:o + A_WIDTH]; o += A_WIDTH
    cq = z[:, o:o + Q_LORA]; o += Q_LORA
    ckv = z[:, o:o + KV_LORA]; o += KV_LORA
    kpe = z[:, o:o + HEAD_PAD]; o += HEAD_PAD
    kpe_rot = z[:, o:o + HEAD_PAD]; o += HEAD_PAD
    fc = z[:, o:o + C_WIDTH]

    cosa = cosa_ref[...]
    sina = sina_ref[...]
    rot = rot_ref[...]
    _split_residues(scr, qa * cosa + _dot(qa.astype(BF16), rot) * sina, qa_ref, (qa4_ref, qa16_ref))
    _split_residues(scr, ka * cosa + _dot(ka.astype(BF16), rot) * sina, ka_ref, (ka4_ref, ka16_ref))
    _split_residues(scr, va, va_ref, (va4_ref, va16_ref))

    cqn = _rms(cq, qn_ref[...]).astype(BF16)
    qbt = _dot_nt(wuq_ref[...], cqn) * ((B_NOPE + B_ROPE) ** -0.5 * math.log2(math.e))
    cosq = cosq_ref[...]
    sinq = sinq_ref[...]
    half = B_ROPE // 2
    for hd in range(B_HEADS):
        r0 = hd * HEAD_PAD
        x1 = qbt[r0 + B_NOPE:r0 + B_NOPE + half]
        x2 = qbt[r0 + B_NOPE + half:r0 + B_NOPE + B_ROPE]
        qbt_ref[0, r0:r0 + B_NOPE, :] = qbt[r0:r0 + B_NOPE].astype(BF16)
        qbt_ref[0, r0 + B_NOPE:r0 + B_NOPE + half, :] = (x1 * cosq - x2 * sinq).astype(BF16)
        qbt_ref[0, r0 + B_NOPE + half:r0 + B_NOPE + B_ROPE, :] = (x2 * cosq + x1 * sinq).astype(BF16)
        qbt_ref[0, r0 + B_NOPE + B_ROPE:r0 + HEAD_PAD, :] = jnp.zeros(
            (HEAD_PAD - B_NOPE - B_ROPE, qbt.shape[1]), BF16)

    ckvn = _rms(ckv, kvn_ref[...]).astype(BF16)
    kn = _dot(ckvn, wkn_ref[...])
    kpe_r = kpe * cosk_ref[...] + kpe_rot * sink_ref[...]
    for hd in range(B_HEADS):
        kb_ref[0, hd] = (kn[:, hd * HEAD_PAD:(hd + 1) * HEAD_PAD] + kpe_r).astype(BF16)
    vbt_ref[0] = _dot_nt(wv_ref[...], ckvn).astype(BF16)

    xcs = _dot(fc.astype(BF16), cs_ref[...]).astype(BF16)
    xcs_ref[0] = xcs
    xcs_flip_ref[0] = _dot(flip_ref[...], xcs).astype(BF16)


def _proj_call(x2, g, wm, rot, tabs, qn, wuq, kvn, wkn, wv, cs, batch, seq):
    t, d = x2.shape
    tm = min(TM_PROJ, seq)
    ns = seq // tm
    nm = wm.shape[1]
    cosa, sina, cosk, sink, cosq, sinq = tabs
    row = lambda i: (i, 0)
    pos = lambda i: (i % ns, 0)
    post = lambda i: (0, i % ns)
    const = lambda i: (0, 0)
    in_specs = [
        pl.BlockSpec((tm, d), row),
        pl.BlockSpec((1, d), const),
        pl.BlockSpec((d, nm), const),
        pl.BlockSpec((A_WIDTH, A_WIDTH), const),
        pl.BlockSpec((tm, A_WIDTH), pos),
        pl.BlockSpec((tm, A_WIDTH), pos),
        pl.BlockSpec((tm, HEAD_PAD), pos),
        pl.BlockSpec((tm, HEAD_PAD), pos),
        pl.BlockSpec((B_ROPE // 2, tm), post),
        pl.BlockSpec((B_ROPE // 2, tm), post),
        pl.BlockSpec((1, Q_LORA), const),
        pl.BlockSpec((B_HEADS * HEAD_PAD, Q_LORA), const),
        pl.BlockSpec((1, KV_LORA), const),
        pl.BlockSpec((KV_LORA, B_HEADS * HEAD_PAD), const),
        pl.BlockSpec((B_WIDTH, KV_LORA), const),
        pl.BlockSpec((C_WIDTH, 2 * C_WIDTH), const),
        pl.BlockSpec((tm, tm), const),
    ]
    bpos = lambda i: (i // ns, i % ns, 0)
    a_shapes, a_specs = [], []
    for _ in range(3):
        a_shapes.append(jax.ShapeDtypeStruct((t, A_WIDTH), BF16))
        a_specs.append(pl.BlockSpec((tm, A_WIDTH), row))
        for dil in DILATIONS[1:]:
            a_shapes.append(jax.ShapeDtypeStruct((batch, seq // dil, dil * A_WIDTH), BF16))
            a_specs.append(pl.BlockSpec((1, tm // dil, dil * A_WIDTH), bpos))
    out_shape = (
        *a_shapes,
        jax.ShapeDtypeStruct((batch, B_HEADS * HEAD_PAD, seq), BF16),
        jax.ShapeDtypeStruct((batch, B_HEADS, seq, HEAD_PAD), BF16),
        jax.ShapeDtypeStruct((batch, B_WIDTH, seq), BF16),
        jax.ShapeDtypeStruct((batch, seq, 2 * C_WIDTH), BF16),
        jax.ShapeDtypeStruct((batch, seq, 2 * C_WIDTH), BF16),
    )
    out_specs = (
        *a_specs,
        pl.BlockSpec((1, B_HEADS * HEAD_PAD, tm), lambda i: (i // ns, 0, i % ns)),
        pl.BlockSpec((1, B_HEADS, tm, HEAD_PAD), lambda i: (i // ns, 0, i % ns, 0)),
        pl.BlockSpec((1, B_WIDTH, tm), lambda i: (i // ns, 0, i % ns)),
        pl.BlockSpec((1, tm, 2 * C_WIDTH), bpos),
        pl.BlockSpec((1, tm, 2 * C_WIDTH), lambda i: (i // ns, ns - 1 - i % ns, 0)),
    )
    flip = jnp.asarray(np.eye(tm, dtype=np.float32)[::-1], BF16)
    outs = pl.pallas_call(
        _proj_body, grid=(t // tm,), in_specs=in_specs, out_specs=out_specs, out_shape=out_shape,
        scratch_shapes=[pltpu.VMEM((A_WIDTH // LANES, tm, LANES), F32)],
        compiler_params=_cparams(("arbitrary",)), name="proj",
    )(x2, g, wm, rot, cosa, sina, cosk, sink, cosq, sinq, qn, wuq, kvn, wkn, wv, cs, flip)
    nd = len(DILATIONS)
    qkv_views = [tuple(outs[a * nd + j] for a in range(3)) for j in range(nd)]
    return qkv_views, outs[3 * nd:]


def _win_body(q_ref, kp_ref, km_ref, kn_ref, vp_ref, vm_ref, vn_ref, o_ref, l_ref, *, seg_len, tq):
    j = pl.program_id(2)
    span = WIN_SUB + 2 * WINDOW_RADIUS
    lane = lax.broadcasted_iota(jnp.int32, (1, LANES), 1)
    lo_half = lane < HEAD_DIM
    rel = (lax.broadcasted_iota(jnp.int32, (WIN_SUB, span), 1) - WINDOW_RADIUS
           - lax.broadcasted_iota(jnp.int32, (WIN_SUB, span), 0))
    band = jnp.abs(rel) <= WINDOW_RADIUS
    kcol = lax.broadcasted_iota(jnp.int32, (1, span), 1)
    n_sub = tq // WIN_SUB
    biases = []
    for sb in range(n_sub):
        kpos = j * tq + sb * WIN_SUB - WINDOW_RADIUS + kcol
        biases.append(jnp.where(band & (kpos >= 0) & (kpos < seg_len), 0.0, NEG_INF))
    for pair in range(A_HEADS // 2):
        cs = slice(pair * LANES, (pair + 1) * LANES)
        q2 = q_ref[0, :, cs] * (HEAD_DIM ** -0.5 * math.log2(math.e))
        qh = (jnp.where(lo_half, q2, 0).astype(BF16), jnp.where(lo_half, 0, q2).astype(BF16))
        kcat = jnp.concatenate([kp_ref[0, :, cs], km_ref[0, :, cs], kn_ref[0, :, cs]], axis=0)
        vcat = jnp.concatenate([vp_ref[0, :, cs], vm_ref[0, :, cs], vn_ref[0, :, cs]], axis=0)
        for sb in range(n_sub):
            r0 = sb * WIN_SUB
            ks = kcat[r0:r0 + span]
            vs = vcat[r0:r0 + span]
            outs, lses = [], []
            for hh in range(2):
                s = _dot_nt(qh[hh][r0:r0 + WIN_SUB], ks) + biases[sb]
                m = jnp.max(s, axis=-1, keepdims=True)
                p = jnp.exp2(s - m)
                den = jnp.sum(p, axis=-1, keepdims=True)
                outs.append(_dot(p.astype(BF16), vs) / den)
                lses.append(m * math.log(2.0) + jnp.log(den))
            o_ref[0, r0:r0 + WIN_SUB, cs] = jnp.where(lo_half, outs[0], outs[1]).astype(BF16)
            l_ref[0, r0:r0 + WIN_SUB, cs] = jnp.where(lo_half, lses[0], lses[1])


def _win_call(q, k, v, batch, seq, dil):
    seg = seq // dil
    tq = min(TQ_WIN, seg)
    nb = tq // WINDOW_RADIUS
    last = seg // WINDOW_RADIUS - 1
    main = pl.BlockSpec((1, tq, A_WIDTH), lambda b, r, j: (b, j, r))
    prev = pl.BlockSpec((1, WINDOW_RADIUS, A_WIDTH), lambda b, r, j: (b, jnp.maximum(j * nb - 1, 0), r))
    nxt = pl.BlockSpec((1, WINDOW_RADIUS, A_WIDTH), lambda b, r, j: (b, jnp.minimum((j + 1) * nb, last), r))
    return pl.pallas_call(
        functools.partial(_win_body, seg_len=seg, tq=tq),
        grid=(batch, dil, seg // tq),
        in_specs=[main, prev, main, nxt, prev, main, nxt],
        out_specs=(main, main),
        out_shape=(jax.ShapeDtypeStruct((batch, seg, dil * A_WIDTH), BF16),
                   jax.ShapeDtypeStruct((batch, seg, dil * A_WIDTH), F32)),
        compiler_params=_cparams(("arbitrary", "arbitrary", "arbitrary")), name=f"win_d{dil}",
    )(q, k, k, k, v, v, v)


def _mla_body(qt_ref, k_ref, vt_ref, o_ref, sa, sb, pa, pb, acc_ref, *, seq, tk):
    tq = qt_ref.shape[2]
    nk = seq // tk
    qts = [qt_ref[0, h * HEAD_PAD:(h + 1) * HEAD_PAD, :] for h in range(B_HEADS)]
    ones = jnp.ones((BF16_SUBLANES, tk), BF16)

    def scores(h, kt, dst):
        start = kt * tk if isinstance(kt, int) else pl.multiple_of(kt * tk, tk)
        dst[h] = _dot(k_ref[0, h, pl.ds(start, tk), :], qts[h])

    def values(h, kt, src, alpha):
        start = kt * tk if isinstance(kt, int) else pl.multiple_of(kt * tk, tk)
        vt = jnp.concatenate([vt_ref[0, h * B_V:(h + 1) * B_V, pl.ds(start, tk)], ones], axis=0)
        acc_ref[h] = alpha * acc_ref[h] + _dot(vt, src[h])

    def softmax(h, src, dst, m):
        st = src[h]
        m_new = jnp.maximum(m, jnp.max(st, axis=0, keepdims=True))
        dst[h] = jnp.exp2(st - m_new).astype(BF16)
        return m_new, jnp.exp2(m - m_new)

    for h in range(B_HEADS):
        scores(h, 0, sa)
        acc_ref[h] = jnp.zeros((B_V + BF16_SUBLANES, tq), F32)
    pb[...] = jnp.zeros_like(pb)

    def step(i, carry):
        out = []
        for h in range(B_HEADS):
            m, alpha_prev = carry[h]
            t0 = 2 * i
            scores(h, t0 + 1, sb)
            values(h, jnp.maximum(t0 - 1, 0), pb, alpha_prev)
            m, alpha0 = softmax(h, sa, pa, m)
            scores(h, jnp.minimum(t0 + 2, nk - 1), sa)
            values(h, t0, pa, alpha0)
            m, alpha1 = softmax(h, sb, pb, m)
            out.append((m, alpha1))
        return tuple(out)

    init = tuple((jnp.full((1, tq), NEG_INF, F32), jnp.ones((1, tq), F32)) for _ in range(B_HEADS))
    res = lax.fori_loop(0, nk // 2, step, init)
    outs = []
    for h in range(B_HEADS):
        values(h, nk - 1, pb, res[h][1])
        acc = acc_ref[h]
        outs.append(acc[:B_V] / acc[B_V:B_V + 1])
    o_ref[...] = jnp.concatenate(outs, axis=0).T.astype(BF16)


def _mla_call(qbt, kb, vbt, batch, seq):
    tq = min(TQ_MLA, seq)
    tk = min(TK_MLA, seq)
    nq = seq // tq
    assert (seq // tk) % 2 == 0
    return pl.pallas_call(
        functools.partial(_mla_body, seq=seq, tk=tk),
        grid=(batch, nq),
        in_specs=[
            pl.BlockSpec((1, B_HEADS * HEAD_PAD, tq), lambda b, qi: (b, 0, qi)),
            pl.BlockSpec((1, B_HEADS, seq, HEAD_PAD), lambda b, qi: (b, 0, 0, 0)),
            pl.BlockSpec((1, B_WIDTH, seq), lambda b, qi: (b, 0, 0)),
        ],
        out_specs=pl.BlockSpec((tq, B_WIDTH), lambda b, qi: (b * nq + qi, 0)),
        out_shape=jax.ShapeDtypeStruct((batch * seq, B_WIDTH), BF16),
        scratch_shapes=[pltpu.VMEM((B_HEADS, tk, tq), F32), pltpu.VMEM((B_HEADS, tk, tq), F32),
                        pltpu.VMEM((B_HEADS, tk, tq), BF16), pltpu.VMEM((B_HEADS, tk, tq), BF16),
                        pltpu.VMEM((B_HEADS, B_V + BF16_SUBLANES, tq), F32)],
        compiler_params=_cparams(("arbitrary", "arbitrary")), name="mla_attn",
    )(qbt, kb, vbt)


def _dft_body(c_ref, ns_ref, x_ref, xf_ref, xfp_ref, xh_ref, o_ref, acc_ref, *, seq):
    k = pl.program_id(1)
    tm = c_ref.shape[0]
    tk = x_ref.shape[1]

    @pl.when(k == 0)
    def _():
        row = lax.broadcasted_iota(jnp.int32, (tm, 1), 0)
        sign = (1 - 2 * (row & 1)).astype(F32) * (seq ** -0.5)
        for b in range(x_ref.shape[0]):
            acc_ref[b] = sign * xh_ref[b, 0:1, 0:C_WIDTH].astype(F32)

    c = c_ref[...]
    ns = ns_ref[...]
    first_row = lax.broadcasted_iota(jnp.int32, (tk, 1), 0) == 0
    has_prev = (k > 0).astype(F32)
    for b in range(x_ref.shape[0]):
        x = x_ref[b].astype(F32)
        carry_in = xfp_ref[b, 7:8, :].astype(F32) * has_prev
        xr = jnp.where(first_row, carry_in, pltpu.roll(xf_ref[b].astype(F32), 1, axis=0))
        sym = (x[:, 0:C_WIDTH] + xr[:, 0:C_WIDTH]).astype(BF16)
        anti = (x[:, C_WIDTH:2 * C_WIDTH] - xr[:, C_WIDTH:2 * C_WIDTH]).astype(BF16)
        acc_ref[b] += _dot(c, sym) + _dot(ns, anti)

    @pl.when(k == pl.num_programs(1) - 1)
    def _():
        o_ref[...] = acc_ref[...].astype(o_ref.dtype)


def _dft_call(cmat, nsmat, x3, x3_flip, batch, seq):
    half = seq // 2
    tm = min(TM_DFT, seq)
    tk = min(TK_DFT, half)
    rows8 = tk // 8
    blk = lambda rows, index: pl.BlockSpec((batch, rows, 2 * C_WIDTH), index)
    y = pl.pallas_call(
        functools.partial(_dft_body, seq=seq), grid=(seq // tm, half // tk),
        in_specs=[pl.BlockSpec((tm, tk), lambda i, k: (i, k)),
                  pl.BlockSpec((tm, tk), lambda i, k: (i, k)),
                  blk(tk, lambda i, k: (0, k, 0)),
                  blk(tk, lambda i, k: (0, k, 0)),
                  blk(8, lambda i, k: (0, jnp.maximum(k * rows8 - 1, 0), 0)),
                  blk(8, lambda i, k: (0, half // 8, 0))],
        out_specs=pl.BlockSpec((batch, tm, C_WIDTH), lambda i, k: (0, i, 0)),
        out_shape=jax.ShapeDtypeStruct((batch, seq, C_WIDTH), BF16),
        scratch_shapes=[pltpu.VMEM((batch, tm, C_WIDTH), F32)],
        compiler_params=_cparams(("arbitrary", "arbitrary")), name="seq_dft",
    )(cmat, nsmat, x3, x3_flip, x3_flip, x3)
    return y.reshape(batch * seq, C_WIDTH)


def _merge_residues(scr, ref, dil):
    tm = scr.shape[1]
    for r in range(dil):
        for c in range(A_WIDTH // LANES):
            col = r * A_WIDTH + c * LANES
            scr[c, pl.ds(r, tm // dil, stride=dil), :] = ref[0, :, col:col + LANES].astype(F32)
    return jnp.concatenate([scr[c] for c in range(A_WIDTH // LANES)], axis=1)


def _out_body(x_ref, o1_ref, o4_ref, o16_ref, l1_ref, l4_ref, l16_ref, ob_ref, oc_ref,
              ga_ref, gb_ref, gc_ref, wa_ref, wb_ref, wc_ref, y_ref, so4, sl4, so16, sl16):
    l1 = l1_ref[0]
    l4 = _merge_residues(sl4, l4_ref, DILATIONS[1])
    l16 = _merge_residues(sl16, l16_ref, DILATIONS[2])
    o4 = _merge_residues(so4, o4_ref, DILATIONS[1])
    o16 = _merge_residues(so16, o16_ref, DILATIONS[2])
    m = jnp.maximum(jnp.maximum(l1, l4), l16)
    e1, e4, e16 = jnp.exp(l1 - m), jnp.exp(l4 - m), jnp.exp(l16 - m)
    oa = (o1_ref[0].astype(F32) * e1 + o4 * e4 + o16 * e16) / (e1 + e4 + e16)
    na = _rms(oa, ga_ref[...]).astype(BF16)
    nb = _rms(ob_ref[...].astype(F32), gb_ref[...]).astype(BF16)
    nc = _rms(oc_ref[...].astype(F32), gc_ref[...]).astype(BF16)
    y_ref[...] = (x_ref[...] + _dot(na, wa_ref[...]) + _dot(nb, wb_ref[...]) + _dot(nc, wc_ref[...]))


def _out_call(x2, oa, la, ob, oc, ga, gb, gc, wa, wb, wc, batch, seq):
    t, d = x2.shape
    tm = min(TM_PROJ, seq)
    ns = seq // tm
    row = lambda i: (i, 0)
    const = lambda i: (0, 0)
    wide = lambda w: pl.BlockSpec((tm, w), row)
    view = lambda dil: pl.BlockSpec((1, tm // dil, dil * A_WIDTH), lambda i: (i // ns, i % ns, 0))
    views = [view(dil) for dil in DILATIONS]
    slab = pltpu.VMEM((A_WIDTH // LANES, tm, LANES), F32)
    return pl.pallas_call(
        _out_body, grid=(t // tm,),
        in_specs=[wide(d)] + views + views + [wide(B_WIDTH), wide(C_WIDTH),
                  pl.BlockSpec((1, A_WIDTH), const), pl.BlockSpec((1, B_WIDTH), const),
                  pl.BlockSpec((1, C_WIDTH), const),
                  pl.BlockSpec((A_WIDTH, d), const), pl.BlockSpec((B_WIDTH, d), const),
                  pl.BlockSpec((C_WIDTH, d), const)],
        out_specs=wide(d),
        out_shape=jax.ShapeDtypeStruct((t, d), F32),
        scratch_shapes=[slab, slab, slab, slab],
        compiler_params=_cparams(("arbitrary",)), name="out_proj",
    )(x2, oa[0], oa[1], oa[2], la[0], la[1], la[2], ob, oc, ga, gb, gc, wa, wb, wc)


def _ffn_body(x_ref, g_ref, wg_ref, wu_ref, wd_ref, y_ref):
    x = x_ref[...]
    h = _rms(x, g_ref[...]).astype(BF16)
    gate = _dot(h, wg_ref[...])
    up = _dot(h, wu_ref[...])
    y_ref[...] = x + _dot((jax.nn.silu(gate) * up).astype(BF16), wd_ref[...])


def _ffn_call(x2, g, wg, wu, wd):
    t, d = x2.shape
    d_ff = wg.shape[1]
    tm = min(TM_FFN, t)
    resident = lambda shape: pl.BlockSpec(shape, lambda i: (0, 0), pipeline_mode=pl.Buffered(1))
    return pl.pallas_call(
        _ffn_body, grid=(t // tm,),
        in_specs=[pl.BlockSpec((tm, d), lambda i: (i, 0)),
                  pl.BlockSpec((1, d), lambda i: (0, 0)),
                  resident((d, d_ff)), resident((d, d_ff)), resident((d_ff, d))],
        out_specs=pl.BlockSpec((tm, d), lambda i: (i, 0)),
        out_shape=jax.ShapeDtypeStruct((t, d), F32),
        compiler_params=_cparams(("arbitrary",)), name="dense_ffn",
    )(x2, g, wg, wu, wd)


def _to_token_tiles(ref, val):
    rows, d = val.shape
    parts = d // LANES
    for c in range(parts):
        ref[pl.ds(c, rows, stride=parts), :] = val[:, c * LANES:(c + 1) * LANES]


def _from_token_tiles(ref, rows):
    parts = ref.shape[0] // rows
    return jnp.concatenate([ref[pl.ds(c, rows, stride=parts), :] for c in range(parts)], axis=1)


def _router_body(x_ref, g_ref, wr_ref, idx_ref, gate_ref, htile_ref):
    h = _rms(x_ref[...], g_ref[...])
    _to_token_tiles(htile_ref, h)
    logits = lax.dot_general(wr_ref[...], h, (((1,), (1,)), ((), ())),
                             precision=lax.Precision.HIGHEST, preferred_element_type=F32)
    eid = lax.broadcasted_iota(jnp.int32, logits.shape, 0)
    m1 = jnp.max(logits, axis=0, keepdims=True)
    i1 = jnp.min(jnp.where(logits == m1, eid, N_EXPERTS), axis=0, keepdims=True)
    rest = jnp.where(eid == i1, -jnp.inf, logits)
    m2 = jnp.max(rest, axis=0, keepdims=True)
    i2 = jnp.min(jnp.where(rest == m2, eid, N_EXPERTS), axis=0, keepdims=True)
    e2 = jnp.exp(m2 - m1)
    idx_ref[...] = jnp.concatenate([i1, i2], axis=0)
    gate_ref[...] = jnp.concatenate([1.0 / (1.0 + e2), e2 / (1.0 + e2)], axis=0)


def _router_call(x2, g, wr_t):
    t, d = x2.shape
    tm = min(TM_ROUTER, t)
    return pl.pallas_call(
        _router_body, grid=(t // tm,),
        in_specs=[pl.BlockSpec((tm, d), lambda i: (i, 0)),
                  pl.BlockSpec((1, d), lambda i: (0, 0)),
                  pl.BlockSpec((N_EXPERTS, d), lambda i: (0, 0))],
        out_specs=(pl.BlockSpec((TOP_K, tm), lambda i: (0, i)),
                   pl.BlockSpec((TOP_K, tm), lambda i: (0, i)),
                   pl.BlockSpec((tm * d // LANES, LANES), lambda i: (i, 0))),
        out_shape=(jax.ShapeDtypeStruct((TOP_K, t), jnp.int32),
                   jax.ShapeDtypeStruct((TOP_K, t), F32),
                   jax.ShapeDtypeStruct((t * d // LANES, LANES), F32)),
        compiler_params=_cparams(("arbitrary",)), name="router",
    )(x2, g, wr_t)


def _expert_body(bexp_ref, tok_cur, tok_next, dst_prev, dst_cur, h_hbm, wg_ref, wu_ref, wd_ref,
                 y_hbm, xb0, xb1, hbuf, acc_ref, yb0, yb1, gsem, ssem):
    i = pl.program_id(0)
    f = pl.program_id(1)
    nb = pl.num_programs(0)
    nf = pl.num_programs(1)
    rows = hbuf.shape[0]
    parts = xb0.shape[0] // rows
    chunk = rows // EXP_STEPS

    def variant(xcur, xnxt, ycur, yprv, s, compute):
        def tile(ref, first_row):
            if not isinstance(first_row, int):
                first_row = pl.multiple_of(first_row, parts)
            return ref.at[pl.ds(first_row, parts)]

        def row_in(r, src, buf, sem):
            return pltpu.make_async_copy(tile(h_hbm, src), tile(buf, r * parts), sem)

        def row_out(r, dst, buf, sem):
            return pltpu.make_async_copy(tile(buf, r * parts), tile(y_hbm, dst), sem)

        def all_rows(fn):
            def body(r, c):
                fn(r)
                return c
            lax.fori_loop(0, rows, body, 0, unroll=8)

        @pl.when(f == 0)
        def _():
            @pl.when(i == 0)
            def _():
                all_rows(lambda r: row_in(r, tok_cur[0, 0, r], xcur, gsem.at[s]).start())
                yprv[...] = jnp.zeros_like(yprv)

            all_rows(lambda r: row_in(r, 0, xcur, gsem.at[s]).wait())
            if compute:
                hbuf[...] = _from_token_tiles(xcur, rows).astype(BF16)
                acc_ref[...] = jnp.zeros_like(acc_ref)

        base = pl.multiple_of(f * chunk, chunk)
        for k in range(chunk):
            r = base + k
            row_in(r, tok_next[0, 0, r], xnxt, gsem.at[1 - s]).start()
            row_out(r, dst_prev[0, 0, r], yprv, ssem.at[1 - s]).start()
        if compute:
            h = hbuf[...]
            gate = _dot(h, wg_ref[0])
            up = _dot(h, wu_ref[0])
            acc_ref[...] += _dot((jax.nn.silu(gate) * up).astype(BF16), wd_ref[0])

        @pl.when(f == nf - 1)
        def _():
            if compute:
                _to_token_tiles(ycur, acc_ref[...])
            else:
                ycur[...] = jnp.zeros_like(ycur)
            all_rows(lambda r: row_out(r, 0, yprv, ssem.at[1 - s]).wait())

            @pl.when(i == nb - 1)
            def _():
                all_rows(lambda r: row_out(r, dst_cur[0, 0, r], ycur, ssem.at[s]).start())
                all_rows(lambda r: row_out(r, 0, ycur, ssem.at[s]).wait())
                all_rows(lambda r: row_in(r, 0, xnxt, gsem.at[1 - s]).wait())

    even = i % 2 == 0
    padding_only = bexp_ref[i] >= N_EXPERTS
    for is_even, bufs in ((True, (xb0, xb1, yb0, yb1, 0)), (False, (xb1, xb0, yb1, yb0, 1))):
        for compute in (True, False):
            @pl.when((even == is_even) & (padding_only != compute))
            def _(bufs=bufs, compute=compute):
                variant(*bufs, compute)


def _expert_call(h_tiles, d, wg, wu, wd, bexp, row_tok, row_dst, n_out_rows):
    parts = d // LANES
    n_blocks = bexp.shape[0]
    d_ff = wg.shape[2]
    assert d_ff % EXP_STEPS == 0 and (d_ff // EXP_STEPS) % LANES == 0
    tf = d_ff // EXP_STEPS
    last = n_blocks - 1
    rows_of = lambda which: pl.BlockSpec((1, 1, TM_EXP), which, memory_space=pltpu.SMEM)
    tiles = pltpu.VMEM((TM_EXP * parts, LANES), F32)
    grid_spec = pltpu.PrefetchScalarGridSpec(
        num_scalar_prefetch=1,
        grid=(n_blocks, EXP_STEPS),
        in_specs=[
            rows_of(lambda i, f, be: (i, 0, 0)),
            rows_of(lambda i, f, be: (jnp.minimum(i + 1, last), 0, 0)),
            rows_of(lambda i, f, be: (jnp.maximum(i - 1, 0), 0, 0)),
            rows_of(lambda i, f, be: (i, 0, 0)),
            pl.BlockSpec(memory_space=pl.ANY),
            pl.BlockSpec((1, d, tf), lambda i, f, be: (be[i] % N_EXPERTS, 0, f)),
            pl.BlockSpec((1, d, tf), lambda i, f, be: (be[i] % N_EXPERTS, 0, f)),
            pl.BlockSpec((1, tf, d), lambda i, f, be: (be[i] % N_EXPERTS, f, 0)),
        ],
        out_specs=pl.BlockSpec(memory_space=pl.ANY),
        scratch_shapes=[tiles, tiles, pltpu.VMEM((TM_EXP, d), BF16), pltpu.VMEM((TM_EXP, d), F32),
                        tiles, tiles, pltpu.SemaphoreType.DMA((2,)), pltpu.SemaphoreType.DMA((2,))],
    )
    return pl.pallas_call(
        _expert_body, grid_spec=grid_spec,
        out_shape=jax.ShapeDtypeStruct((n_out_rows * parts, LANES), F32),
        compiler_params=_cparams(("arbitrary", "arbitrary")), name="experts",
    )(bexp, row_tok, row_tok, row_dst, row_dst, h_tiles, wg, wu, wd)


def _combine_body(x_ref, y0_ref, y1_ref, gate_ref, gf_ref, o_ref, *, final):
    g0 = gate_ref[:, 0:1]
    g1 = gate_ref[:, 1:2]
    rows = x_ref.shape[0]
    y = x_ref[...] + (_from_token_tiles(y0_ref, rows) * g0 + _from_token_tiles(y1_ref, rows) * g1)
    o_ref[...] = _rms(y, gf_ref[...]) if final else y


def _combine_call(x2, y_tiles, gates_t, gfinal, final):
    t, d = x2.shape
    tm = min(TM_ROUTER, t)
    nt = t // tm
    parts = d // LANES
    return pl.pallas_call(
        functools.partial(_combine_body, final=final), grid=(nt,),
        in_specs=[pl.BlockSpec((tm, d), lambda i: (i, 0)),
                  pl.BlockSpec((tm * parts, LANES), lambda i: (i, 0)),
                  pl.BlockSpec((tm * parts, LANES), lambda i: (i + nt, 0)),
                  pl.BlockSpec((tm, TOP_K), lambda i: (i, 0)),
                  pl.BlockSpec((1, d), lambda i: (0, 0))],
        out_specs=pl.BlockSpec((tm, d), lambda i: (i, 0)),
        out_shape=jax.ShapeDtypeStruct((t, d), F32),
        compiler_params=_cparams(("arbitrary",)), name="moe_combine",
    )(x2, y_tiles, y_tiles, gates_t, gfinal)


def _final_norm_body(x_ref, g_ref, o_ref):
    o_ref[...] = _rms(x_ref[...], g_ref[...])


def _final_norm_call(x2, g):
    t, d = x2.shape
    tm = min(TM_ROUTER, t)
    return pl.pallas_call(
        _final_norm_body, grid=(t // tm,),
        in_specs=[pl.BlockSpec((tm, d), lambda i: (i, 0)), pl.BlockSpec((1, d), lambda i: (0, 0))],
        out_specs=pl.BlockSpec((tm, d), lambda i: (i, 0)),
        out_shape=jax.ShapeDtypeStruct((t, d), F32),
        compiler_params=_cparams(("arbitrary",)), name="final_norm",
    )(x2, g)


def _route_plan(idx, t):
    n_assign = TOP_K * t
    flat_e = idx.reshape(-1)
    order = jnp.argsort(flat_e).astype(jnp.int32)
    counts = jnp.sum(flat_e[:, None] == jnp.arange(N_EXPERTS, dtype=jnp.int32)[None, :], axis=0,
                     dtype=jnp.int32)
    padded = ((counts + TM_EXP - 1) // TM_EXP) * TM_EXP
    start = jnp.cumsum(counts) - counts
    cum_padded = jnp.cumsum(padded)
    start_padded = cum_padded - padded
    n_blocks = -(-n_assign // TM_EXP) + N_EXPERTS
    block_start = jnp.arange(n_blocks, dtype=jnp.int32) * TM_EXP
    bexp = jnp.minimum(jnp.sum(block_start[:, None] >= cum_padded[None, :], axis=1),
                       N_EXPERTS - 1).astype(jnp.int32)
    off = block_start[:, None] + jnp.arange(TM_EXP, dtype=jnp.int32)[None, :] - start_padded[bexp][:, None]
    valid = off < counts[bexp][:, None]
    a = order[jnp.clip(start[bexp][:, None] + off, 0, n_assign - 1)]
    bexp = bexp + N_EXPERTS * (~jnp.any(valid, axis=1)).astype(jnp.int32)
    pad_rank = (jnp.cumsum((~valid).reshape(-1).astype(jnp.int32)) - 1).reshape(valid.shape)
    row_tok = jnp.where(valid, a % t, 0).astype(jnp.int32)
    row_dst = jnp.where(valid, a, n_assign + pad_rank).astype(jnp.int32)
    n_out_rows = n_blocks * TM_EXP
    return bexp, row_tok.reshape(n_blocks, 1, TM_EXP), row_dst.reshape(n_blocks, 1, TM_EXP), n_out_rows


def _rope_tables(seq):
    pos = jnp.arange(seq, dtype=F32)[:, None]
    half_a = A_ROT_DIM // 2
    ang_a = pos * (ROPE_THETA ** (-jnp.arange(half_a, dtype=F32) / half_a))[None, :]
    one = jnp.ones((seq, HEAD_DIM - A_ROT_DIM), F32)
    cos_head = jnp.concatenate([jnp.cos(ang_a), jnp.cos(ang_a), one], axis=1)
    sin_head = jnp.concatenate([jnp.sin(ang_a), jnp.sin(ang_a), 0 * one], axis=1)
    cosa = jnp.tile(cos_head, (1, A_HEADS))
    sina = jnp.tile(sin_head, (1, A_HEADS))
    half_b = B_ROPE // 2
    ang_b = pos * (ROPE_THETA ** (-jnp.arange(half_b, dtype=F32) / half_b))[None, :]
    zl = jnp.zeros((seq, B_NOPE), F32)
    zr = jnp.zeros((seq, HEAD_PAD - B_NOPE - B_ROPE), F32)
    cosk = jnp.concatenate([zl, jnp.cos(ang_b), jnp.cos(ang_b), zr], axis=1)
    sink = jnp.concatenate([zl, jnp.sin(ang_b), jnp.sin(ang_b), zr], axis=1)
    return cosa, sina, cosk, sink, jnp.cos(ang_b).T, jnp.sin(ang_b).T


def _rotate_half_matrix():
    r = np.zeros((A_WIDTH, A_WIDTH), np.float32)
    half = A_ROT_DIM // 2
    for hd in range(A_HEADS):
        for i in range(half):
            r[hd * HEAD_DIM + half + i, hd * HEAD_DIM + i] = -1.0
            r[hd * HEAD_DIM + i, hd * HEAD_DIM + half + i] = 1.0
    return jnp.asarray(r, BF16)


def _dft_constants(seq):
    lo_n = 64
    hi_n = seq // lo_n
    k = jnp.arange(seq, dtype=jnp.int32)[:, None]
    ang_hi = ((k * jnp.arange(hi_n // 2, dtype=jnp.int32)[None, :]) % hi_n).astype(F32) * (2.0 * np.pi / hi_n)
    ang_lo = ((k * jnp.arange(lo_n, dtype=jnp.int32)[None, :]) % seq).astype(F32) * (2.0 * np.pi / seq)
    ch, sh = jnp.cos(ang_hi)[:, :, None], jnp.sin(ang_hi)[:, :, None]
    cl, sl = jnp.cos(ang_lo)[:, None, :], jnp.sin(ang_lo)[:, None, :]
    scale = seq ** -0.5
    cmat = ((ch * cl - sh * sl) * scale).astype(BF16).reshape(seq, seq // 2)
    nsmat = ((sh * cl + ch * sl) * -scale).astype(BF16).reshape(seq, seq // 2)
    c = np.arange(C_GROUP_DIM)
    angc = 2.0 * np.pi * ((c[:, None] * c[None, :]) % C_GROUP_DIM) / C_GROUP_DIM
    eye = np.eye(C_GROUPS)
    cs = np.concatenate([np.kron(eye, np.cos(angc)), np.kron(eye, np.sin(angc))], axis=1)
    return cmat, nsmat, jnp.asarray(cs * C_GROUP_DIM ** -0.5, BF16)


def _layer_weights(w_in, w_uq, w_ukv, w_out):
    d = w_in.shape[0]
    c_kpe = 3 * A_WIDTH + Q_LORA + KV_LORA
    w_kpe = w_in[:, c_kpe:c_kpe + B_ROPE]
    half = B_ROPE // 2
    w_kpe_rot = jnp.concatenate([-w_kpe[:, half:], w_kpe[:, :half]], axis=1)
    zl = jnp.zeros((d, B_NOPE), w_in.dtype)
    zr = jnp.zeros((d, HEAD_PAD - B_NOPE - B_ROPE), w_in.dtype)
    wm = jnp.concatenate([w_in[:, :c_kpe], zl, w_kpe, zr, zl, w_kpe_rot, zr,
                          w_in[:, c_kpe + B_ROPE:]], axis=1).astype(BF16)
    uq = w_uq.reshape(Q_LORA, B_HEADS, B_NOPE + B_ROPE)
    uq = jnp.pad(uq, ((0, 0), (0, 0), (0, HEAD_PAD - B_NOPE - B_ROPE)))
    wuq_t = uq.reshape(Q_LORA, B_HEADS * HEAD_PAD).T.astype(BF16)
    ukv = w_ukv.reshape(KV_LORA, B_HEADS, B_NOPE + B_V)
    wkn = jnp.pad(ukv[:, :, :B_NOPE], ((0, 0), (0, 0), (0, HEAD_PAD - B_NOPE)))
    wkn = wkn.reshape(KV_LORA, B_HEADS * HEAD_PAD).astype(BF16)
    wv_t = ukv[:, :, B_NOPE:].reshape(KV_LORA, B_WIDTH).T.astype(BF16)
    wo = w_out.astype(BF16)
    return wm, wuq_t, wkn, wv_t, wo[:A_WIDTH], wo[A_WIDTH:A_WIDTH + B_WIDTH], wo[A_WIDTH + B_WIDTH:]


def kernel(x, attn_norm, w_in, q_norm, w_uq, kv_norm, w_ukv, mix_gain, w_out, ffn_norm,
           w_ffn_gate, w_ffn_up, w_ffn_down, w_router, w_exp_gate, w_exp_up, w_exp_down,
           final_norm):
    batch, seq, d = x.shape
    depth = w_in.shape[0]
    t = batch * seq
    assert seq % (max(DILATIONS) * 2 * WIN_SUB) == 0
    tabs = _rope_tables(seq)
    rot = _rotate_half_matrix()
    cmat, nsmat, cs = _dft_constants(seq)
    row = lambda v: v.reshape(1, -1)

    x2 = x.reshape(t, d)
    for l in range(depth):
        wm, wuq_t, wkn, wv_t, wa, wb, wc = _layer_weights(w_in[l], w_uq[l], w_ukv[l], w_out[l])
        qkv_views, (qbt, kb, vbt, xcs, xcs_flip) = _proj_call(
            x2, row(attn_norm[l]), wm, rot, tabs, row(q_norm[l]), wuq_t, row(kv_norm[l]), wkn, wv_t,
            cs, batch, seq)
        win = [_win_call(*(a.reshape(batch, seq // dil, dil * A_WIDTH) for a in qkv), batch, seq, dil)
               for dil, qkv in zip(DILATIONS, qkv_views)]
        ob = _mla_call(qbt, kb, vbt, batch, seq)
        oc = _dft_call(cmat, nsmat, xcs, xcs_flip, batch, seq)
        g = mix_gain[l]
        x2 = _out_call(x2, [w[0] for w in win], [w[1] for w in win], ob, oc,
                       row(g[:A_WIDTH]), row(g[A_WIDTH:A_WIDTH + B_WIDTH]), row(g[A_WIDTH + B_WIDTH:]),
                       wa, wb, wc, batch, seq)
        i = l // 2
        if l % 2 == 0:
            x2 = _ffn_call(x2, row(ffn_norm[l]), w_ffn_gate[i].astype(BF16), w_ffn_up[i].astype(BF16),
                           w_ffn_down[i].astype(BF16))
            if l == depth - 1:
                x2 = _final_norm_call(x2, row(final_norm))
        else:
            idx, gates, h_tiles = _router_call(x2, row(ffn_norm[l]), w_router[i].T)
            bexp, row_tok, row_dst, n_out_rows = _route_plan(idx, t)
            parts = d // LANES
            y_tiles = _expert_call(h_tiles, d, w_exp_gate[i].astype(BF16), w_exp_up[i].astype(BF16),
                                   w_exp_down[i].astype(BF16), bexp, row_tok * parts, row_dst * parts,
                                   n_out_rows)
            x2 = _combine_call(x2, y_tiles, gates.T, row(final_norm), final=(l == depth - 1))
    return x2.reshape(batch, seq, d)
```

```python
import functools
import math

import jax
import jax.numpy as jnp
import numpy as np
from jax import lax
from jax.experimental import pallas as pl
from jax.experimental.pallas import tpu as pltpu

F32 = jnp.float32
BF16 = jnp.bfloat16

HEAD_DIM = 64
A_HEADS = 6
A_WIDTH = A_HEADS * HEAD_DIM
A_ROT_DIM = HEAD_DIM // 4
DILATIONS = (1, 4, 16)
WINDOW_RADIUS = 64
B_HEADS = 6
B_NOPE = 64
B_ROPE = 32
B_V = 64
B_WIDTH = B_HEADS * B_V
Q_LORA = 384
KV_LORA = 128
C_GROUPS = 4
C_GROUP_DIM = 64
C_WIDTH = C_GROUPS * C_GROUP_DIM
ROPE_THETA = 500000.0
N_EXPERTS = 8
TOP_K = 2
RMS_EPS = 1e-6
NEG_INF = -1e30

LANES = 128
BF16_SUBLANES = 16
HEAD_PAD = 128
VMEM_LIMIT_BYTES = 56 * 1024 * 1024

TM_PROJ = 512
TQ_WIN = 1024
WIN_SUB = 128
TQ_MLA = 512
TK_MLA = 256
TM_DFT = 1024
TK_DFT = 512
TM_FFN = 512
TM_ROUTER = 512
EXP_STEPS = 7
TM_EXP = 128 * EXP_STEPS


def _cparams(sem):
    return pltpu.CompilerParams(dimension_semantics=sem, vmem_limit_bytes=VMEM_LIMIT_BYTES)


def _rms(x, g):
    return x * lax.rsqrt(jnp.mean(x * x, axis=-1, keepdims=True) + RMS_EPS) * g


def _dot(a, b):
    return jnp.dot(a, b, preferred_element_type=F32)


def _dot_nt(a, b):
    return lax.dot_general(a, b, (((1,), (1,)), ((), ())), preferred_element_type=F32)


def _split_residues(scr, val, nat_ref, view_refs):
    tm = val.shape[0]
    nat_ref[...] = val.astype(BF16)
    for c in range(A_WIDTH // LANES):
        scr[c] = val[:, c * LANES:(c + 1) * LANES]
    for dil, ref in zip(DILATIONS[1:], view_refs):
        for r in range(dil):
            for c in range(A_WIDTH // LANES):
                col = r * A_WIDTH + c * LANES
                ref[0, :, col:col + LANES] = scr[c, pl.ds(r, tm // dil, stride=dil), :].astype(BF16)


def _proj_body(x_ref, g_ref, wm_ref, rot_ref, cosa_ref, sina_ref, cosk_ref, sink_ref,
               cosq_ref, sinq_ref, qn_ref, wuq_ref, kvn_ref, wkn_ref, wv_ref, cs_ref, flip_ref,
               qa_ref, qa4_ref, qa16_ref, ka_ref, ka4_ref, ka16_ref, va_ref, va4_ref, va16_ref,
               qbt_ref, kb_ref, vbt_ref, xcs_ref, xcs_flip_ref, scr):
    h = _rms(x_ref[...], g_ref[...]).astype(BF16)
    z = _dot(h, wm_ref[...])
    o = 0
    qa = z[:, o:o + A_WIDTH]; o += A_WIDTH
    ka = z[:, o:o + A_WIDTH]; o += A_WIDTH
    va = z[:, o:o + A_WIDTH]; o += A_WIDTH
    cq = z[:, o:o + Q_LORA]; o += Q_LORA
    ckv = z[:, o:o + KV_LORA]; o += KV_LORA
    kpe = z[:, o:o + HEAD_PAD]; o += HEAD_PAD
    kpe_rot = z[:, o:o + HEAD_PAD]; o += HEAD_PAD
    fc = z[:, o:o + C_WIDTH]

    cosa = cosa_ref[...]
    sina = sina_ref[...]
    rot = rot_ref[...]
    _split_residues(scr, qa * cosa + _dot(qa.astype(BF16), rot) * sina, qa_ref, (qa4_ref, qa16_ref))
    _split_residues(scr, ka * cosa + _dot(ka.astype(BF16), rot) * sina, ka_ref, (ka4_ref, ka16_ref))
    _split_residues(scr, va, va_ref, (va4_ref, va16_ref))

    cqn = _rms(cq, qn_ref[...]).astype(BF16)
    qbt = _dot_nt(wuq_ref[...], cqn) * ((B_NOPE + B_ROPE) ** -0.5 * math.log2(math.e))
    cosq = cosq_ref[...]
    sinq = sinq_ref[...]
    half = B_ROPE // 2
    for hd in range(B_HEADS):
        r0 = hd * HEAD_PAD
        x1 = qbt[r0 + B_NOPE:r0 + B_NOPE + half]
        x2 = qbt[r0 + B_NOPE + half:r0 + B_NOPE + B_ROPE]
        qbt_ref[0, r0:r0 + B_NOPE, :] = qbt[r0:r0 + B_NOPE].astype(BF16)
        qbt_ref[0, r0 + B_NOPE:r0 + B_NOPE + half, :] = (x1 * cosq - x2 * sinq).astype(BF16)
        qbt_ref[0, r0 + B_NOPE + half:r0 + B_NOPE + B_ROPE, :] = (x2 * cosq + x1 * sinq).astype(BF16)
        qbt_ref[0, r0 + B_NOPE + B_ROPE:r0 + HEAD_PAD, :] = jnp.zeros(
            (HEAD_PAD - B_NOPE - B_ROPE, qbt.shape[1]), BF16)

    ckvn = _rms(ckv, kvn_ref[...]).astype(BF16)
    kn = _dot(ckvn, wkn_ref[...])
    kpe_r = kpe * cosk_ref[...] + kpe_rot * sink_ref[...]
    for hd in range(B_HEADS):
        kb_ref[0, hd] = (kn[:, hd * HEAD_PAD:(hd + 1) * HEAD_PAD] + kpe_r).astype(BF16)
    vbt_ref[0] = _dot_nt(wv_ref[...], ckvn).astype(BF16)

    xcs = _dot(fc.astype(BF16), cs_ref[...]).astype(BF16)
    xcs_ref[0] = xcs
    xcs_flip_ref[0] = _dot(flip_ref[...], xcs).astype(BF16)


def _proj_call(x2, g, wm, rot, tabs, qn, wuq, kvn, wkn, wv, cs, batch, seq):
    t, d = x2.shape
    tm = min(TM_PROJ, seq)
    ns = seq // tm
    nm = wm.shape[1]
    cosa, sina, cosk, sink, cosq, sinq = tabs
    row = lambda i: (i, 0)
    pos = lambda i: (i % ns, 0)
    post = lambda i: (0, i % ns)
    const = lambda i: (0, 0)
    in_specs = [
        pl.BlockSpec((tm, d), row),
        pl.BlockSpec((1, d), const),
        pl.BlockSpec((d, nm), const),
        pl.BlockSpec((A_WIDTH, A_WIDTH), const),
        pl.BlockSpec((tm, A_WIDTH), pos),
        pl.BlockSpec((tm, A_WIDTH), pos),
        pl.BlockSpec((tm, HEAD_PAD), pos),
        pl.BlockSpec((tm, HEAD_PAD), pos),
        pl.BlockSpec((B_ROPE // 2, tm), post),
        pl.BlockSpec((B_ROPE // 2, tm), post),
        pl.BlockSpec((1, Q_LORA), const),
        pl.BlockSpec((B_HEADS * HEAD_PAD, Q_LORA), const),
        pl.BlockSpec((1, KV_LORA), const),
        pl.BlockSpec((KV_LORA, B_HEADS * HEAD_PAD), const),
        pl.BlockSpec((B_WIDTH, KV_LORA), const),
        pl.BlockSpec((C_WIDTH, 2 * C_WIDTH), const),
        pl.BlockSpec((tm, tm), const),
    ]
    bpos = lambda i: (i // ns, i % ns, 0)
    a_shapes, a_specs = [], []
    for _ in range(3):
        a_shapes.append(jax.ShapeDtypeStruct((t, A_WIDTH), BF16))
        a_specs.append(pl.BlockSpec((tm, A_WIDTH), row))
        for dil in DILATIONS[1:]:
            a_shapes.append(jax.ShapeDtypeStruct((batch, seq // dil, dil * A_WIDTH), BF16))
            a_specs.append(pl.BlockSpec((1, tm // dil, dil * A_WIDTH), bpos))
    out_shape = (
        *a_shapes,
        jax.ShapeDtypeStruct((batch, B_HEADS * HEAD_PAD, seq), BF16),
        jax.ShapeDtypeStruct((batch, B_HEADS, seq, HEAD_PAD), BF16),
        jax.ShapeDtypeStruct((batch, B_WIDTH, seq), BF16),
        jax.ShapeDtypeStruct((batch, seq, 2 * C_WIDTH), BF16),
        jax.ShapeDtypeStruct((batch, seq, 2 * C_WIDTH), BF16),
    )
    out_specs = (
        *a_specs,
        pl.BlockSpec((1, B_HEADS * HEAD_PAD, tm), lambda i: (i // ns, 0, i % ns)),
        pl.BlockSpec((1, B_HEADS, tm, HEAD_PAD), lambda i: (i // ns, 0, i % ns, 0)),
        pl.BlockSpec((1, B_WIDTH, tm), lambda i: (i // ns, 0, i % ns)),
        pl.BlockSpec((1, tm, 2 * C_WIDTH), bpos),
        pl.BlockSpec((1, tm, 2 * C_WIDTH), lambda i: (i // ns, ns - 1 - i % ns, 0)),
    )
    flip = jnp.asarray(np.eye(tm, dtype=np.float32)[::-1], BF16)
    outs = pl.pallas_call(
        _proj_body, grid=(t // tm,), in_specs=in_specs, out_specs=out_specs, out_shape=out_shape,
        scratch_shapes=[pltpu.VMEM((A_WIDTH // LANES, tm, LANES), F32)],
        compiler_params=_cparams(("arbitrary",)), name="proj",
    )(x2, g, wm, rot, cosa, sina, cosk, sink, cosq, sinq, qn, wuq, kvn, wkn, wv, cs, flip)
    nd = len(DILATIONS)
    qkv_views = [tuple(outs[a * nd + j] for a in range(3)) for j in range(nd)]
    return qkv_views, outs[3 * nd:]


def _win_body(q_ref, kp_ref, km_ref, kn_ref, vp_ref, vm_ref, vn_ref, o_ref, l_ref, *, seg_len, tq):
    j = pl.program_id(2)
    span = WIN_SUB + 2 * WINDOW_RADIUS
    lane = lax.broadcasted_iota(jnp.int32, (1, LANES), 1)
    lo_half = lane < HEAD_DIM
    rel = (lax.broadcasted_iota(jnp.int32, (WIN_SUB, span), 1) - WINDOW_RADIUS
           - lax.broadcasted_iota(jnp.int32, (WIN_SUB, span), 0))
    band = jnp.abs(rel) <= WINDOW_RADIUS
    kcol = lax.broadcasted_iota(jnp.int32, (1, span), 1)
    n_sub = tq // WIN_SUB
    biases = []
    for sb in range(n_sub):
        kpos = j * tq + sb * WIN_SUB - WINDOW_RADIUS + kcol
        biases.append(jnp.where(band & (kpos >= 0) & (kpos < seg_len), 0.0, NEG_INF))
    for pair in range(A_HEADS // 2):
        cs = slice(pair * LANES, (pair + 1) * LANES)
        q2 = q_ref[0, :, cs] * (HEAD_DIM ** -0.5 * math.log2(math.e))
        qh = (jnp.where(lo_half, q2, 0).astype(BF16), jnp.where(lo_half, 0, q2).astype(BF16))
        kcat = jnp.concatenate([kp_ref[0, :, cs], km_ref[0, :, cs], kn_ref[0, :, cs]], axis=0)
        vcat = jnp.concatenate([vp_ref[0, :, cs], vm_ref[0, :, cs], vn_ref[0, :, cs]], axis=0)
        for sb in range(n_sub):
            r0 = sb * WIN_SUB
            ks = kcat[r0:r0 + span]
            vs = vcat[r0:r0 + span]
            outs, lses = [], []
            for hh in range(2):
                s = _dot_nt(qh[hh][r0:r0 + WIN_SUB], ks) + biases[sb]
                m = jnp.max(s, axis=-1, keepdims=True)
                p = jnp.exp2(s - m)
                den = jnp.sum(p, axis=-1, keepdims=True)
                outs.append(_dot(p.astype(BF16), vs) / den)
                lses.append(m * math.log(2.0) + jnp.log(den))
            o_ref[0, r0:r0 + WIN_SUB, cs] = jnp.where(lo_half, outs[0], outs[1]).astype(BF16)
            l_ref[0, r0:r0 + WIN_SUB, cs] = jnp.where(lo_half, lses[0], lses[1])


def _win_call(q, k, v, batch, seq, dil):
    seg = seq // dil
    tq = min(TQ_WIN, seg)
    nb = tq // WINDOW_RADIUS
    last = seg // WINDOW_RADIUS - 1
    main = pl.BlockSpec((1, tq, A_WIDTH), lambda b, r, j: (b, j, r))
    prev = pl.BlockSpec((1, WINDOW_RADIUS, A_WIDTH), lambda b, r, j: (b, jnp.maximum(j * nb - 1, 0), r))
    nxt = pl.BlockSpec((1, WINDOW_RADIUS, A_WIDTH), lambda b, r, j: (b, jnp.minimum((j + 1) * nb, last), r))
    return pl.pallas_call(
        functools.partial(_win_body, seg_len=seg, tq=tq),
        grid=(batch, dil, seg // tq),
        in_specs=[main, prev, main, nxt, prev, main, nxt],
        out_specs=(main, main),
        out_shape=(jax.ShapeDtypeStruct((batch, seg, dil * A_WIDTH), BF16),
                   jax.ShapeDtypeStruct((batch, seg, dil * A_WIDTH), F32)),
        compiler_params=_cparams(("arbitrary", "arbitrary", "arbitrary")), name=f"win_d{dil}",
    )(q, k, k, k, v, v, v)


def _mla_body(qt_ref, k_ref, vt_ref, o_ref, sa, sb, pa, pb, acc_ref, *, seq, tk):
    tq = qt_ref.shape[2]
    nk = seq // tk
    qts = [qt_ref[0, h * HEAD_PAD:(h + 1) * HEAD_PAD, :] for h in range(B_HEADS)]
    ones = jnp.ones((BF16_SUBLANES, tk), BF16)

    def scores(h, kt, dst):
        start = kt * tk if isinstance(kt, int) else pl.multiple_of(kt * tk, tk)
        dst[h] = _dot(k_ref[0, h, pl.ds(start, tk), :], qts[h])

    def values(h, kt, src, alpha):
        start = kt * tk if isinstance(kt, int) else pl.multiple_of(kt * tk, tk)
        vt = jnp.concatenate([vt_ref[0, h * B_V:(h + 1) * B_V, pl.ds(start, tk)], ones], axis=0)
        acc_ref[h] = alpha * acc_ref[h] + _dot(vt, src[h])

    def softmax(h, src, dst, m):
        st = src[h]
        m_new = jnp.maximum(m, jnp.max(st, axis=0, keepdims=True))
        dst[h] = jnp.exp2(st - m_new).astype(BF16)
        return m_new, jnp.exp2(m - m_new)

    for h in range(B_HEADS):
        scores(h, 0, sa)
        acc_ref[h] = jnp.zeros((B_V + BF16_SUBLANES, tq), F32)
    pb[...] = jnp.zeros_like(pb)

    def step(i, carry):
        out = []
        for h in range(B_HEADS):
            m, alpha_prev = carry[h]
            t0 = 2 * i
            scores(h, t0 + 1, sb)
            values(h, jnp.maximum(t0 - 1, 0), pb, alpha_prev)
            m, alpha0 = softmax(h, sa, pa, m)
            scores(h, jnp.minimum(t0 + 2, nk - 1), sa)
            values(h, t0, pa, alpha0)
            m, alpha1 = softmax(h, sb, pb, m)
            out.append((m, alpha1))
        return tuple(out)

    init = tuple((jnp.full((1, tq), NEG_INF, F32), jnp.ones((1, tq), F32)) for _ in range(B_HEADS))
    res = lax.fori_loop(0, nk // 2, step, init)
    outs = []
    for h in range(B_HEADS):
        values(h, nk - 1, pb, res[h][1])
        acc = acc_ref[h]
        outs.append(acc[:B_V] / acc[B_V:B_V + 1])
    o_ref[...] = jnp.concatenate(outs, axis=0).T.astype(BF16)


def _mla_call(qbt, kb, vbt, batch, seq):
    tq = min(TQ_MLA, seq)
    tk = min(TK_MLA, seq)
    nq = seq // tq
    assert (seq // tk) % 2 == 0
    return pl.pallas_call(
        functools.partial(_mla_body, seq=seq, tk=tk),
        grid=(batch, nq),
        in_specs=[
            pl.BlockSpec((1, B_HEADS * HEAD_PAD, tq), lambda b, qi: (b, 0, qi)),
            pl.BlockSpec((1, B_HEADS, seq, HEAD_PAD), lambda b, qi: (b, 0, 0, 0)),
            pl.BlockSpec((1, B_WIDTH, seq), lambda b, qi: (b, 0, 0)),
        ],
        out_specs=pl.BlockSpec((tq, B_WIDTH), lambda b, qi: (b * nq + qi, 0)),
        out_shape=jax.ShapeDtypeStruct((batch * seq, B_WIDTH), BF16),
        scratch_shapes=[pltpu.VMEM((B_HEADS, tk, tq), F32), pltpu.VMEM((B_HEADS, tk, tq), F32),
                        pltpu.VMEM((B_HEADS, tk, tq), BF16), pltpu.VMEM((B_HEADS, tk, tq), BF16),
                        pltpu.VMEM((B_HEADS, B_V + BF16_SUBLANES, tq), F32)],
        compiler_params=_cparams(("arbitrary", "arbitrary")), name="mla_attn",
    )(qbt, kb, vbt)


def _dft_body(c_ref, ns_ref, x_ref, xf_ref, xfp_ref, xh_ref, o_ref, acc_ref, *, seq):
    k = pl.program_id(1)
    tm = c_ref.shape[0]
    tk = x_ref.shape[1]

    @pl.when(k == 0)
    def _():
        row = lax.broadcasted_iota(jnp.int32, (tm, 1), 0)
        sign = (1 - 2 * (row & 1)).astype(F32) * (seq ** -0.5)
        for b in range(x_ref.shape[0]):
            acc_ref[b] = sign * xh_ref[b, 0:1, 0:C_WIDTH].astype(F32)

    c = c_ref[...]
    ns = ns_ref[...]
    first_row = lax.broadcasted_iota(jnp.int32, (tk, 1), 0) == 0
    has_prev = (k > 0).astype(F32)
    for b in range(x_ref.shape[0]):
        x = x_ref[b].astype(F32)
        carry_in = xfp_ref[b, 7:8, :].astype(F32) * has_prev
        xr = jnp.where(first_row, carry_in, pltpu.roll(xf_ref[b].astype(F32), 1, axis=0))
        sym = (x[:, 0:C_WIDTH] + xr[:, 0:C_WIDTH]).astype(BF16)
        anti = (x[:, C_WIDTH:2 * C_WIDTH] - xr[:, C_WIDTH:2 * C_WIDTH]).astype(BF16)
        acc_ref[b] += _dot(c, sym) + _dot(ns, anti)

    @pl.when(k == pl.num_programs(1) - 1)
    def _():
        o_ref[...] = acc_ref[...].astype(o_ref.dtype)


def _dft_call(cmat, nsmat, x3, x3_flip, batch, seq):
    half = seq // 2
    tm = min(TM_DFT, seq)
    tk = min(TK_DFT, half)
    rows8 = tk // 8
    blk = lambda rows, index: pl.BlockSpec((batch, rows, 2 * C_WIDTH), index)
    y = pl.pallas_call(
        functools.partial(_dft_body, seq=seq), grid=(seq // tm, half // tk),
        in_specs=[pl.BlockSpec((tm, tk), lambda i, k: (i, k)),
                  pl.BlockSpec((tm, tk), lambda i, k: (i, k)),
                  blk(tk, lambda i, k: (0, k, 0)),
                  blk(tk, lambda i, k: (0, k, 0)),
                  blk(8, lambda i, k: (0, jnp.maximum(k * rows8 - 1, 0), 0)),
                  blk(8, lambda i, k: (0, half // 8, 0))],
        out_specs=pl.BlockSpec((batch, tm, C_WIDTH), lambda i, k: (0, i, 0)),
        out_shape=jax.ShapeDtypeStruct((batch, seq, C_WIDTH), BF16),
        scratch_shapes=[pltpu.VMEM((batch, tm, C_WIDTH), F32)],
        compiler_params=_cparams(("arbitrary", "arbitrary")), name="seq_dft",
    )(cmat, nsmat, x3, x3_flip, x3_flip, x3)
    return y.reshape(batch * seq, C_WIDTH)


def _merge_residues(scr, ref, dil):
    tm = scr.shape[1]
    for r in range(dil):
        for c in range(A_WIDTH // LANES):
            col = r * A_WIDTH + c * LANES
            scr[c, pl.ds(r, tm // dil, stride=dil), :] = ref[0, :, col:col + LANES].astype(F32)
    return jnp.concatenate([scr[c] for c in range(A_WIDTH // LANES)], axis=1)


def _out_body(x_ref, o1_ref, o4_ref, o16_ref, l1_ref, l4_ref, l16_ref, ob_ref, oc_ref,
              ga_ref, gb_ref, gc_ref, wa_ref, wb_ref, wc_ref, y_ref, so4, sl4, so16, sl16):
    l1 = l1_ref[0]
    l4 = _merge_residues(sl4, l4_ref, DILATIONS[1])
    l16 = _merge_residues(sl16, l16_ref, DILATIONS[2])
    o4 = _merge_residues(so4, o4_ref, DILATIONS[1])
    o16 = _merge_residues(so16, o16_ref, DILATIONS[2])
    m = jnp.maximum(jnp.maximum(l1, l4), l16)
    e1, e4, e16 = jnp.exp(l1 - m), jnp.exp(l4 - m), jnp.exp(l16 - m)
    oa = (o1_ref[0].astype(F32) * e1 + o4 * e4 + o16 * e16) / (e1 + e4 + e16)
    na = _rms(oa, ga_ref[...]).astype(BF16)
    nb = _rms(ob_ref[...].astype(F32), gb_ref[...]).astype(BF16)
    nc = _rms(oc_ref[...].astype(F32), gc_ref[...]).astype(BF16)
    y_ref[...] = (x_ref[...] + _dot(na, wa_ref[...]) + _dot(nb, wb_ref[...]) + _dot(nc, wc_ref[...]))


def _out_call(x2, oa, la, ob, oc, ga, gb, gc, wa, wb, wc, batch, seq):
    t, d = x2.shape
    tm = min(TM_PROJ, seq)
    ns = seq // tm
    row = lambda i: (i, 0)
    const = lambda i: (0, 0)
    wide = lambda w: pl.BlockSpec((tm, w), row)
    view = lambda dil: pl.BlockSpec((1, tm // dil, dil * A_WIDTH), lambda i: (i // ns, i % ns, 0))
    views = [view(dil) for dil in DILATIONS]
    slab = pltpu.VMEM((A_WIDTH // LANES, tm, LANES), F32)
    return pl.pallas_call(
        _out_body, grid=(t // tm,),
        in_specs=[wide(d)] + views + views + [wide(B_WIDTH), wide(C_WIDTH),
                  pl.BlockSpec((1, A_WIDTH), const), pl.BlockSpec((1, B_WIDTH), const),
                  pl.BlockSpec((1, C_WIDTH), const),
                  pl.BlockSpec((A_WIDTH, d), const), pl.BlockSpec((B_WIDTH, d), const),
                  pl.BlockSpec((C_WIDTH, d), const)],
        out_specs=wide(d),
        out_shape=jax.ShapeDtypeStruct((t, d), F32),
        scratch_shapes=[slab, slab, slab, slab],
        compiler_params=_cparams(("arbitrary",)), name="out_proj",
    )(x2, oa[0], oa[1], oa[2], la[0], la[1], la[2], ob, oc, ga, gb, gc, wa, wb, wc)


def _ffn_body(x_ref, g_ref, wg_ref, wu_ref, wd_ref, y_ref):
    x = x_ref[...]
    h = _rms(x, g_ref[...]).astype(BF16)
    gate = _dot(h, wg_ref[...])
    up = _dot(h, wu_ref[...])
    y_ref[...] = x + _dot((jax.nn.silu(gate) * up).astype(BF16), wd_ref[...])


def _ffn_call(x2, g, wg, wu, wd):
    t, d = x2.shape
    d_ff = wg.shape[1]
    tm = min(TM_FFN, t)
    resident = lambda shape: pl.BlockSpec(shape, lambda i: (0, 0), pipeline_mode=pl.Buffered(1))
    return pl.pallas_call(
        _ffn_body, grid=(t // tm,),
        in_specs=[pl.BlockSpec((tm, d), lambda i: (i, 0)),
                  pl.BlockSpec((1, d), lambda i: (0, 0)),
                  resident((d, d_ff)), resident((d, d_ff)), resident((d_ff, d))],
        out_specs=pl.BlockSpec((tm, d), lambda i: (i, 0)),
        out_shape=jax.ShapeDtypeStruct((t, d), F32),
        compiler_params=_cparams(("arbitrary",)), name="dense_ffn",
    )(x2, g, wg, wu, wd)


def _to_token_tiles(ref, val):
    rows, d = val.shape
    parts = d // LANES
    for c in range(parts):
        ref[pl.ds(c, rows, stride=parts), :] = val[:, c * LANES:(c + 1) * LANES]


def _from_token_tiles(ref, rows):
    parts = ref.shape[0] // rows
    return jnp.concatenate([ref[pl.ds(c, rows, stride=parts), :] for c in range(parts)], axis=1)


def _router_body(x_ref, g_ref, wr_ref, idx_ref, gate_ref, htile_ref):
    h = _rms(x_ref[...], g_ref[...])
    _to_token_tiles(htile_ref, h)
    logits = lax.dot_general(wr_ref[...], h, (((1,), (1,)), ((), ())),
                             precision=lax.Precision.HIGHEST, preferred_element_type=F32)
    eid = lax.broadcasted_iota(jnp.int32, logits.shape, 0)
    m1 = jnp.max(logits, axis=0, keepdims=True)
    i1 = jnp.min(jnp.where(logits == m1, eid, N_EXPERTS), axis=0, keepdims=True)
    rest = jnp.where(eid == i1, -jnp.inf, logits)
    m2 = jnp.max(rest, axis=0, keepdims=True)
    i2 = jnp.min(jnp.where(rest == m2, eid, N_EXPERTS), axis=0, keepdims=True)
    e2 = jnp.exp(m2 - m1)
    idx_ref[...] = jnp.concatenate([i1, i2], axis=0)
    gate_ref[...] = jnp.concatenate([1.0 / (1.0 + e2), e2 / (1.0 + e2)], axis=0)


def _router_call(x2, g, wr_t):
    t, d = x2.shape
    tm = min(TM_ROUTER, t)
    return pl.pallas_call(
        _router_body, grid=(t // tm,),
        in_specs=[pl.BlockSpec((tm, d), lambda i: (i, 0)),
                  pl.BlockSpec((1, d), lambda i: (0, 0)),
                  pl.BlockSpec((N_EXPERTS, d), lambda i: (0, 0))],
        out_specs=(pl.BlockSpec((TOP_K, tm), lambda i: (0, i)),
                   pl.BlockSpec((TOP_K, tm), lambda i: (0, i)),
                   pl.BlockSpec((tm * d // LANES, LANES), lambda i: (i, 0))),
        out_shape=(jax.ShapeDtypeStruct((TOP_K, t), jnp.int32),
                   jax.ShapeDtypeStruct((TOP_K, t), F32),
                   jax.ShapeDtypeStruct((t * d // LANES, LANES), F32)),
        compiler_params=_cparams(("arbitrary",)), name="router",
    )(x2, g, wr_t)


def _expert_body(bexp_ref, tok_cur, tok_next, dst_prev, dst_cur, h_hbm, wg_ref, wu_ref, wd_ref,
                 y_hbm, xb0, xb1, hbuf, acc_ref, yb0, yb1, gsem, ssem):
    i = pl.program_id(0)
    nb = pl.num_programs(0)
    rows = hbuf.shape[0]
    parts = xb0.shape[0] // rows
    chunk = rows // EXP_STEPS
    tf = wg_ref.shape[2] // EXP_STEPS

    def variant(xcur, xnxt, ycur, yprv, s, compute):
        def tile(ref, first_row):
            if not isinstance(first_row, int):
                first_row = pl.multiple_of(first_row, parts)
            return ref.at[pl.ds(first_row, parts)]

        def row_in(r, src, buf, sem):
            return pltpu.make_async_copy(tile(h_hbm, src), tile(buf, r * parts), sem)

        def row_out(r, dst, buf, sem):
            return pltpu.make_async_copy(tile(buf, r * parts), tile(y_hbm, dst), sem)

        def all_rows(fn):
            def body(r, c):
                fn(r)
                return c
            lax.fori_loop(0, rows, body, 0, unroll=8)

        @pl.when(i == 0)
        def _():
            all_rows(lambda r: row_in(r, tok_cur[0, 0, r], xcur, gsem.at[s]).start())
            yprv[...] = jnp.zeros_like(yprv)

        all_rows(lambda r: row_in(r, 0, xcur, gsem.at[s]).wait())
        if compute:
            hbuf[...] = _from_token_tiles(xcur, rows).astype(BF16)
        for f in range(EXP_STEPS):
            for r in range(f * chunk, (f + 1) * chunk):
                row_in(r, tok_next[0, 0, r], xnxt, gsem.at[1 - s]).start()
                row_out(r, dst_prev[0, 0, r], yprv, ssem.at[1 - s]).start()
            if compute:
                h = hbuf[...]
                cols = slice(f * tf, (f + 1) * tf)
                gate = _dot(h, wg_ref[0, :, cols])
                up = _dot(h, wu_ref[0, :, cols])
                part = _dot((jax.nn.silu(gate) * up).astype(BF16), wd_ref[0, cols, :])
                acc_ref[...] = part if f == 0 else acc_ref[...] + part
        if compute:
            _to_token_tiles(ycur, acc_ref[...])
        else:
            ycur[...] = jnp.zeros_like(ycur)
        all_rows(lambda r: row_out(r, 0, yprv, ssem.at[1 - s]).wait())

        @pl.when(i == nb - 1)
        def _():
            all_rows(lambda r: row_out(r, dst_cur[0, 0, r], ycur, ssem.at[s]).start())
            all_rows(lambda r: row_out(r, 0, ycur, ssem.at[s]).wait())
            all_rows(lambda r: row_in(r, 0, xnxt, gsem.at[1 - s]).wait())

    even = i % 2 == 0
    padding_only = bexp_ref[i] >= N_EXPERTS
    for is_even, bufs in ((True, (xb0, xb1, yb0, yb1, 0)), (False, (xb1, xb0, yb1, yb0, 1))):
        for compute in (True, False):
            @pl.when((even == is_even) & (padding_only != compute))
            def _(bufs=bufs, compute=compute):
                variant(*bufs, compute)


def _expert_call(h_tiles, d, wg, wu, wd, bexp, row_tok, row_dst, n_out_rows):
    parts = d // LANES
    n_blocks = bexp.shape[0]
    d_ff = wg.shape[2]
    assert d_ff % EXP_STEPS == 0 and (d_ff // EXP_STEPS) % LANES == 0
    last = n_blocks - 1
    rows_of = lambda which: pl.BlockSpec((1, 1, TM_EXP), which, memory_space=pltpu.SMEM)
    expert = lambda shape: pl.BlockSpec(shape, lambda i, be: (be[i] % N_EXPERTS, 0, 0),
                                        pipeline_mode=pl.Buffered(1))
    tiles = pltpu.VMEM((TM_EXP * parts, LANES), F32)
    grid_spec = pltpu.PrefetchScalarGridSpec(
        num_scalar_prefetch=1,
        grid=(n_blocks,),
        in_specs=[
            rows_of(lambda i, be: (i, 0, 0)),
            rows_of(lambda i, be: (jnp.minimum(i + 1, last), 0, 0)),
            rows_of(lambda i, be: (jnp.maximum(i - 1, 0), 0, 0)),
            rows_of(lambda i, be: (i, 0, 0)),
            pl.BlockSpec(memory_space=pl.ANY),
            expert((1, d, d_ff)), expert((1, d, d_ff)), expert((1, d_ff, d)),
        ],
        out_specs=pl.BlockSpec(memory_space=pl.ANY),
        scratch_shapes=[tiles, tiles, pltpu.VMEM((TM_EXP, d), BF16), pltpu.VMEM((TM_EXP, d), F32),
                        tiles, tiles, pltpu.SemaphoreType.DMA((2,)), pltpu.SemaphoreType.DMA((2,))],
    )
    return pl.pallas_call(
        _expert_body, grid_spec=grid_spec,
        out_shape=jax.ShapeDtypeStruct((n_out_rows * parts, LANES), F32),
        compiler_params=_cparams(("arbitrary",)), name="experts",
    )(bexp, row_tok, row_tok, row_dst, row_dst, h_tiles, wg, wu, wd)


def _combine_body(x_ref, y0_ref, y1_ref, gate_ref, gf_ref, o_ref, *, final):
    g0 = gate_ref[:, 0:1]
    g1 = gate_ref[:, 1:2]
    rows = x_ref.shape[0]
    y = x_ref[...] + (_from_token_tiles(y0_ref, rows) * g0 + _from_token_tiles(y1_ref, rows) * g1)
    o_ref[...] = _rms(y, gf_ref[...]) if final else y


def _combine_call(x2, y_tiles, gates_t, gfinal, final):
    t, d = x2.shape
    tm = min(TM_ROUTER, t)
    nt = t // tm
    parts = d // LANES
    return pl.pallas_call(
        functools.partial(_combine_body, final=final), grid=(nt,),
        in_specs=[pl.BlockSpec((tm, d), lambda i: (i, 0)),
                  pl.BlockSpec((tm * parts, LANES), lambda i: (i, 0)),
                  pl.BlockSpec((tm * parts, LANES), lambda i: (i + nt, 0)),
                  pl.BlockSpec((tm, TOP_K), lambda i: (i, 0)),
                  pl.BlockSpec((1, d), lambda i: (0, 0))],
        out_specs=pl.BlockSpec((tm, d), lambda i: (i, 0)),
        out_shape=jax.ShapeDtypeStruct((t, d), F32),
        compiler_params=_cparams(("arbitrary",)), name="moe_combine",
    )(x2, y_tiles, y_tiles, gates_t, gfinal)


def _final_norm_body(x_ref, g_ref, o_ref):
    o_ref[...] = _rms(x_ref[...], g_ref[...])


def _final_norm_call(x2, g):
    t, d = x2.shape
    tm = min(TM_ROUTER, t)
    return pl.pallas_call(
        _final_norm_body, grid=(t // tm,),
        in_specs=[pl.BlockSpec((tm, d), lambda i: (i, 0)), pl.BlockSpec((1, d), lambda i: (0, 0))],
        out_specs=pl.BlockSpec((tm, d), lambda i: (i, 0)),
        out_shape=jax.ShapeDtypeStruct((t, d), F32),
        compiler_params=_cparams(("arbitrary",)), name="final_norm",
    )(x2, g)


def _route_plan(idx, t):
    n_assign = TOP_K * t
    flat_e = idx.reshape(-1)
    order = jnp.argsort(flat_e).astype(jnp.int32)
    counts = jnp.sum(flat_e[:, None] == jnp.arange(N_EXPERTS, dtype=jnp.int32)[None, :], axis=0,
                     dtype=jnp.int32)
    padded = ((counts + TM_EXP - 1) // TM_EXP) * TM_EXP
    start = jnp.cumsum(counts) - counts
    cum_padded = jnp.cumsum(padded)
    start_padded = cum_padded - padded
    n_blocks = -(-n_assign // TM_EXP) + N_EXPERTS
    block_start = jnp.arange(n_blocks, dtype=jnp.int32) * TM_EXP
    bexp = jnp.minimum(jnp.sum(block_start[:, None] >= cum_padded[None, :], axis=1),
                       N_EXPERTS - 1).astype(jnp.int32)
    off = block_start[:, None] + jnp.arange(TM_EXP, dtype=jnp.int32)[None, :] - start_padded[bexp][:, None]
    valid = off < counts[bexp][:, None]
    a = order[jnp.clip(start[bexp][:, None] + off, 0, n_assign - 1)]
    bexp = bexp + N_EXPERTS * (~jnp.any(valid, axis=1)).astype(jnp.int32)
    pad_rank = (jnp.cumsum((~valid).reshape(-1).astype(jnp.int32)) - 1).reshape(valid.shape)
    row_tok = jnp.where(valid, a % t, 0).astype(jnp.int32)
    row_dst = jnp.where(valid, a, n_assign + pad_rank).astype(jnp.int32)
    n_out_rows = n_blocks * TM_EXP
    return bexp, row_tok.reshape(n_blocks, 1, TM_EXP), row_dst.reshape(n_blocks, 1, TM_EXP), n_out_rows


def _rope_tables(seq):
    pos = jnp.arange(seq, dtype=F32)[:, None]
    half_a = A_ROT_DIM // 2
    ang_a = pos * (ROPE_THETA ** (-jnp.arange(half_a, dtype=F32) / half_a))[None, :]
    one = jnp.ones((seq, HEAD_DIM - A_ROT_DIM), F32)
    cos_head = jnp.concatenate([jnp.cos(ang_a), jnp.cos(ang_a), one], axis=1)
    sin_head = jnp.concatenate([jnp.sin(ang_a), jnp.sin(ang_a), 0 * one], axis=1)
    cosa = jnp.tile(cos_head, (1, A_HEADS))
    sina = jnp.tile(sin_head, (1, A_HEADS))
    half_b = B_ROPE // 2
    ang_b = pos * (ROPE_THETA ** (-jnp.arange(half_b, dtype=F32) / half_b))[None, :]
    zl = jnp.zeros((seq, B_NOPE), F32)
    zr = jnp.zeros((seq, HEAD_PAD - B_NOPE - B_ROPE), F32)
    cosk = jnp.concatenate([zl, jnp.cos(ang_b), jnp.cos(ang_b), zr], axis=1)
    sink = jnp.concatenate([zl, jnp.sin(ang_b), jnp.sin(ang_b), zr], axis=1)
    return cosa, sina, cosk, sink, jnp.cos(ang_b).T, jnp.sin(ang_b).T


def _rotate_half_matrix():
    r = np.zeros((A_WIDTH, A_WIDTH), np.float32)
    half = A_ROT_DIM // 2
    for hd in range(A_HEADS):
        for i in range(half):
            r[hd * HEAD_DIM + half + i, hd * HEAD_DIM + i] = -1.0
            r[hd * HEAD_DIM + i, hd * HEAD_DIM + half + i] = 1.0
    return jnp.asarray(r, BF16)


def _dft_constants(seq):
    lo_n = 64
    hi_n = seq // lo_n
    k = jnp.arange(seq, dtype=jnp.int32)[:, None]
    ang_hi = ((k * jnp.arange(hi_n // 2, dtype=jnp.int32)[None, :]) % hi_n).astype(F32) * (2.0 * np.pi / hi_n)
    ang_lo = ((k * jnp.arange(lo_n, dtype=jnp.int32)[None, :]) % seq).astype(F32) * (2.0 * np.pi / seq)
    ch, sh = jnp.cos(ang_hi)[:, :, None], jnp.sin(ang_hi)[:, :, None]
    cl, sl = jnp.cos(ang_lo)[:, None, :], jnp.sin(ang_lo)[:, None, :]
    scale = seq ** -0.5
    cmat = ((ch * cl - sh * sl) * scale).astype(BF16).reshape(seq, seq // 2)
    nsmat = ((sh * cl + ch * sl) * -scale).astype(BF16).reshape(seq, seq // 2)
    c = np.arange(C_GROUP_DIM)
    angc = 2.0 * np.pi * ((c[:, None] * c[None, :]) % C_GROUP_DIM) / C_GROUP_DIM
    eye = np.eye(C_GROUPS)
    cs = np.concatenate([np.kron(eye, np.cos(angc)), np.kron(eye, np.sin(angc))], axis=1)
    return cmat, nsmat, jnp.asarray(cs * C_GROUP_DIM ** -0.5, BF16)


def _layer_weights(w_in, w_uq, w_ukv, w_out):
    d = w_in.shape[0]
    c_kpe = 3 * A_WIDTH + Q_LORA + KV_LORA
    w_kpe = w_in[:, c_kpe:c_kpe + B_ROPE]
    half = B_ROPE // 2
    w_kpe_rot = jnp.concatenate([-w_kpe[:, half:], w_kpe[:, :half]], axis=1)
    zl = jnp.zeros((d, B_NOPE), w_in.dtype)
    zr = jnp.zeros((d, HEAD_PAD - B_NOPE - B_ROPE), w_in.dtype)
    wm = jnp.concatenate([w_in[:, :c_kpe], zl, w_kpe, zr, zl, w_kpe_rot, zr,
                          w_in[:, c_kpe + B_ROPE:]], axis=1).astype(BF16)
    uq = w_uq.reshape(Q_LORA, B_HEADS, B_NOPE + B_ROPE)
    uq = jnp.pad(uq, ((0, 0), (0, 0), (0, HEAD_PAD - B_NOPE - B_ROPE)))
    wuq_t = uq.reshape(Q_LORA, B_HEADS * HEAD_PAD).T.astype(BF16)
    ukv = w_ukv.reshape(KV_LORA, B_HEADS, B_NOPE + B_V)
    wkn = jnp.pad(ukv[:, :, :B_NOPE], ((0, 0), (0, 0), (0, HEAD_PAD - B_NOPE)))
    wkn = wkn.reshape(KV_LORA, B_HEADS * HEAD_PAD).astype(BF16)
    wv_t = ukv[:, :, B_NOPE:].reshape(KV_LORA, B_WIDTH).T.astype(BF16)
    wo = w_out.astype(BF16)
    return wm, wuq_t, wkn, wv_t, wo[:A_WIDTH], wo[A_WIDTH:A_WIDTH + B_WIDTH], wo[A_WIDTH + B_WIDTH:]


def kernel(x, attn_norm, w_in, q_norm, w_uq, kv_norm, w_ukv, mix_gain, w_out, ffn_norm,
           w_ffn_gate, w_ffn_up, w_ffn_down, w_router, w_exp_gate, w_exp_up, w_exp_down,
           final_norm):
    batch, seq, d = x.shape
    depth = w_in.shape[0]
    t = batch * seq
    assert seq % (max(DILATIONS) * 2 * WIN_SUB) == 0
    tabs = _rope_tables(seq)
    rot = _rotate_half_matrix()
    cmat, nsmat, cs = _dft_constants(seq)
    row = lambda v: v.reshape(1, -1)

    x2 = x.reshape(t, d)
    for l in range(depth):
        wm, wuq_t, wkn, wv_t, wa, wb, wc = _layer_weights(w_in[l], w_uq[l], w_ukv[l], w_out[l])
        qkv_views, (qbt, kb, vbt, xcs, xcs_flip) = _proj_call(
            x2, row(attn_norm[l]), wm, rot, tabs, row(q_norm[l]), wuq_t, row(kv_norm[l]), wkn, wv_t,
            cs, batch, seq)
        win = [_win_call(*(a.reshape(batch, seq // dil, dil * A_WIDTH) for a in qkv), batch, seq, dil)
               for dil, qkv in zip(DILATIONS, qkv_views)]
        ob = _mla_call(qbt, kb, vbt, batch, seq)
        oc = _dft_call(cmat, nsmat, xcs, xcs_flip, batch, seq)
        g = mix_gain[l]
        x2 = _out_call(x2, [w[0] for w in win], [w[1] for w in win], ob, oc,
                       row(g[:A_WIDTH]), row(g[A_WIDTH:A_WIDTH + B_WIDTH]), row(g[A_WIDTH + B_WIDTH:]),
                       wa, wb, wc, batch, seq)
        i = l // 2
        if l % 2 == 0:
            x2 = _ffn_call(x2, row(ffn_norm[l]), w_ffn_gate[i].astype(BF16), w_ffn_up[i].astype(BF16),
                           w_ffn_down[i].astype(BF16))
            if l == depth - 1:
                x2 = _final_norm_call(x2, row(final_norm))
        else:
            idx, gates, h_tiles = _router_call(x2, row(ffn_norm[l]), w_router[i].T)
            bexp, row_tok, row_dst, n_out_rows = _route_plan(idx, t)
            parts = d // LANES
            y_tiles = _expert_call(h_tiles, d, w_exp_gate[i].astype(BF16), w_exp_up[i].astype(BF16),
                                   w_exp_down[i].astype(BF16), bexp, row_tok * parts, row_dst * parts,
                                   n_out_rows)
            x2 = _combine_call(x2, y_tiles, gates.T, row(final_norm), final=(l == depth - 1))
    return x2.reshape(batch, seq, d)
```

```python
import functools
import math

import jax
import jax.numpy as jnp
import numpy as np
from jax import lax
from jax.experimental import pallas as pl
from jax.experimental.pallas import tpu as pltpu

F32 = jnp.float32
BF16 = jnp.bfloat16

HEAD_DIM = 64
A_HEADS = 6
A_WIDTH = A_HEADS * HEAD_DIM
A_ROT_DIM = HEAD_DIM // 4
DILATIONS = (1, 4, 16)
WINDOW_RADIUS = 64
B_HEADS = 6
B_NOPE = 64
B_ROPE = 32
B_V = 64
B_WIDTH = B_HEADS * B_V
Q_LORA = 384
KV_LORA = 128
C_GROUPS = 4
C_GROUP_DIM = 64
C_WIDTH = C_GROUPS * C_GROUP_DIM
ROPE_THETA = 500000.0
N_EXPERTS = 8
TOP_K = 2
RMS_EPS = 1e-6
NEG_INF = -1e30

LANES = 128
BF16_SUBLANES = 16
HEAD_PAD = 128
VMEM_LIMIT_BYTES = 56 * 1024 * 1024

TM_PROJ = 512
TQ_WIN = 1024
WIN_SUB = 128
TQ_MLA = 512
TK_MLA = 256
TM_DFT = 1024
TK_DFT = 512
TM_FFN = 512
TM_ROUTER = 512
EXP_STEPS = 7
TM_EXP = 128 * EXP_STEPS


def _cparams(sem):
    return pltpu.CompilerParams(dimension_semantics=sem, vmem_limit_bytes=VMEM_LIMIT_BYTES)


def _rms(x, g):
    return x * lax.rsqrt(jnp.mean(x * x, axis=-1, keepdims=True) + RMS_EPS) * g


def _dot(a, b):
    return jnp.dot(a, b, preferred_element_type=F32)


def _dot_nt(a, b):
    return lax.dot_general(a, b, (((1,), (1,)), ((), ())), preferred_element_type=F32)


def _split_residues(scr, val, nat_ref, view_refs):
    tm = val.shape[0]
    nat_ref[...] = val.astype(BF16)
    for c in range(A_WIDTH // LANES):
        scr[c] = val[:, c * LANES:(c + 1) * LANES]
    for dil, ref in zip(DILATIONS[1:], view_refs):
        for r in range(dil):
            for c in range(A_WIDTH // LANES):
                col = r * A_WIDTH + c * LANES
                ref[0, :, col:col + LANES] = scr[c, pl.ds(r, tm // dil, stride=dil), :].astype(BF16)


def _proj_body(x_ref, g_ref, wm_ref, rot_ref, cosa_ref, sina_ref, cosk_ref, sink_ref,
               cosq_ref, sinq_ref, qn_ref, wuq_ref, kvn_ref, wkn_ref, wv_ref, cs_ref, flip_ref,
               qa_ref, qa4_ref, qa16_ref, ka_ref, ka4_ref, ka16_ref, va_ref, va4_ref, va16_ref,
               qbt_ref, kb_ref, vbt_ref, xcs_ref, xcs_flip_ref, scr):
    h = _rms(x_ref[...], g_ref[...]).astype(BF16)
    z = _dot(h, wm_ref[...])
    o = 0
    qa = z[:, o:o + A_WIDTH]; o += A_WIDTH
    ka = z[:, o:o + A_WIDTH]; o += A_WIDTH
    va = z[:, o:o + A_WIDTH]; o += A_WIDTH
    cq = z[:, o:o + Q_LORA]; o += Q_LORA
    ckv = z[:, o:o + KV_LORA]; o += KV_LORA
    kpe = z[:, o:o + HEAD_PAD]; o += HEAD_PAD
    kpe_rot = z[:, o:o + HEAD_PAD]; o += HEAD_PAD
    fc = z[:, o:o + C_WIDTH]

    cosa = cosa_ref[...]
    sina = sina_ref[...]
    rot = rot_ref[...]
    _split_residues(scr, qa * cosa + _dot(qa.astype(BF16), rot) * sina, qa_ref, (qa4_ref, qa16_ref))
    _split_residues(scr, ka * cosa + _dot(ka.astype(BF16), rot) * sina, ka_ref, (ka4_ref, ka16_ref))
    _split_residues(scr, va, va_ref, (va4_ref, va16_ref))

    cqn = _rms(cq, qn_ref[...]).astype(BF16)
    qbt = _dot_nt(wuq_ref[...], cqn) * ((B_NOPE + B_ROPE) ** -0.5 * math.log2(math.e))
    cosq = cosq_ref[...]
    sinq = sinq_ref[...]
    half = B_ROPE // 2
    for hd in range(B_HEADS):
        r0 = hd * HEAD_PAD
        x1 = qbt[r0 + B_NOPE:r0 + B_NOPE + half]
        x2 = qbt[r0 + B_NOPE + half:r0 + B_NOPE + B_ROPE]
        qbt_ref[0, r0:r0 + B_NOPE, :] = qbt[r0:r0 + B_NOPE].astype(BF16)
        qbt_ref[0, r0 + B_NOPE:r0 + B_NOPE + half, :] = (x1 * cosq - x2 * sinq).astype(BF16)
        qbt_ref[0, r0 + B_NOPE + half:r0 + B_NOPE + B_ROPE, :] = (x2 * cosq + x1 * sinq).astype(BF16)
        qbt_ref[0, r0 + B_NOPE + B_ROPE:r0 + HEAD_PAD, :] = jnp.zeros(
            (HEAD_PAD - B_NOPE - B_ROPE, qbt.shape[1]), BF16)

    ckvn = _rms(ckv, kvn_ref[...]).astype(BF16)
    kn = _dot(ckvn, wkn_ref[...])
    kpe_r = kpe * cosk_ref[...] + kpe_rot * sink_ref[...]
    for hd in range(B_HEADS):
        kb_ref[0, hd] = (kn[:, hd * HEAD_PAD:(hd + 1) * HEAD_PAD] + kpe_r).astype(BF16)
    vbt_ref[0] = _dot_nt(wv_ref[...], ckvn).astype(BF16)

    xcs = _dot(fc.astype(BF16), cs_ref[...]).astype(BF16)
    xcs_ref[0] = xcs
    xcs_flip_ref[0] = _dot(flip_ref[...], xcs).astype(BF16)


def _proj_call(x2, g, wm, rot, tabs, qn, wuq, kvn, wkn, wv, cs, batch, seq):
    t, d = x2.shape
    tm = min(TM_PROJ, seq)
    ns = seq // tm
    nm = wm.shape[1]
    cosa, sina, cosk, sink, cosq, sinq = tabs
    row = lambda i: (i, 0)
    pos = lambda i: (i % ns, 0)
    post = lambda i: (0, i % ns)
    const = lambda i: (0, 0)
    in_specs = [
        pl.BlockSpec((tm, d), row),
        pl.BlockSpec((1, d), const),
        pl.BlockSpec((d, nm), const),
        pl.BlockSpec((A_WIDTH, A_WIDTH), const),
        pl.BlockSpec((tm, A_WIDTH), pos),
        pl.BlockSpec((tm, A_WIDTH), pos),
        pl.BlockSpec((tm, HEAD_PAD), pos),
        pl.BlockSpec((tm, HEAD_PAD), pos),
        pl.BlockSpec((B_ROPE // 2, tm), post),
        pl.BlockSpec((B_ROPE // 2, tm), post),
        pl.BlockSpec((1, Q_LORA), const),
        pl.BlockSpec((B_HEADS * HEAD_PAD, Q_LORA), const),
        pl.BlockSpec((1, KV_LORA), const),
        pl.BlockSpec((KV_LORA, B_HEADS * HEAD_PAD), const),
        pl.BlockSpec((B_WIDTH, KV_LORA), const),
        pl.BlockSpec((C_WIDTH, 2 * C_WIDTH), const),
        pl.BlockSpec((tm, tm), const),
    ]
    bpos = lambda i: (i // ns, i % ns, 0)
    a_shapes, a_specs = [], []
    for _ in range(3):
        a_shapes.append(jax.ShapeDtypeStruct((t, A_WIDTH), BF16))
        a_specs.append(pl.BlockSpec((tm, A_WIDTH), row))
        for dil in DILATIONS[1:]:
            a_shapes.append(jax.ShapeDtypeStruct((batch, seq // dil, dil * A_WIDTH), BF16))
            a_specs.append(pl.BlockSpec((1, tm // dil, dil * A_WIDTH), bpos))
    out_shape = (
        *a_shapes,
        jax.ShapeDtypeStruct((batch, B_HEADS * HEAD_PAD, seq), BF16),
        jax.ShapeDtypeStruct((batch, B_HEADS, seq, HEAD_PAD), BF16),
        jax.ShapeDtypeStruct((batch, B_WIDTH, seq), BF16),
        jax.ShapeDtypeStruct((batch, seq, 2 * C_WIDTH), BF16),
        jax.ShapeDtypeStruct((batch, seq, 2 * C_WIDTH), BF16),
    )
    out_specs = (
        *a_specs,
        pl.BlockSpec((1, B_HEADS * HEAD_PAD, tm), lambda i: (i // ns, 0, i % ns)),
        pl.BlockSpec((1, B_HEADS, tm, HEAD_PAD), lambda i: (i // ns, 0, i % ns, 0)),
        pl.BlockSpec((1, B_WIDTH, tm), lambda i: (i // ns, 0, i % ns)),
        pl.BlockSpec((1, tm, 2 * C_WIDTH), bpos),
        pl.BlockSpec((1, tm, 2 * C_WIDTH), lambda i: (i // ns, ns - 1 - i % ns, 0)),
    )
    flip = jnp.asarray(np.eye(tm, dtype=np.float32)[::-1], BF16)
    outs = pl.pallas_call(
        _proj_body, grid=(t // tm,), in_specs=in_specs, out_specs=out_specs, out_shape=out_shape,
        scratch_shapes=[pltpu.VMEM((A_WIDTH // LANES, tm, LANES), F32)],
        compiler_params=_cparams(("arbitrary",)), name="proj",
    )(x2, g, wm, rot, cosa, sina, cosk, sink, cosq, sinq, qn, wuq, kvn, wkn, wv, cs, flip)
    nd = len(DILATIONS)
    qkv_views = [tuple(outs[a * nd + j] for a in range(3)) for j in range(nd)]
    return qkv_views, outs[3 * nd:]


def _win_body(q_ref, kp_ref, km_ref, kn_ref, vp_ref, vm_ref, vn_ref, o_ref, l_ref, *, seg_len, tq):
    j = pl.program_id(2)
    span = WIN_SUB + 2 * WINDOW_RADIUS
    lane = lax.broadcasted_iota(jnp.int32, (1, LANES), 1)
    lo_half = lane < HEAD_DIM
    rel = (lax.broadcasted_iota(jnp.int32, (WIN_SUB, span), 1) - WINDOW_RADIUS
           - lax.broadcasted_iota(jnp.int32, (WIN_SUB, span), 0))
    band = jnp.abs(rel) <= WINDOW_RADIUS
    kcol = lax.broadcasted_iota(jnp.int32, (1, span), 1)
    n_sub = tq // WIN_SUB
    biases = []
    for sb in range(n_sub):
        kpos = j * tq + sb * WIN_SUB - WINDOW_RADIUS + kcol
        biases.append(jnp.where(band & (kpos >= 0) & (kpos < seg_len), 0.0, NEG_INF))
    for pair in range(A_HEADS // 2):
        cs = slice(pair * LANES, (pair + 1) * LANES)
        q2 = q_ref[0, :, cs] * (HEAD_DIM ** -0.5 * math.log2(math.e))
        qh = (jnp.where(lo_half, q2, 0).astype(BF16), jnp.where(lo_half, 0, q2).astype(BF16))
        kcat = jnp.concatenate([kp_ref[0, :, cs], km_ref[0, :, cs], kn_ref[0, :, cs]], axis=0)
        vcat = jnp.concatenate([vp_ref[0, :, cs], vm_ref[0, :, cs], vn_ref[0, :, cs]], axis=0)
        for sb in range(n_sub):
            r0 = sb * WIN_SUB
            ks = kcat[r0:r0 + span]
            vs = vcat[r0:r0 + span]
            outs, lses = [], []
            for hh in range(2):
                s = _dot_nt(qh[hh][r0:r0 + WIN_SUB], ks) + biases[sb]
                m = jnp.max(s, axis=-1, keepdims=True)
                p = jnp.exp2(s - m)
                den = jnp.sum(p, axis=-1, keepdims=True)
                outs.append(_dot(p.astype(BF16), vs) / den)
                lses.append(m * math.log(2.0) + jnp.log(den))
            o_ref[0, r0:r0 + WIN_SUB, cs] = jnp.where(lo_half, outs[0], outs[1]).astype(BF16)
            l_ref[0, r0:r0 + WIN_SUB, cs] = jnp.where(lo_half, lses[0], lses[1])


def _win_call(q, k, v, batch, seq, dil):
    seg = seq // dil
    tq = min(TQ_WIN, seg)
    nb = tq // WINDOW_RADIUS
    last = seg // WINDOW_RADIUS - 1
    main = pl.BlockSpec((1, tq, A_WIDTH), lambda b, r, j: (b, j, r))
    prev = pl.BlockSpec((1, WINDOW_RADIUS, A_WIDTH), lambda b, r, j: (b, jnp.maximum(j * nb - 1, 0), r))
    nxt = pl.BlockSpec((1, WINDOW_RADIUS, A_WIDTH), lambda b, r, j: (b, jnp.minimum((j + 1) * nb, last), r))
    return pl.pallas_call(
        functools.partial(_win_body, seg_len=seg, tq=tq),
        grid=(batch, dil, seg // tq),
        in_specs=[main, prev, main, nxt, prev, main, nxt],
        out_specs=(main, main),
        out_shape=(jax.ShapeDtypeStruct((batch, seg, dil * A_WIDTH), BF16),
                   jax.ShapeDtypeStruct((batch, seg, dil * A_WIDTH), F32)),
        compiler_params=_cparams(("arbitrary", "arbitrary", "arbitrary")), name=f"win_d{dil}",
    )(q, k, k, k, v, v, v)


def _mla_body(qt_ref, k_ref, vt_ref, o_ref, sa, sb, pa, pb, acc_ref, *, seq, tk):
    tq = qt_ref.shape[2]
    nk = seq // tk
    qts = [qt_ref[0, h * HEAD_PAD:(h + 1) * HEAD_PAD, :] for h in range(B_HEADS)]
    ones = jnp.ones((BF16_SUBLANES, tk), BF16)

    def scores(h, kt, dst):
        start = kt * tk if isinstance(kt, int) else pl.multiple_of(kt * tk, tk)
        dst[h] = _dot(k_ref[0, h, pl.ds(start, tk), :], qts[h])

    def values(h, kt, src, alpha):
        start = kt * tk if isinstance(kt, int) else pl.multiple_of(kt * tk, tk)
        vt = jnp.concatenate([vt_ref[0, h * B_V:(h + 1) * B_V, pl.ds(start, tk)], ones], axis=0)
        acc_ref[h] = alpha * acc_ref[h] + _dot(vt, src[h])

    def softmax(h, src, dst, m):
        st = src[h]
        m_new = jnp.maximum(m, jnp.max(st, axis=0, keepdims=True))
        dst[h] = jnp.exp2(st - m_new).astype(BF16)
        return m_new, jnp.exp2(m - m_new)

    for h in range(B_HEADS):
        scores(h, 0, sa)
        acc_ref[h] = jnp.zeros((B_V + BF16_SUBLANES, tq), F32)
    pb[...] = jnp.zeros_like(pb)

    def step(i, carry):
        out = []
        for h in range(B_HEADS):
            m, alpha_prev = carry[h]
            t0 = 2 * i
            scores(h, t0 + 1, sb)
            values(h, jnp.maximum(t0 - 1, 0), pb, alpha_prev)
            m, alpha0 = softmax(h, sa, pa, m)
            scores(h, jnp.minimum(t0 + 2, nk - 1), sa)
            values(h, t0, pa, alpha0)
            m, alpha1 = softmax(h, sb, pb, m)
            out.append((m, alpha1))
        return tuple(out)

    init = tuple((jnp.full((1, tq), NEG_INF, F32), jnp.ones((1, tq), F32)) for _ in range(B_HEADS))
    res = lax.fori_loop(0, nk // 2, step, init)
    outs = []
    for h in range(B_HEADS):
        values(h, nk - 1, pb, res[h][1])
        acc = acc_ref[h]
        outs.append(acc[:B_V] / acc[B_V:B_V + 1])
    o_ref[...] = jnp.concatenate(outs, axis=0).T.astype(BF16)


def _mla_call(qbt, kb, vbt, batch, seq):
    tq = min(TQ_MLA, seq)
    tk = min(TK_MLA, seq)
    nq = seq // tq
    assert (seq // tk) % 2 == 0
    return pl.pallas_call(
        functools.partial(_mla_body, seq=seq, tk=tk),
        grid=(batch, nq),
        in_specs=[
            pl.BlockSpec((1, B_HEADS * HEAD_PAD, tq), lambda b, qi: (b, 0, qi)),
            pl.BlockSpec((1, B_HEADS, seq, HEAD_PAD), lambda b, qi: (b, 0, 0, 0)),
            pl.BlockSpec((1, B_WIDTH, seq), lambda b, qi: (b, 0, 0)),
        ],
        out_specs=pl.BlockSpec((tq, B_WIDTH), lambda b, qi: (b * nq + qi, 0)),
        out_shape=jax.ShapeDtypeStruct((batch * seq, B_WIDTH), BF16),
        scratch_shapes=[pltpu.VMEM((B_HEADS, tk, tq), F32), pltpu.VMEM((B_HEADS, tk, tq), F32),
                        pltpu.VMEM((B_HEADS, tk, tq), BF16), pltpu.VMEM((B_HEADS, tk, tq), BF16),
                        pltpu.VMEM((B_HEADS, B_V + BF16_SUBLANES, tq), F32)],
        compiler_params=_cparams(("arbitrary", "arbitrary")), name="mla_attn",
    )(qbt, kb, vbt)


def _dft_body(c_ref, ns_ref, x_ref, xf_ref, xfp_ref, xh_ref, o_ref, acc_ref, *, seq):
    k = pl.program_id(1)
    tm = c_ref.shape[0]
    tk = x_ref.shape[1]

    @pl.when(k == 0)
    def _():
        row = lax.broadcasted_iota(jnp.int32, (tm, 1), 0)
        sign = (1 - 2 * (row & 1)).astype(F32) * (seq ** -0.5)
        for b in range(x_ref.shape[0]):
            acc_ref[b] = sign * xh_ref[b, 0:1, 0:C_WIDTH].astype(F32)

    c = c_ref[...]
    ns = ns_ref[...]
    first_row = lax.broadcasted_iota(jnp.int32, (tk, 1), 0) == 0
    has_prev = (k > 0).astype(F32)
    for b in range(x_ref.shape[0]):
        x = x_ref[b].astype(F32)
        carry_in = xfp_ref[b, 7:8, :].astype(F32) * has_prev
        xr = jnp.where(first_row, carry_in, pltpu.roll(xf_ref[b].astype(F32), 1, axis=0))
        sym = (x[:, 0:C_WIDTH] + xr[:, 0:C_WIDTH]).astype(BF16)
        anti = (x[:, C_WIDTH:2 * C_WIDTH] - xr[:, C_WIDTH:2 * C_WIDTH]).astype(BF16)
        acc_ref[b] += _dot(c, sym) + _dot(ns, anti)

    @pl.when(k == pl.num_programs(1) - 1)
    def _():
        o_ref[...] = acc_ref[...].astype(o_ref.dtype)


def _dft_call(cmat, nsmat, x3, x3_flip, batch, seq):
    half = seq // 2
    tm = min(TM_DFT, seq)
    tk = min(TK_DFT, half)
    rows8 = tk // 8
    blk = lambda rows, index: pl.BlockSpec((batch, rows, 2 * C_WIDTH), index)
    y = pl.pallas_call(
        functools.partial(_dft_body, seq=seq), grid=(seq // tm, half // tk),
        in_specs=[pl.BlockSpec((tm, tk), lambda i, k: (i, k)),
                  pl.BlockSpec((tm, tk), lambda i, k: (i, k)),
                  blk(tk, lambda i, k: (0, k, 0)),
                  blk(tk, lambda i, k: (0, k, 0)),
                  blk(8, lambda i, k: (0, jnp.maximum(k * rows8 - 1, 0), 0)),
                  blk(8, lambda i, k: (0, half // 8, 0))],
        out_specs=pl.BlockSpec((batch, tm, C_WIDTH), lambda i, k: (0, i, 0)),
        out_shape=jax.ShapeDtypeStruct((batch, seq, C_WIDTH), BF16),
        scratch_shapes=[pltpu.VMEM((batch, tm, C_WIDTH), F32)],
        compiler_params=_cparams(("arbitrary", "arbitrary")), name="seq_dft",
    )(cmat, nsmat, x3, x3_flip, x3_flip, x3)
    return y.reshape(batch * seq, C_WIDTH)


def _merge_residues(scr, ref, dil):
    tm = scr.shape[1]
    for r in range(dil):
        for c in range(A_WIDTH // LANES):
            col = r * A_WIDTH + c * LANES
            scr[c, pl.ds(r, tm // dil, stride=dil), :] = ref[0, :, col:col + LANES].astype(F32)
    return jnp.concatenate([scr[c] for c in range(A_WIDTH // LANES)], axis=1)


def _out_body(x_ref, o1_ref, o4_ref, o16_ref, l1_ref, l4_ref, l16_ref, ob_ref, oc_ref,
              ga_ref, gb_ref, gc_ref, wa_ref, wb_ref, wc_ref, y_ref, so4, sl4, so16, sl16):
    l1 = l1_ref[0]
    l4 = _merge_residues(sl4, l4_ref, DILATIONS[1])
    l16 = _merge_residues(sl16, l16_ref, DILATIONS[2])
    o4 = _merge_residues(so4, o4_ref, DILATIONS[1])
    o16 = _merge_residues(so16, o16_ref, DILATIONS[2])
    m = jnp.maximum(jnp.maximum(l1, l4), l16)
    e1, e4, e16 = jnp.exp(l1 - m), jnp.exp(l4 - m), jnp.exp(l16 - m)
    oa = (o1_ref[0].astype(F32) * e1 + o4 * e4 + o16 * e16) / (e1 + e4 + e16)
    na = _rms(oa, ga_ref[...]).astype(BF16)
    nb = _rms(ob_ref[...].astype(F32), gb_ref[...]).astype(BF16)
    nc = _rms(oc_ref[...].astype(F32), gc_ref[...]).astype(BF16)
    y_ref[...] = (x_ref[...] + _dot(na, wa_ref[...]) + _dot(nb, wb_ref[...]) + _dot(nc, wc_ref[...]))


def _out_call(x2, oa, la, ob, oc, ga, gb, gc, wa, wb, wc, batch, seq):
    t, d = x2.shape
    tm = min(TM_PROJ, seq)
    ns = seq // tm
    row = lambda i: (i, 0)
    const = lambda i: (0, 0)
    wide = lambda w: pl.BlockSpec((tm, w), row)
    view = lambda dil: pl.BlockSpec((1, tm // dil, dil * A_WIDTH), lambda i: (i // ns, i % ns, 0))
    views = [view(dil) for dil in DILATIONS]
    slab = pltpu.VMEM((A_WIDTH // LANES, tm, LANES), F32)
    return pl.pallas_call(
        _out_body, grid=(t // tm,),
        in_specs=[wide(d)] + views + views + [wide(B_WIDTH), wide(C_WIDTH),
                  pl.BlockSpec((1, A_WIDTH), const), pl.BlockSpec((1, B_WIDTH), const),
                  pl.BlockSpec((1, C_WIDTH), const),
                  pl.BlockSpec((A_WIDTH, d), const), pl.BlockSpec((B_WIDTH, d), const),
                  pl.BlockSpec((C_WIDTH, d), const)],
        out_specs=wide(d),
        out_shape=jax.ShapeDtypeStruct((t, d), F32),
        scratch_shapes=[slab, slab, slab, slab],
        compiler_params=_cparams(("arbitrary",)), name="out_proj",
    )(x2, oa[0], oa[1], oa[2], la[0], la[1], la[2], ob, oc, ga, gb, gc, wa, wb, wc)


def _ffn_body(x_ref, g_ref, wg_ref, wu_ref, wd_ref, y_ref):
    x = x_ref[...]
    h = _rms(x, g_ref[...]).astype(BF16)
    gate = _dot(h, wg_ref[...])
    up = _dot(h, wu_ref[...])
    y_ref[...] = x + _dot((jax.nn.silu(gate) * up).astype(BF16), wd_ref[...])


def _ffn_call(x2, g, wg, wu, wd):
    t, d = x2.shape
    d_ff = wg.shape[1]
    tm = min(TM_FFN, t)
    resident = lambda shape: pl.BlockSpec(shape, lambda i: (0, 0), pipeline_mode=pl.Buffered(1))
    return pl.pallas_call(
        _ffn_body, grid=(t // tm,),
        in_specs=[pl.BlockSpec((tm, d), lambda i: (i, 0)),
                  pl.BlockSpec((1, d), lambda i: (0, 0)),
                  resident((d, d_ff)), resident((d, d_ff)), resident((d_ff, d))],
        out_specs=pl.BlockSpec((tm, d), lambda i: (i, 0)),
        out_shape=jax.ShapeDtypeStruct((t, d), F32),
        compiler_params=_cparams(("arbitrary",)), name="dense_ffn",
    )(x2, g, wg, wu, wd)


def _to_token_tiles(ref, val):
    rows, d = val.shape
    parts = d // LANES
    for c in range(parts):
        ref[pl.ds(c, rows, stride=parts), :] = val[:, c * LANES:(c + 1) * LANES]


def _from_token_tiles(ref, rows):
    parts = ref.shape[0] // rows
    return jnp.concatenate([ref[pl.ds(c, rows, stride=parts), :] for c in range(parts)], axis=1)


def _router_body(x_ref, g_ref, wr_ref, idx_ref, gate_ref, htile_ref):
    h = _rms(x_ref[...], g_ref[...])
    _to_token_tiles(htile_ref, h)
    logits = lax.dot_general(wr_ref[...], h, (((1,), (1,)), ((), ())),
                             precision=lax.Precision.HIGHEST, preferred_element_type=F32)
    eid = lax.broadcasted_iota(jnp.int32, logits.shape, 0)
    m1 = jnp.max(logits, axis=0, keepdims=True)
    i1 = jnp.min(jnp.where(logits == m1, eid, N_EXPERTS), axis=0, keepdims=True)
    rest = jnp.where(eid == i1, -jnp.inf, logits)
    m2 = jnp.max(rest, axis=0, keepdims=True)
    i2 = jnp.min(jnp.where(rest == m2, eid, N_EXPERTS), axis=0, keepdims=True)
    e2 = jnp.exp(m2 - m1)
    idx_ref[...] = jnp.concatenate([i1, i2], axis=0)
    gate_ref[...] = jnp.concatenate([1.0 / (1.0 + e2), e2 / (1.0 + e2)], axis=0)


def _router_call(x2, g, wr_t):
    t, d = x2.shape
    tm = min(TM_ROUTER, t)
    return pl.pallas_call(
        _router_body, grid=(t // tm,),
        in_specs=[pl.BlockSpec((tm, d), lambda i: (i, 0)),
                  pl.BlockSpec((1, d), lambda i: (0, 0)),
                  pl.BlockSpec((N_EXPERTS, d), lambda i: (0, 0))],
        out_specs=(pl.BlockSpec((TOP_K, tm), lambda i: (0, i)),
                   pl.BlockSpec((TOP_K, tm), lambda i: (0, i)),
                   pl.BlockSpec((tm * d // LANES, LANES), lambda i: (i, 0))),
        out_shape=(jax.ShapeDtypeStruct((TOP_K, t), jnp.int32),
                   jax.ShapeDtypeStruct((TOP_K, t), F32),
                   jax.ShapeDtypeStruct((t * d // LANES, LANES), F32)),
        compiler_params=_cparams(("arbitrary",)), name="router",
    )(x2, g, wr_t)


def _expert_body(bexp_ref, tok_cur, tok_next, dst_prev, dst_cur, h_hbm, wg_ref, wu_ref, wd_ref,
                 y_hbm, xb0, xb1, hbuf, acc_ref, yb0, yb1, gsem, ssem):
    i = pl.program_id(0)
    f = pl.program_id(1)
    nb = pl.num_programs(0)
    nf = pl.num_programs(1)
    rows = hbuf.shape[0]
    parts = xb0.shape[0] // rows
    chunk = rows // EXP_STEPS

    def variant(xcur, xnxt, ycur, yprv, s, compute):
        def tile(ref, first_row):
            if not isinstance(first_row, int):
                first_row = pl.multiple_of(first_row, parts)
            return ref.at[pl.ds(first_row, parts)]

        def row_in(r, src, buf, sem):
            return pltpu.make_async_copy(tile(h_hbm, src), tile(buf, r * parts), sem)

        def row_out(r, dst, buf, sem):
            return pltpu.make_async_copy(tile(buf, r * parts), tile(y_hbm, dst), sem)

        def all_rows(fn):
            def body(r, c):
                fn(r)
                return c
            lax.fori_loop(0, rows, body, 0, unroll=8)

        @pl.when(f == 0)
        def _():
            @pl.when(i == 0)
            def _():
                all_rows(lambda r: row_in(r, tok_cur[0, 0, r], xcur, gsem.at[s]).start())
                yprv[...] = jnp.zeros_like(yprv)

            all_rows(lambda r: row_in(r, 0, xcur, gsem.at[s]).wait())
            if compute:
                hbuf[...] = _from_token_tiles(xcur, rows).astype(BF16)
                acc_ref[...] = jnp.zeros_like(acc_ref)

        base = pl.multiple_of(f * chunk, chunk)
        for k in range(chunk):
            r = base + k
            row_in(r, tok_next[0, 0, r], xnxt, gsem.at[1 - s]).start()
            row_out(r, dst_prev[0, 0, r], yprv, ssem.at[1 - s]).start()
        if compute:
            h = hbuf[...]
            gate = _dot(h, wg_ref[0])
            up = _dot(h, wu_ref[0])
            acc_ref[...] += _dot((jax.nn.silu(gate) * up).astype(BF16), wd_ref[0])

        @pl.when(f == nf - 1)
        def _():
            if compute:
                _to_token_tiles(ycur, acc_ref[...])
            else:
                ycur[...] = jnp.zeros_like(ycur)
            all_rows(lambda r: row_out(r, 0, yprv, ssem.at[1 - s]).wait())

            @pl.when(i == nb - 1)
            def _():
                all_rows(lambda r: row_out(r, dst_cur[0, 0, r], ycur, ssem.at[s]).start())
                all_rows(lambda r: row_out(r, 0, ycur, ssem.at[s]).wait())
                all_rows(lambda r: row_in(r, 0, xnxt, gsem.at[1 - s]).wait())

    even = i % 2 == 0
    padding_only = bexp_ref[i] >= N_EXPERTS
    for is_even, bufs in ((True, (xb0, xb1, yb0, yb1, 0)), (False, (xb1, xb0, yb1, yb0, 1))):
        for compute in (True, False):
            @pl.when((even == is_even) & (padding_only != compute))
            def _(bufs=bufs, compute=compute):
                variant(*bufs, compute)


def _expert_call(h_tiles, d, wg, wu, wd, bexp, row_tok, row_dst, n_out_rows):
    parts = d // LANES
    n_blocks = bexp.shape[0]
    d_ff = wg.shape[2]
    assert d_ff % EXP_STEPS == 0 and (d_ff // EXP_STEPS) % LANES == 0
    tf = d_ff // EXP_STEPS
    last = n_blocks - 1
    rows_of = lambda which: pl.BlockSpec((1, 1, TM_EXP), which, memory_space=pltpu.SMEM)
    tiles = pltpu.VMEM((TM_EXP * parts, LANES), F32)
    grid_spec = pltpu.PrefetchScalarGridSpec(
        num_scalar_prefetch=1,
        grid=(n_blocks, EXP_STEPS),
        in_specs=[
            rows_of(lambda i, f, be: (i, 0, 0)),
            rows_of(lambda i, f, be: (jnp.minimum(i + 1, last), 0, 0)),
            rows_of(lambda i, f, be: (jnp.maximum(i - 1, 0), 0, 0)),
            rows_of(lambda i, f, be: (i, 0, 0)),
            pl.BlockSpec(memory_space=pl.ANY),
            pl.BlockSpec((1, d, tf), lambda i, f, be: (be[i] % N_EXPERTS, 0, f)),
            pl.BlockSpec((1, d, tf), lambda i, f, be: (be[i] % N_EXPERTS, 0, f)),
            pl.BlockSpec((1, tf, d), lambda i, f, be: (be[i] % N_EXPERTS, f, 0)),
        ],
        out_specs=pl.BlockSpec(memory_space=pl.ANY),
        scratch_shapes=[tiles, tiles, pltpu.VMEM((TM_EXP, d), BF16), pltpu.VMEM((TM_EXP, d), F32),
                        tiles, tiles, pltpu.SemaphoreType.DMA((2,)), pltpu.SemaphoreType.DMA((2,))],
    )
    return pl.pallas_call(
        _expert_body, grid_spec=grid_spec,
        out_shape=jax.ShapeDtypeStruct((n_out_rows * parts, LANES), F32),
        compiler_params=_cparams(("arbitrary", "arbitrary")), name="experts",
    )(bexp, row_tok, row_tok, row_dst, row_dst, h_tiles, wg, wu, wd)


def _combine_body(x_ref, y0_ref, y1_ref, gate_ref, gf_ref, o_ref, *, final):
    g0 = gate_ref[:, 0:1]
    g1 = gate_ref[:, 1:2]
    rows = x_ref.shape[0]
    y = x_ref[...] + (_from_token_tiles(y0_ref, rows) * g0 + _from_token_tiles(y1_ref, rows) * g1)
    o_ref[...] = _rms(y, gf_ref[...]) if final else y


def _combine_call(x2, y_tiles, gates_t, gfinal, final):
    t, d = x2.shape
    tm = min(TM_ROUTER, t)
    nt = t // tm
    parts = d // LANES
    return pl.pallas_call(
        functools.partial(_combine_body, final=final), grid=(nt,),
        in_specs=[pl.BlockSpec((tm, d), lambda i: (i, 0)),
                  pl.BlockSpec((tm * parts, LANES), lambda i: (i, 0)),
                  pl.BlockSpec((tm * parts, LANES), lambda i: (i + nt, 0)),
                  pl.BlockSpec((tm, TOP_K), lambda i: (i, 0)),
                  pl.BlockSpec((1, d), lambda i: (0, 0))],
        out_specs=pl.BlockSpec((tm, d), lambda i: (i, 0)),
        out_shape=jax.ShapeDtypeStruct((t, d), F32),
        compiler_params=_cparams(("arbitrary",)), name="moe_combine",
    )(x2, y_tiles, y_tiles, gates_t, gfinal)


def _final_norm_body(x_ref, g_ref, o_ref):
    o_ref[...] = _rms(x_ref[...], g_ref[...])


def _final_norm_call(x2, g):
    t, d = x2.shape
    tm = min(TM_ROUTER, t)
    return pl.pallas_call(
        _final_norm_body, grid=(t // tm,),
        in_specs=[pl.BlockSpec((tm, d), lambda i: (i, 0)), pl.BlockSpec((1, d), lambda i: (0, 0))],
        out_specs=pl.BlockSpec((tm, d), lambda i: (i, 0)),
        out_shape=jax.ShapeDtypeStruct((t, d), F32),
        compiler_params=_cparams(("arbitrary",)), name="final_norm",
    )(x2, g)


def _route_plan(idx, t):
    n_assign = TOP_K * t
    flat_e = idx.reshape(-1)
    order = jnp.argsort(flat_e).astype(jnp.int32)
    counts = jnp.sum(flat_e[:, None] == jnp.arange(N_EXPERTS, dtype=jnp.int32)[None, :], axis=0,
                     dtype=jnp.int32)
    padded = ((counts + TM_EXP - 1) // TM_EXP) * TM_EXP
    start = jnp.cumsum(counts) - counts
    cum_padded = jnp.cumsum(padded)
    start_padded = cum_padded - padded
    n_blocks = -(-n_assign // TM_EXP) + N_EXPERTS
    block_start = jnp.arange(n_blocks, dtype=jnp.int32) * TM_EXP
    bexp = jnp.minimum(jnp.sum(block_start[:, None] >= cum_padded[None, :], axis=1),
                       N_EXPERTS - 1).astype(jnp.int32)
    off = block_start[:, None] + jnp.arange(TM_EXP, dtype=jnp.int32)[None, :] - start_padded[bexp][:, None]
    valid = off < counts[bexp][:, None]
    a = order[jnp.clip(start[bexp][:, None] + off, 0, n_assign - 1)]
    bexp = bexp + N_EXPERTS * (~jnp.any(valid, axis=1)).astype(jnp.int32)
    pad_rank = (jnp.cumsum((~valid).reshape(-1).astype(jnp.int32)) - 1).reshape(valid.shape)
    row_tok = jnp.where(valid, a % t, 0).astype(jnp.int32)
    row_dst = jnp.where(valid, a, n_assign + pad_rank).astype(jnp.int32)
    n_out_rows = n_blocks * TM_EXP
    return bexp, row_tok.reshape(n_blocks, 1, TM_EXP), row_dst.reshape(n_blocks, 1, TM_EXP), n_out_rows


def _rope_tables(seq):
    pos = jnp.arange(seq, dtype=F32)[:, None]
    half_a = A_ROT_DIM // 2
    ang_a = pos * (ROPE_THETA ** (-jnp.arange(half_a, dtype=F32) / half_a))[None, :]
    one = jnp.ones((seq, HEAD_DIM - A_ROT_DIM), F32)
    cos_head = jnp.concatenate([jnp.cos(ang_a), jnp.cos(ang_a), one], axis=1)
    sin_head = jnp.concatenate([jnp.sin(ang_a), jnp.sin(ang_a), 0 * one], axis=1)
    cosa = jnp.tile(cos_head, (1, A_HEADS))
    sina = jnp.tile(sin_head, (1, A_HEADS))
    half_b = B_ROPE // 2
    ang_b = pos * (ROPE_THETA ** (-jnp.arange(half_b, dtype=F32) / half_b))[None, :]
    zl = jnp.zeros((seq, B_NOPE), F32)
    zr = jnp.zeros((seq, HEAD_PAD - B_NOPE - B_ROPE), F32)
    cosk = jnp.concatenate([zl, jnp.cos(ang_b), jnp.cos(ang_b), zr], axis=1)
    sink = jnp.concatenate([zl, jnp.sin(ang_b), jnp.sin(ang_b), zr], axis=1)
    return cosa, sina, cosk, sink, jnp.cos(ang_b).T, jnp.sin(ang_b).T


def _rotate_half_matrix():
    r = np.zeros((A_WIDTH, A_WIDTH), np.float32)
    half = A_ROT_DIM // 2
    for hd in range(A_HEADS):
        for i in range(half):
            r[hd * HEAD_DIM + half + i, hd * HEAD_DIM + i] = -1.0
            r[hd * HEAD_DIM + i, hd * HEAD_DIM + half + i] = 1.0
    return jnp.asarray(r, BF16)


def _dft_constants(seq):
    lo_n = 64
    hi_n = seq // lo_n
    k = jnp.arange(seq, dtype=jnp.int32)[:, None]
    ang_hi = ((k * jnp.arange(hi_n // 2, dtype=jnp.int32)[None, :]) % hi_n).astype(F32) * (2.0 * np.pi / hi_n)
    ang_lo = ((k * jnp.arange(lo_n, dtype=jnp.int32)[None, :]) % seq).astype(F32) * (2.0 * np.pi / seq)
    ch, sh = jnp.cos(ang_hi)[:, :, None], jnp.sin(ang_hi)[:, :, None]
    cl, sl = jnp.cos(ang_lo)[:, None, :], jnp.sin(ang_lo)[:, None, :]
    scale = seq ** -0.5
    cmat = ((ch * cl - sh * sl) * scale).astype(BF16).reshape(seq, seq // 2)
    nsmat = ((sh * cl + ch * sl) * -scale).astype(BF16).reshape(seq, seq // 2)
    c = np.arange(C_GROUP_DIM)
    angc = 2.0 * np.pi * ((c[:, None] * c[None, :]) % C_GROUP_DIM) / C_GROUP_DIM
    eye = np.eye(C_GROUPS)
    cs = np.concatenate([np.kron(eye, np.cos(angc)), np.kron(eye, np.sin(angc))], axis=1)
    return cmat, nsmat, jnp.asarray(cs * C_GROUP_DIM ** -0.5, BF16)


def _layer_weights(w_in, w_uq, w_ukv, w_out):
    d = w_in.shape[0]
    c_kpe = 3 * A_WIDTH + Q_LORA + KV_LORA
    w_kpe = w_in[:, c_kpe:c_kpe + B_ROPE]
    half = B_ROPE // 2
    w_kpe_rot = jnp.concatenate([-w_kpe[:, half:], w_kpe[:, :half]], axis=1)
    zl = jnp.zeros((d, B_NOPE), w_in.dtype)
    zr = jnp.zeros((d, HEAD_PAD - B_NOPE - B_ROPE), w_in.dtype)
    wm = jnp.concatenate([w_in[:, :c_kpe], zl, w_kpe, zr, zl, w_kpe_rot, zr,
                          w_in[:, c_kpe + B_ROPE:]], axis=1).astype(BF16)
    uq = w_uq.reshape(Q_LORA, B_HEADS, B_NOPE + B_ROPE)
    uq = jnp.pad(uq, ((0, 0), (0, 0), (0, HEAD_PAD - B_NOPE - B_ROPE)))
    wuq_t = uq.reshape(Q_LORA, B_HEADS * HEAD_PAD).T.astype(BF16)
    ukv = w_ukv.reshape(KV_LORA, B_HEADS, B_NOPE + B_V)
    wkn = jnp.pad(ukv[:, :, :B_NOPE], ((0, 0), (0, 0), (0, HEAD_PAD - B_NOPE)))
    wkn = wkn.reshape(KV_LORA, B_HEADS * HEAD_PAD).astype(BF16)
    wv_t = ukv[:, :, B_NOPE:].reshape(KV_LORA, B_WIDTH).T.astype(BF16)
    wo = w_out.astype(BF16)
    return wm, wuq_t, wkn, wv_t, wo[:A_WIDTH], wo[A_WIDTH:A_WIDTH + B_WIDTH], wo[A_WIDTH + B_WIDTH:]


def kernel(x, attn_norm, w_in, q_norm, w_uq, kv_norm, w_ukv, mix_gain, w_out, ffn_norm,
           w_ffn_gate, w_ffn_up, w_ffn_down, w_router, w_exp_gate, w_exp_up, w_exp_down,
           final_norm):
    batch, seq, d = x.shape
    depth = w_in.shape[0]
    t = batch * seq
    assert seq % (max(DILATIONS) * 2 * WIN_SUB) == 0
    tabs = _rope_tables(seq)
    rot = _rotate_half_matrix()
    cmat, nsmat, cs = _dft_constants(seq)
    row = lambda v: v.reshape(1, -1)

    x2 = x.reshape(t, d)
    for l in range(depth):
        wm, wuq_t, wkn, wv_t, wa, wb, wc = _layer_weights(w_in[l], w_uq[l], w_ukv[l], w_out[l])
        qkv_views, (qbt, kb, vbt, xcs, xcs_flip) = _proj_call(
            x2, row(attn_norm[l]), wm, rot, tabs, row(q_norm[l]), wuq_t, row(kv_norm[l]), wkn, wv_t,
            cs, batch, seq)
        win = [_win_call(*(a.reshape(batch, seq // dil, dil * A_WIDTH) for a in qkv), batch, seq, dil)
               for dil, qkv in zip(DILATIONS, qkv_views)]
        ob = _mla_call(qbt, kb, vbt, batch, seq)
        oc = _dft_call(cmat, nsmat, xcs, xcs_flip, batch, seq)
        g = mix_gain[l]
        x2 = _out_call(x2, [w[0] for w in win], [w[1] for w in win], ob, oc,
                       row(g[:A_WIDTH]), row(g[A_WIDTH:A_WIDTH + B_WIDTH]), row(g[A_WIDTH + B_WIDTH:]),
                       wa, wb, wc, batch, seq)
        i = l // 2
        if l % 2 == 0:
            x2 = _ffn_call(x2, row(ffn_norm[l]), w_ffn_gate[i].astype(BF16), w_ffn_up[i].astype(BF16),
                           w_ffn_down[i].astype(BF16))
            if l == depth - 1:
                x2 = _final_norm_call(x2, row(final_norm))
        else:
            idx, gates, h_tiles = _router_call(x2, row(ffn_norm[l]), w_router[i].T)
            bexp, row_tok, row_dst, n_out_rows = _route_plan(idx, t)
            parts = d // LANES
            y_tiles = _expert_call(h_tiles, d, w_exp_gate[i].astype(BF16), w_exp_up[i].astype(BF16),
                                   w_exp_down[i].astype(BF16), bexp, row_tok * parts, row_dst * parts,
                                   n_out_rows)
            x2 = _combine_call(x2, y_tiles, gates.T, row(final_norm), final=(l == depth - 1))
    return x2.reshape(batch, seq, d)
```

```python
import functools
import math

import jax
import jax.numpy as jnp
import numpy as np
from jax import lax
from jax.experimental import pallas as pl
from jax.experimental.pallas import tpu as pltpu

F32 = jnp.float32
BF16 = jnp.bfloat16

HEAD_DIM = 64
A_HEADS = 6
A_WIDTH = A_HEADS * HEAD_DIM
A_ROT_DIM = HEAD_DIM // 4
DILATIONS = (1, 4, 16)
WINDOW_RADIUS = 64
B_HEADS = 6
B_NOPE = 64
B_ROPE = 32
B_V = 64
B_WIDTH = B_HEADS * B_V
Q_LORA = 384
KV_LORA = 128
C_GROUPS = 4
C_GROUP_DIM = 64
C_WIDTH = C_GROUPS * C_GROUP_DIM
ROPE_THETA = 500000.0
N_EXPERTS = 8
TOP_K = 2
RMS_EPS = 1e-6
NEG_INF = -1e30

LANES = 128
BF16_SUBLANES = 16
HEAD_PAD = 128
VMEM_LIMIT_BYTES = 56 * 1024 * 1024

TM_PROJ = 512
TQ_WIN = 1024
WIN_SUB = 128
TQ_MLA = 512
TK_MLA = 256
TM_DFT = 1024
TK_DFT = 512
TM_FFN = 512
TM_ROUTER = 512
EXP_STEPS = 7
TM_EXP = 128 * EXP_STEPS


def _cparams(sem):
    return pltpu.CompilerParams(dimension_semantics=sem, vmem_limit_bytes=VMEM_LIMIT_BYTES)


def _rms(x, g):
    return x * lax.rsqrt(jnp.mean(x * x, axis=-1, keepdims=True) + RMS_EPS) * g


def _dot(a, b):
    return jnp.dot(a, b, preferred_element_type=F32)


def _dot_nt(a, b):
    return lax.dot_general(a, b, (((1,), (1,)), ((), ())), preferred_element_type=F32)


def _split_residues(scr, val, nat_ref, view_refs):
    tm = val.shape[0]
    nat_ref[...] = val.astype(BF16)
    for c in range(A_WIDTH // LANES):
        scr[c] = val[:, c * LANES:(c + 1) * LANES]
    for dil, ref in zip(DILATIONS[1:], view_refs):
        for r in range(dil):
            for c in range(A_WIDTH // LANES):
                col = r * A_WIDTH + c * LANES
                ref[0, :, col:col + LANES] = scr[c, pl.ds(r, tm // dil, stride=dil), :].astype(BF16)


def _proj_body(x_ref, g_ref, wm_ref, rot_ref, cosa_ref, sina_ref, cosk_ref, sink_ref,
               cosq_ref, sinq_ref, qn_ref, wuq_ref, kvn_ref, wkn_ref, wv_ref, cs_ref, flip_ref,
               qa_ref, qa4_ref, qa16_ref, ka_ref, ka4_ref, ka16_ref, va_ref, va4_ref, va16_ref,
               qbt_ref, kb_ref, vbt_ref, xcs_ref, xcs_flip_ref, scr):
    h = _rms(x_ref[...], g_ref[...]).astype(BF16)
    z = _dot(h, wm_ref[...])
    o = 0
    qa = z[:, o:o + A_WIDTH]; o += A_WIDTH
    ka = z[:, o:o + A_WIDTH]; o += A_WIDTH
    va = z[:, o:o + A_WIDTH]; o += A_WIDTH
    cq = z[:, o:o + Q_LORA]; o += Q_LORA
    ckv = z[:, o:o + KV_LORA]; o += KV_LORA
    kpe = z[:, o:o + HEAD_PAD]; o += HEAD_PAD
    kpe_rot = z[:, o:o + HEAD_PAD]; o += HEAD_PAD
    fc = z[:, o:o + C_WIDTH]

    cosa = cosa_ref[...]
    sina = sina_ref[...]
    rot = rot_ref[...]
    _split_residues(scr, qa * cosa + _dot(qa.astype(BF16), rot) * sina, qa_ref, (qa4_ref, qa16_ref))
    _split_residues(scr, ka * cosa + _dot(ka.astype(BF16), rot) * sina, ka_ref, (ka4_ref, ka16_ref))
    _split_residues(scr, va, va_ref, (va4_ref, va16_ref))

    cqn = _rms(cq, qn_ref[...]).astype(BF16)
    qbt = _dot_nt(wuq_ref[...], cqn) * ((B_NOPE + B_ROPE) ** -0.5 * math.log2(math.e))
    cosq = cosq_ref[...]
    sinq = sinq_ref[...]
    half = B_ROPE // 2
    for hd in range(B_HEADS):
        r0 = hd * HEAD_PAD
        x1 = qbt[r0 + B_NOPE:r0 + B_NOPE + half]
        x2 = qbt[r0 + B_NOPE + half:r0 + B_NOPE + B_ROPE]
        qbt_ref[0, r0:r0 + B_NOPE, :] = qbt[r0:r0 + B_NOPE].astype(BF16)
        qbt_ref[0, r0 + B_NOPE:r0 + B_NOPE + half, :] = (x1 * cosq - x2 * sinq).astype(BF16)
        qbt_ref[0, r0 + B_NOPE + half:r0 + B_NOPE + B_ROPE, :] = (x2 * cosq + x1 * sinq).astype(BF16)
        qbt_ref[0, r0 + B_NOPE + B_ROPE:r0 + HEAD_PAD, :] = jnp.zeros(
            (HEAD_PAD - B_NOPE - B_ROPE, qbt.shape[1]), BF16)

    ckvn = _rms(ckv, kvn_ref[...]).astype(BF16)
    kn = _dot(ckvn, wkn_ref[...])
    kpe_r = kpe * cosk_ref[...] + kpe_rot * sink_ref[...]
    for hd in range(B_HEADS):
        kb_ref[0, hd] = (kn[:, hd * HEAD_PAD:(hd + 1) * HEAD_PAD] + kpe_r).astype(BF16)
    vbt_ref[0] = _dot_nt(wv_ref[...], ckvn).astype(BF16)

    xcs = _dot(fc.astype(BF16), cs_ref[...]).astype(BF16)
    xcs_ref[0] = xcs
    xcs_flip_ref[0] = _dot(flip_ref[...], xcs).astype(BF16)


def _proj_call(x2, g, wm, rot, tabs, qn, wuq, kvn, wkn, wv, cs, batch, seq):
    t, d = x2.shape
    tm = min(TM_PROJ, seq)
    ns = seq // tm
    nm = wm.shape[1]
    cosa, sina, cosk, sink, cosq, sinq = tabs
    row = lambda i: (i, 0)
    pos = lambda i: (i % ns, 0)
    post = lambda i: (0, i % ns)
    const = lambda i: (0, 0)
    in_specs = [
        pl.BlockSpec((tm, d), row),
        pl.BlockSpec((1, d), const),
        pl.BlockSpec((d, nm), const),
        pl.BlockSpec((A_WIDTH, A_WIDTH), const),
        pl.BlockSpec((tm, A_WIDTH), pos),
        pl.BlockSpec((tm, A_WIDTH), pos),
        pl.BlockSpec((tm, HEAD_PAD), pos),
        pl.BlockSpec((tm, HEAD_PAD), pos),
        pl.BlockSpec((B_ROPE // 2, tm), post),
        pl.BlockSpec((B_ROPE // 2, tm), post),
        pl.BlockSpec((1, Q_LORA), const),
        pl.BlockSpec((B_HEADS * HEAD_PAD, Q_LORA), const),
        pl.BlockSpec((1, KV_LORA), const),
        pl.BlockSpec((KV_LORA, B_HEADS * HEAD_PAD), const),
        pl.BlockSpec((B_WIDTH, KV_LORA), const),
        pl.BlockSpec((C_WIDTH, 2 * C_WIDTH), const),
        pl.BlockSpec((tm, tm), const),
    ]
    bpos = lambda i: (i // ns, i % ns, 0)
    a_shapes, a_specs = [], []
    for _ in range(3):
        a_shapes.append(jax.ShapeDtypeStruct((t, A_WIDTH), BF16))
        a_specs.append(pl.BlockSpec((tm, A_WIDTH), row))
        for dil in DILATIONS[1:]:
            a_shapes.append(jax.ShapeDtypeStruct((batch, seq // dil, dil * A_WIDTH), BF16))
            a_specs.append(pl.BlockSpec((1, tm // dil, dil * A_WIDTH), bpos))
    out_shape = (
        *a_shapes,
        jax.ShapeDtypeStruct((batch, B_HEADS * HEAD_PAD, seq), BF16),
        jax.ShapeDtypeStruct((batch, B_HEADS, seq, HEAD_PAD), BF16),
        jax.ShapeDtypeStruct((batch, B_WIDTH, seq), BF16),
        jax.ShapeDtypeStruct((batch, seq, 2 * C_WIDTH), BF16),
        jax.ShapeDtypeStruct((batch, seq, 2 * C_WIDTH), BF16),
    )
    out_specs = (
        *a_specs,
        pl.BlockSpec((1, B_HEADS * HEAD_PAD, tm), lambda i: (i // ns, 0, i % ns)),
        pl.BlockSpec((1, B_HEADS, tm, HEAD_PAD), lambda i: (i // ns, 0, i % ns, 0)),
        pl.BlockSpec((1, B_WIDTH, tm), lambda i: (i // ns, 0, i % ns)),
        pl.BlockSpec((1, tm, 2 * C_WIDTH), bpos),
        pl.BlockSpec((1, tm, 2 * C_WIDTH), lambda i: (i // ns, ns - 1 - i % ns, 0)),
    )
    flip = jnp.asarray(np.eye(tm, dtype=np.float32)[::-1], BF16)
    outs = pl.pallas_call(
        _proj_body, grid=(t // tm,), in_specs=in_specs, out_specs=out_specs, out_shape=out_shape,
        scratch_shapes=[pltpu.VMEM((A_WIDTH // LANES, tm, LANES), F32)],
        compiler_params=_cparams(("arbitrary",)), name="proj",
    )(x2, g, wm, rot, cosa, sina, cosk, sink, cosq, sinq, qn, wuq, kvn, wkn, wv, cs, flip)
    nd = len(DILATIONS)
    qkv_views = [tuple(outs[a * nd + j] for a in range(3)) for j in range(nd)]
    return qkv_views, outs[3 * nd:]


def _win_body(q_ref, kp_ref, km_ref, kn_ref, vp_ref, vm_ref, vn_ref, o_ref, l_ref, *, seg_len, tq):
    j = pl.program_id(2)
    span = WIN_SUB + 2 * WINDOW_RADIUS
    lane = lax.broadcasted_iota(jnp.int32, (1, LANES), 1)
    lo_half = lane < HEAD_DIM
    rel = (lax.broadcasted_iota(jnp.int32, (WIN_SUB, span), 1) - WINDOW_RADIUS
           - lax.broadcasted_iota(jnp.int32, (WIN_SUB, span), 0))
    band = jnp.abs(rel) <= WINDOW_RADIUS
    kcol = lax.broadcasted_iota(jnp.int32, (1, span), 1)
    n_sub = tq // WIN_SUB
    biases = []
    for sb in range(n_sub):
        kpos = j * tq + sb * WIN_SUB - WINDOW_RADIUS + kcol
        biases.append(jnp.where(band & (kpos >= 0) & (kpos < seg_len), 0.0, NEG_INF))
    l_ref[...] = jnp.zeros_like(l_ref)
    for pair in range(A_HEADS // 2):
        cs = slice(pair * LANES, (pair + 1) * LANES)
        q2 = q_ref[0, :, cs] * (HEAD_DIM ** -0.5 * math.log2(math.e))
        qh = (jnp.where(lo_half, q2, 0).astype(BF16), jnp.where(lo_half, 0, q2).astype(BF16))
        kcat = jnp.concatenate([kp_ref[0, :, cs], km_ref[0, :, cs], kn_ref[0, :, cs]], axis=0)
        vcat = jnp.concatenate([vp_ref[0, :, cs], vm_ref[0, :, cs], vn_ref[0, :, cs]], axis=0)
        for sb in range(n_sub):
            r0 = sb * WIN_SUB
            ks = kcat[r0:r0 + span]
            vs = vcat[r0:r0 + span]
            outs = []
            for hh in range(2):
                s = _dot_nt(qh[hh][r0:r0 + WIN_SUB], ks) + biases[sb]
                m = jnp.max(s, axis=-1, keepdims=True)
                p = jnp.exp2(s - m)
                den = jnp.sum(p, axis=-1, keepdims=True)
                outs.append(_dot(p.astype(BF16), vs) / den)
                head = 2 * pair + hh
                l_ref[0, r0:r0 + WIN_SUB, head:head + 1] = m * math.log(2.0) + jnp.log(den)
            o_ref[0, r0:r0 + WIN_SUB, cs] = jnp.where(lo_half, outs[0], outs[1]).astype(BF16)


def _win_call(q, k, v, batch, seq, dil):
    seg = seq // dil
    tq = min(TQ_WIN, seg)
    nb = tq // WINDOW_RADIUS
    last = seg // WINDOW_RADIUS - 1
    main = pl.BlockSpec((1, tq, A_WIDTH), lambda b, r, j: (b, j, r))
    prev = pl.BlockSpec((1, WINDOW_RADIUS, A_WIDTH), lambda b, r, j: (b, jnp.maximum(j * nb - 1, 0), r))
    nxt = pl.BlockSpec((1, WINDOW_RADIUS, A_WIDTH), lambda b, r, j: (b, jnp.minimum((j + 1) * nb, last), r))
    return pl.pallas_call(
        functools.partial(_win_body, seg_len=seg, tq=tq),
        grid=(batch, dil, seg // tq),
        in_specs=[main, prev, main, nxt, prev, main, nxt],
        out_specs=(main, pl.BlockSpec((1, tq, LANES), lambda b, r, j: (b, j, r))),
        out_shape=(jax.ShapeDtypeStruct((batch, seg, dil * A_WIDTH), BF16),
                   jax.ShapeDtypeStruct((batch, seg, dil * LANES), F32)),
        compiler_params=_cparams(("arbitrary", "arbitrary", "arbitrary")), name=f"win_d{dil}",
    )(q, k, k, k, v, v, v)


def _mla_body(qt_ref, k_ref, vt_ref, o_ref, sa, sb, pa, pb, acc_ref, *, seq, tk):
    tq = qt_ref.shape[2]
    nk = seq // tk
    qts = [qt_ref[0, h * HEAD_PAD:(h + 1) * HEAD_PAD, :] for h in range(B_HEADS)]
    ones = jnp.ones((BF16_SUBLANES, tk), BF16)

    def scores(h, kt, dst):
        start = kt * tk if isinstance(kt, int) else pl.multiple_of(kt * tk, tk)
        dst[h] = _dot(k_ref[0, h, pl.ds(start, tk), :], qts[h])

    def values(h, kt, src, alpha):
        start = kt * tk if isinstance(kt, int) else pl.multiple_of(kt * tk, tk)
        vt = jnp.concatenate([vt_ref[0, h * B_V:(h + 1) * B_V, pl.ds(start, tk)], ones], axis=0)
        acc_ref[h] = alpha * acc_ref[h] + _dot(vt, src[h])

    def softmax(h, src, dst, m):
        st = src[h]
        m_new = jnp.maximum(m, jnp.max(st, axis=0, keepdims=True))
        dst[h] = jnp.exp2(st - m_new).astype(BF16)
        return m_new, jnp.exp2(m - m_new)

    for h in range(B_HEADS):
        scores(h, 0, sa)
        acc_ref[h] = jnp.zeros((B_V + BF16_SUBLANES, tq), F32)
    pb[...] = jnp.zeros_like(pb)

    def step(i, carry):
        out = []
        for h in range(B_HEADS):
            m, alpha_prev = carry[h]
            t0 = 2 * i
            scores(h, t0 + 1, sb)
            values(h, jnp.maximum(t0 - 1, 0), pb, alpha_prev)
            m, alpha0 = softmax(h, sa, pa, m)
            scores(h, jnp.minimum(t0 + 2, nk - 1), sa)
            values(h, t0, pa, alpha0)
            m, alpha1 = softmax(h, sb, pb, m)
            out.append((m, alpha1))
        return tuple(out)

    init = tuple((jnp.full((1, tq), NEG_INF, F32), jnp.ones((1, tq), F32)) for _ in range(B_HEADS))
    res = lax.fori_loop(0, nk // 2, step, init)
    outs = []
    for h in range(B_HEADS):
        values(h, nk - 1, pb, res[h][1])
        acc = acc_ref[h]
        outs.append(acc[:B_V] / acc[B_V:B_V + 1])
    o_ref[...] = jnp.concatenate(outs, axis=0).T.astype(BF16)


def _mla_call(qbt, kb, vbt, batch, seq):
    tq = min(TQ_MLA, seq)
    tk = min(TK_MLA, seq)
    nq = seq // tq
    assert (seq // tk) % 2 == 0
    return pl.pallas_call(
        functools.partial(_mla_body, seq=seq, tk=tk),
        grid=(batch, nq),
        in_specs=[
            pl.BlockSpec((1, B_HEADS * HEAD_PAD, tq), lambda b, qi: (b, 0, qi)),
            pl.BlockSpec((1, B_HEADS, seq, HEAD_PAD), lambda b, qi: (b, 0, 0, 0)),
            pl.BlockSpec((1, B_WIDTH, seq), lambda b, qi: (b, 0, 0)),
        ],
        out_specs=pl.BlockSpec((tq, B_WIDTH), lambda b, qi: (b * nq + qi, 0)),
        out_shape=jax.ShapeDtypeStruct((batch * seq, B_WIDTH), BF16),
        scratch_shapes=[pltpu.VMEM((B_HEADS, tk, tq), F32), pltpu.VMEM((B_HEADS, tk, tq), F32),
                        pltpu.VMEM((B_HEADS, tk, tq), BF16), pltpu.VMEM((B_HEADS, tk, tq), BF16),
                        pltpu.VMEM((B_HEADS, B_V + BF16_SUBLANES, tq), F32)],
        compiler_params=_cparams(("arbitrary", "arbitrary")), name="mla_attn",
    )(qbt, kb, vbt)


def _dft_body(c_ref, ns_ref, x_ref, xf_ref, xfp_ref, xh_ref, o_ref, acc_ref, *, seq):
    k = pl.program_id(1)
    tm = c_ref.shape[0]
    tk = x_ref.shape[1]

    @pl.when(k == 0)
    def _():
        row = lax.broadcasted_iota(jnp.int32, (tm, 1), 0)
        sign = (1 - 2 * (row & 1)).astype(F32) * (seq ** -0.5)
        for b in range(x_ref.shape[0]):
            acc_ref[b] = sign * xh_ref[b, 0:1, 0:C_WIDTH].astype(F32)

    c = c_ref[...]
    ns = ns_ref[...]
    first_row = lax.broadcasted_iota(jnp.int32, (tk, 1), 0) == 0
    has_prev = (k > 0).astype(F32)
    for b in range(x_ref.shape[0]):
        x = x_ref[b].astype(F32)
        carry_in = xfp_ref[b, 7:8, :].astype(F32) * has_prev
        xr = jnp.where(first_row, carry_in, pltpu.roll(xf_ref[b].astype(F32), 1, axis=0))
        sym = (x[:, 0:C_WIDTH] + xr[:, 0:C_WIDTH]).astype(BF16)
        anti = (x[:, C_WIDTH:2 * C_WIDTH] - xr[:, C_WIDTH:2 * C_WIDTH]).astype(BF16)
        acc_ref[b] += _dot(c, sym) + _dot(ns, anti)

    @pl.when(k == pl.num_programs(1) - 1)
    def _():
        o_ref[...] = acc_ref[...].astype(o_ref.dtype)


def _dft_call(cmat, nsmat, x3, x3_flip, batch, seq):
    half = seq // 2
    tm = min(TM_DFT, seq)
    tk = min(TK_DFT, half)
    rows8 = tk // 8
    blk = lambda rows, index: pl.BlockSpec((batch, rows, 2 * C_WIDTH), index)
    y = pl.pallas_call(
        functools.partial(_dft_body, seq=seq), grid=(seq // tm, half // tk),
        in_specs=[pl.BlockSpec((tm, tk), lambda i, k: (i, k)),
                  pl.BlockSpec((tm, tk), lambda i, k: (i, k)),
                  blk(tk, lambda i, k: (0, k, 0)),
                  blk(tk, lambda i, k: (0, k, 0)),
                  blk(8, lambda i, k: (0, jnp.maximum(k * rows8 - 1, 0), 0)),
                  blk(8, lambda i, k: (0, half // 8, 0))],
        out_specs=pl.BlockSpec((batch, tm, C_WIDTH), lambda i, k: (0, i, 0)),
        out_shape=jax.ShapeDtypeStruct((batch, seq, C_WIDTH), BF16),
        scratch_shapes=[pltpu.VMEM((batch, tm, C_WIDTH), F32)],
        compiler_params=_cparams(("arbitrary", "arbitrary")), name="seq_dft",
    )(cmat, nsmat, x3, x3_flip, x3_flip, x3)
    return y.reshape(batch * seq, C_WIDTH)


def _merge_residues(scr, ref, dil):
    slabs, tm, _ = scr.shape
    for r in range(dil):
        for c in range(slabs):
            col = (r * slabs + c) * LANES
            scr[c, pl.ds(r, tm // dil, stride=dil), :] = ref[0, :, col:col + LANES].astype(F32)
    return jnp.concatenate([scr[c] for c in range(slabs)], axis=1)


def _out_body(x_ref, o1_ref, o4_ref, o16_ref, l1_ref, l4_ref, l16_ref, ob_ref, oc_ref,
              ga_ref, gb_ref, gc_ref, wa_ref, wb_ref, wc_ref, spread_ref, y_ref, so4, sl4, so16, sl16):
    l1 = l1_ref[0]
    l4 = _merge_residues(sl4, l4_ref, DILATIONS[1])
    l16 = _merge_residues(sl16, l16_ref, DILATIONS[2])
    o4 = _merge_residues(so4, o4_ref, DILATIONS[1])
    o16 = _merge_residues(so16, o16_ref, DILATIONS[2])
    m = jnp.maximum(jnp.maximum(l1, l4), l16)
    e1, e4, e16 = jnp.exp(l1 - m), jnp.exp(l4 - m), jnp.exp(l16 - m)
    inv = 1.0 / (e1 + e4 + e16)

    def per_lane(w):
        hi = w.astype(BF16)
        lo = (w - hi.astype(F32)).astype(BF16)
        return _dot(hi, spread_ref[...]) + _dot(lo, spread_ref[...])

    oa = (o1_ref[0].astype(F32) * per_lane(e1 * inv) + o4 * per_lane(e4 * inv)
          + o16 * per_lane(e16 * inv))
    na = _rms(oa, ga_ref[...]).astype(BF16)
    nb = _rms(ob_ref[...].astype(F32), gb_ref[...]).astype(BF16)
    nc = _rms(oc_ref[...].astype(F32), gc_ref[...]).astype(BF16)
    y_ref[...] = (x_ref[...] + _dot(na, wa_ref[...]) + _dot(nb, wb_ref[...]) + _dot(nc, wc_ref[...]))


def _out_call(x2, oa, la, ob, oc, ga, gb, gc, wa, wb, wc, batch, seq):
    t, d = x2.shape
    tm = min(TM_PROJ, seq)
    ns = seq // tm
    row = lambda i: (i, 0)
    const = lambda i: (0, 0)
    wide = lambda w: pl.BlockSpec((tm, w), row)
    view = lambda dil, w: pl.BlockSpec((1, tm // dil, dil * w), lambda i: (i // ns, i % ns, 0))
    slab = lambda w: pltpu.VMEM((w // LANES, tm, LANES), F32)
    spread = np.zeros((LANES, A_WIDTH), np.float32)
    for hd in range(A_HEADS):
        spread[hd, hd * HEAD_DIM:(hd + 1) * HEAD_DIM] = 1.0
    return pl.pallas_call(
        _out_body, grid=(t // tm,),
        in_specs=[wide(d)] + [view(dil, A_WIDTH) for dil in DILATIONS]
                 + [view(dil, LANES) for dil in DILATIONS] + [wide(B_WIDTH), wide(C_WIDTH),
                  pl.BlockSpec((1, A_WIDTH), const), pl.BlockSpec((1, B_WIDTH), const),
                  pl.BlockSpec((1, C_WIDTH), const),
                  pl.BlockSpec((A_WIDTH, d), const), pl.BlockSpec((B_WIDTH, d), const),
                  pl.BlockSpec((C_WIDTH, d), const), pl.BlockSpec((LANES, A_WIDTH), const)],
        out_specs=wide(d),
        out_shape=jax.ShapeDtypeStruct((t, d), F32),
        scratch_shapes=[slab(A_WIDTH), slab(LANES), slab(A_WIDTH), slab(LANES)],
        compiler_params=_cparams(("arbitrary",)), name="out_proj",
    )(x2, oa[0], oa[1], oa[2], la[0], la[1], la[2], ob, oc, ga, gb, gc, wa, wb, wc,
      jnp.asarray(spread, BF16))


def _ffn_body(x_ref, g_ref, wg_ref, wu_ref, wd_ref, y_ref):
    x = x_ref[...]
    h = _rms(x, g_ref[...]).astype(BF16)
    gate = _dot(h, wg_ref[...])
    up = _dot(h, wu_ref[...])
    y_ref[...] = x + _dot((jax.nn.silu(gate) * up).astype(BF16), wd_ref[...])


def _ffn_call(x2, g, wg, wu, wd):
    t, d = x2.shape
    d_ff = wg.shape[1]
    tm = min(TM_FFN, t)
    resident = lambda shape: pl.BlockSpec(shape, lambda i: (0, 0), pipeline_mode=pl.Buffered(1))
    return pl.pallas_call(
        _ffn_body, grid=(t // tm,),
        in_specs=[pl.BlockSpec((tm, d), lambda i: (i, 0)),
                  pl.BlockSpec((1, d), lambda i: (0, 0)),
                  resident((d, d_ff)), resident((d, d_ff)), resident((d_ff, d))],
        out_specs=pl.BlockSpec((tm, d), lambda i: (i, 0)),
        out_shape=jax.ShapeDtypeStruct((t, d), F32),
        compiler_params=_cparams(("arbitrary",)), name="dense_ffn",
    )(x2, g, wg, wu, wd)


def _to_token_tiles(ref, val):
    rows, d = val.shape
    parts = d // LANES
    for c in range(parts):
        ref[pl.ds(c, rows, stride=parts), :] = val[:, c * LANES:(c + 1) * LANES]


def _from_token_tiles(ref, rows):
    parts = ref.shape[0] // rows
    return jnp.concatenate([ref[pl.ds(c, rows, stride=parts), :] for c in range(parts)], axis=1)


def _router_body(x_ref, g_ref, wr_ref, idx_ref, gate_ref, htile_ref):
    h = _rms(x_ref[...], g_ref[...])
    _to_token_tiles(htile_ref, h)
    logits = lax.dot_general(wr_ref[...], h, (((1,), (1,)), ((), ())),
                             precision=lax.Precision.HIGHEST, preferred_element_type=F32)
    eid = lax.broadcasted_iota(jnp.int32, logits.shape, 0)
    m1 = jnp.max(logits, axis=0, keepdims=True)
    i1 = jnp.min(jnp.where(logits == m1, eid, N_EXPERTS), axis=0, keepdims=True)
    rest = jnp.where(eid == i1, -jnp.inf, logits)
    m2 = jnp.max(rest, axis=0, keepdims=True)
    i2 = jnp.min(jnp.where(rest == m2, eid, N_EXPERTS), axis=0, keepdims=True)
    e2 = jnp.exp(m2 - m1)
    idx_ref[...] = jnp.concatenate([i1, i2], axis=0)
    gate_ref[...] = jnp.concatenate([1.0 / (1.0 + e2), e2 / (1.0 + e2)], axis=0)


def _router_call(x2, g, wr_t):
    t, d = x2.shape
    tm = min(TM_ROUTER, t)
    return pl.pallas_call(
        _router_body, grid=(t // tm,),
        in_specs=[pl.BlockSpec((tm, d), lambda i: (i, 0)),
                  pl.BlockSpec((1, d), lambda i: (0, 0)),
                  pl.BlockSpec((N_EXPERTS, d), lambda i: (0, 0))],
        out_specs=(pl.BlockSpec((TOP_K, tm), lambda i: (0, i)),
                   pl.BlockSpec((TOP_K, tm), lambda i: (0, i)),
                   pl.BlockSpec((tm * d // LANES, LANES), lambda i: (i, 0))),
        out_shape=(jax.ShapeDtypeStruct((TOP_K, t), jnp.int32),
                   jax.ShapeDtypeStruct((TOP_K, t), F32),
                   jax.ShapeDtypeStruct((t * d // LANES, LANES), F32)),
        compiler_params=_cparams(("arbitrary",)), name="router",
    )(x2, g, wr_t)


def _expert_body(bexp_ref, tok_cur, tok_next, dst_prev, dst_cur, h_hbm, wg_ref, wu_ref, wd_ref,
                 y_hbm, xb0, xb1, hbuf, acc_ref, yb0, yb1, gsem, ssem):
    i = pl.program_id(0)
    f = pl.program_id(1)
    nb = pl.num_programs(0)
    nf = pl.num_programs(1)
    rows = hbuf.shape[0]
    parts = xb0.shape[0] // rows
    chunk = rows // EXP_STEPS

    def variant(xcur, xnxt, ycur, yprv, s, compute):
        def tile(ref, first_row):
            if not isinstance(first_row, int):
                first_row = pl.multiple_of(first_row, parts)
            return ref.at[pl.ds(first_row, parts)]

        def row_in(r, src, buf, sem):
            return pltpu.make_async_copy(tile(h_hbm, src), tile(buf, r * parts), sem)

        def row_out(r, dst, buf, sem):
            return pltpu.make_async_copy(tile(buf, r * parts), tile(y_hbm, dst), sem)

        def all_rows(fn):
            def body(r, c):
                fn(r)
                return c
            lax.fori_loop(0, rows, body, 0, unroll=8)

        @pl.when(f == 0)
        def _():
            @pl.when(i == 0)
            def _():
                all_rows(lambda r: row_in(r, tok_cur[0, 0, r], xcur, gsem.at[s]).start())
                yprv[...] = jnp.zeros_like(yprv)

            all_rows(lambda r: row_in(r, 0, xcur, gsem.at[s]).wait())
            if compute:
                hbuf[...] = _from_token_tiles(xcur, rows).astype(BF16)
                acc_ref[...] = jnp.zeros_like(acc_ref)

        base = pl.multiple_of(f * chunk, chunk)
        for k in range(chunk):
            r = base + k
            row_in(r, tok_next[0, 0, r], xnxt, gsem.at[1 - s]).start()
            row_out(r, dst_prev[0, 0, r], yprv, ssem.at[1 - s]).start()
        if compute:
            h = hbuf[...]
            gate = _dot(h, wg_ref[0])
            up = _dot(h, wu_ref[0])
            acc_ref[...] += _dot((jax.nn.silu(gate) * up).astype(BF16), wd_ref[0])

        @pl.when(f == nf - 1)
        def _():
            if compute:
                _to_token_tiles(ycur, acc_ref[...])
            else:
                ycur[...] = jnp.zeros_like(ycur)
            all_rows(lambda r: row_out(r, 0, yprv, ssem.at[1 - s]).wait())

            @pl.when(i == nb - 1)
            def _():
                all_rows(lambda r: row_out(r, dst_cur[0, 0, r], ycur, ssem.at[s]).start())
                all_rows(lambda r: row_out(r, 0, ycur, ssem.at[s]).wait())
                all_rows(lambda r: row_in(r, 0, xnxt, gsem.at[1 - s]).wait())

    even = i % 2 == 0
    padding_only = bexp_ref[i] >= N_EXPERTS
    for is_even, bufs in ((True, (xb0, xb1, yb0, yb1, 0)), (False, (xb1, xb0, yb1, yb0, 1))):
        for compute in (True, False):
            @pl.when((even == is_even) & (padding_only != compute))
            def _(bufs=bufs, compute=compute):
                variant(*bufs, compute)


def _expert_call(h_tiles, d, wg, wu, wd, bexp, row_tok, row_dst, n_out_rows):
    parts = d // LANES
    n_blocks = bexp.shape[0]
    d_ff = wg.shape[2]
    assert d_ff % EXP_STEPS == 0 and (d_ff // EXP_STEPS) % LANES == 0
    tf = d_ff // EXP_STEPS
    last = n_blocks - 1
    rows_of = lambda which: pl.BlockSpec((1, 1, TM_EXP), which, memory_space=pltpu.SMEM)
    tiles = pltpu.VMEM((TM_EXP * parts, LANES), F32)
    grid_spec = pltpu.PrefetchScalarGridSpec(
        num_scalar_prefetch=1,
        grid=(n_blocks, EXP_STEPS),
        in_specs=[
            rows_of(lambda i, f, be: (i, 0, 0)),
            rows_of(lambda i, f, be: (jnp.minimum(i + 1, last), 0, 0)),
            rows_of(lambda i, f, be: (jnp.maximum(i - 1, 0), 0, 0)),
            rows_of(lambda i, f, be: (i, 0, 0)),
            pl.BlockSpec(memory_space=pl.ANY),
            pl.BlockSpec((1, d, tf), lambda i, f, be: (be[i] % N_EXPERTS, 0, f)),
            pl.BlockSpec((1, d, tf), lambda i, f, be: (be[i] % N_EXPERTS, 0, f)),
            pl.BlockSpec((1, tf, d), lambda i, f, be: (be[i] % N_EXPERTS, f, 0)),
        ],
        out_specs=pl.BlockSpec(memory_space=pl.ANY),
        scratch_shapes=[tiles, tiles, pltpu.VMEM((TM_EXP, d), BF16), pltpu.VMEM((TM_EXP, d), F32),
                        tiles, tiles, pltpu.SemaphoreType.DMA((2,)), pltpu.SemaphoreType.DMA((2,))],
    )
    return pl.pallas_call(
        _expert_body, grid_spec=grid_spec,
        out_shape=jax.ShapeDtypeStruct((n_out_rows * parts, LANES), F32),
        compiler_params=_cparams(("arbitrary", "arbitrary")), name="experts",
    )(bexp, row_tok, row_tok, row_dst, row_dst, h_tiles, wg, wu, wd)


def _combine_body(x_ref, y0_ref, y1_ref, gate_ref, gf_ref, o_ref, *, final):
    g0 = gate_ref[:, 0:1]
    g1 = gate_ref[:, 1:2]
    rows = x_ref.shape[0]
    y = x_ref[...] + (_from_token_tiles(y0_ref, rows) * g0 + _from_token_tiles(y1_ref, rows) * g1)
    o_ref[...] = _rms(y, gf_ref[...]) if final else y


def _combine_call(x2, y_tiles, gates_t, gfinal, final):
    t, d = x2.shape
    tm = min(TM_ROUTER, t)
    nt = t // tm
    parts = d // LANES
    return pl.pallas_call(
        functools.partial(_combine_body, final=final), grid=(nt,),
        in_specs=[pl.BlockSpec((tm, d), lambda i: (i, 0)),
                  pl.BlockSpec((tm * parts, LANES), lambda i: (i, 0)),
                  pl.BlockSpec((tm * parts, LANES), lambda i: (i + nt, 0)),
                  pl.BlockSpec((tm, TOP_K), lambda i: (i, 0)),
                  pl.BlockSpec((1, d), lambda i: (0, 0))],
        out_specs=pl.BlockSpec((tm, d), lambda i: (i, 0)),
        out_shape=jax.ShapeDtypeStruct((t, d), F32),
        compiler_params=_cparams(("arbitrary",)), name="moe_combine",
    )(x2, y_tiles, y_tiles, gates_t, gfinal)


def _final_norm_body(x_ref, g_ref, o_ref):
    o_ref[...] = _rms(x_ref[...], g_ref[...])


def _final_norm_call(x2, g):
    t, d = x2.shape
    tm = min(TM_ROUTER, t)
    return pl.pallas_call(
        _final_norm_body, grid=(t // tm,),
        in_specs=[pl.BlockSpec((tm, d), lambda i: (i, 0)), pl.BlockSpec((1, d), lambda i: (0, 0))],
        out_specs=pl.BlockSpec((tm, d), lambda i: (i, 0)),
        out_shape=jax.ShapeDtypeStruct((t, d), F32),
        compiler_params=_cparams(("arbitrary",)), name="final_norm",
    )(x2, g)


def _route_plan(idx, t):
    n_assign = TOP_K * t
    flat_e = idx.reshape(-1)
    order = jnp.argsort(flat_e).astype(jnp.int32)
    counts = jnp.sum(flat_e[:, None] == jnp.arange(N_EXPERTS, dtype=jnp.int32)[None, :], axis=0,
                     dtype=jnp.int32)
    padded = ((counts + TM_EXP - 1) // TM_EXP) * TM_EXP
    start = jnp.cumsum(counts) - counts
    cum_padded = jnp.cumsum(padded)
    start_padded = cum_padded - padded
    n_blocks = -(-n_assign // TM_EXP) + N_EXPERTS
    block_start = jnp.arange(n_blocks, dtype=jnp.int32) * TM_EXP
    bexp = jnp.minimum(jnp.sum(block_start[:, None] >= cum_padded[None, :], axis=1),
                       N_EXPERTS - 1).astype(jnp.int32)
    off = block_start[:, None] + jnp.arange(TM_EXP, dtype=jnp.int32)[None, :] - start_padded[bexp][:, None]
    valid = off < counts[bexp][:, None]
    a = order[jnp.clip(start[bexp][:, None] + off, 0, n_assign - 1)]
    bexp = bexp + N_EXPERTS * (~jnp.any(valid, axis=1)).astype(jnp.int32)
    pad_rank = (jnp.cumsum((~valid).reshape(-1).astype(jnp.int32)) - 1).reshape(valid.shape)
    row_tok = jnp.where(valid, a % t, 0).astype(jnp.int32)
    row_dst = jnp.where(valid, a, n_assign + pad_rank).astype(jnp.int32)
    n_out_rows = n_blocks * TM_EXP
    return bexp, row_tok.reshape(n_blocks, 1, TM_EXP), row_dst.reshape(n_blocks, 1, TM_EXP), n_out_rows


def _rope_tables(seq):
    pos = jnp.arange(seq, dtype=F32)[:, None]
    half_a = A_ROT_DIM // 2
    ang_a = pos * (ROPE_THETA ** (-jnp.arange(half_a, dtype=F32) / half_a))[None, :]
    one = jnp.ones((seq, HEAD_DIM - A_ROT_DIM), F32)
    cos_head = jnp.concatenate([jnp.cos(ang_a), jnp.cos(ang_a), one], axis=1)
    sin_head = jnp.concatenate([jnp.sin(ang_a), jnp.sin(ang_a), 0 * one], axis=1)
    cosa = jnp.tile(cos_head, (1, A_HEADS))
    sina = jnp.tile(sin_head, (1, A_HEADS))
    half_b = B_ROPE // 2
    ang_b = pos * (ROPE_THETA ** (-jnp.arange(half_b, dtype=F32) / half_b))[None, :]
    zl = jnp.zeros((seq, B_NOPE), F32)
    zr = jnp.zeros((seq, HEAD_PAD - B_NOPE - B_ROPE), F32)
    cosk = jnp.concatenate([zl, jnp.cos(ang_b), jnp.cos(ang_b), zr], axis=1)
    sink = jnp.concatenate([zl, jnp.sin(ang_b), jnp.sin(ang_b), zr], axis=1)
    return cosa, sina, cosk, sink, jnp.cos(ang_b).T, jnp.sin(ang_b).T


def _rotate_half_matrix():
    r = np.zeros((A_WIDTH, A_WIDTH), np.float32)
    half = A_ROT_DIM // 2
    for hd in range(A_HEADS):
        for i in range(half):
            r[hd * HEAD_DIM + half + i, hd * HEAD_DIM + i] = -1.0
            r[hd * HEAD_DIM + i, hd * HEAD_DIM + half + i] = 1.0
    return jnp.asarray(r, BF16)


def _dft_constants(seq):
    lo_n = 64
    hi_n = seq // lo_n
    k = jnp.arange(seq, dtype=jnp.int32)[:, None]
    ang_hi = ((k * jnp.arange(hi_n // 2, dtype=jnp.int32)[None, :]) % hi_n).astype(F32) * (2.0 * np.pi / hi_n)
    ang_lo = ((k * jnp.arange(lo_n, dtype=jnp.int32)[None, :]) % seq).astype(F32) * (2.0 * np.pi / seq)
    ch, sh = jnp.cos(ang_hi)[:, :, None], jnp.sin(ang_hi)[:, :, None]
    cl, sl = jnp.cos(ang_lo)[:, None, :], jnp.sin(ang_lo)[:, None, :]
    scale = seq ** -0.5
    cmat = ((ch * cl - sh * sl) * scale).astype(BF16).reshape(seq, seq // 2)
    nsmat = ((sh * cl + ch * sl) * -scale).astype(BF16).reshape(seq, seq // 2)
    c = np.arange(C_GROUP_DIM)
    angc = 2.0 * np.pi * ((c[:, None] * c[None, :]) % C_GROUP_DIM) / C_GROUP_DIM
    eye = np.eye(C_GROUPS)
    cs = np.concatenate([np.kron(eye, np.cos(angc)), np.kron(eye, np.sin(angc))], axis=1)
    return cmat, nsmat, jnp.asarray(cs * C_GROUP_DIM ** -0.5, BF16)


def _layer_weights(w_in, w_uq, w_ukv, w_out):
    d = w_in.shape[0]
    c_kpe = 3 * A_WIDTH + Q_LORA + KV_LORA
    w_kpe = w_in[:, c_kpe:c_kpe + B_ROPE]
    half = B_ROPE // 2
    w_kpe_rot = jnp.concatenate([-w_kpe[:, half:], w_kpe[:, :half]], axis=1)
    zl = jnp.zeros((d, B_NOPE), w_in.dtype)
    zr = jnp.zeros((d, HEAD_PAD - B_NOPE - B_ROPE), w_in.dtype)
    wm = jnp.concatenate([w_in[:, :c_kpe], zl, w_kpe, zr, zl, w_kpe_rot, zr,
                          w_in[:, c_kpe + B_ROPE:]], axis=1).astype(BF16)
    uq = w_uq.reshape(Q_LORA, B_HEADS, B_NOPE + B_ROPE)
    uq = jnp.pad(uq, ((0, 0), (0, 0), (0, HEAD_PAD - B_NOPE - B_ROPE)))
    wuq_t = uq.reshape(Q_LORA, B_HEADS * HEAD_PAD).T.astype(BF16)
    ukv = w_ukv.reshape(KV_LORA, B_HEADS, B_NOPE + B_V)
    wkn = jnp.pad(ukv[:, :, :B_NOPE], ((0, 0), (0, 0), (0, HEAD_PAD - B_NOPE)))
    wkn = wkn.reshape(KV_LORA, B_HEADS * HEAD_PAD).astype(BF16)
    wv_t = ukv[:, :, B_NOPE:].reshape(KV_LORA, B_WIDTH).T.astype(BF16)
    wo = w_out.astype(BF16)
    return wm, wuq_t, wkn, wv_t, wo[:A_WIDTH], wo[A_WIDTH:A_WIDTH + B_WIDTH], wo[A_WIDTH + B_WIDTH:]


def kernel(x, attn_norm, w_in, q_norm, w_uq, kv_norm, w_ukv, mix_gain, w_out, ffn_norm,
           w_ffn_gate, w_ffn_up, w_ffn_down, w_router, w_exp_gate, w_exp_up, w_exp_down,
           final_norm):
    batch, seq, d = x.shape
    depth = w_in.shape[0]
    t = batch * seq
    assert seq % (max(DILATIONS) * 2 * WIN_SUB) == 0
    tabs = _rope_tables(seq)
    rot = _rotate_half_matrix()
    cmat, nsmat, cs = _dft_constants(seq)
    row = lambda v: v.reshape(1, -1)

    x2 = x.reshape(t, d)
    for l in range(depth):
        wm, wuq_t, wkn, wv_t, wa, wb, wc = _layer_weights(w_in[l], w_uq[l], w_ukv[l], w_out[l])
        qkv_views, (qbt, kb, vbt, xcs, xcs_flip) = _proj_call(
            x2, row(attn_norm[l]), wm, rot, tabs, row(q_norm[l]), wuq_t, row(kv_norm[l]), wkn, wv_t,
            cs, batch, seq)
        win = [_win_call(*(a.reshape(batch, seq // dil, dil * A_WIDTH) for a in qkv), batch, seq, dil)
               for dil, qkv in zip(DILATIONS, qkv_views)]
        ob = _mla_call(qbt, kb, vbt, batch, seq)
        oc = _dft_call(cmat, nsmat, xcs, xcs_flip, batch, seq)
        g = mix_gain[l]
        x2 = _out_call(x2, [w[0] for w in win], [w[1] for w in win], ob, oc,
                       row(g[:A_WIDTH]), row(g[A_WIDTH:A_WIDTH + B_WIDTH]), row(g[A_WIDTH + B_WIDTH:]),
                       wa, wb, wc, batch, seq)
        i = l // 2
        if l % 2 == 0:
            x2 = _ffn_call(x2, row(ffn_norm[l]), w_ffn_gate[i].astype(BF16), w_ffn_up[i].astype(BF16),
                           w_ffn_down[i].astype(BF16))
            if l == depth - 1:
                x2 = _final_norm_call(x2, row(final_norm))
        else:
            idx, gates, h_tiles = _router_call(x2, row(ffn_norm[l]), w_router[i].T)
            bexp, row_tok, row_dst, n_out_rows = _route_plan(idx, t)
            parts = d // LANES
            y_tiles = _expert_call(h_tiles, d, w_exp_gate[i].astype(BF16), w_exp_up[i].astype(BF16),
                                   w_exp_down[i].astype(BF16), bexp, row_tok * parts, row_dst * parts,
                                   n_out_rows)
            x2 = _combine_call(x2, y_tiles, gates.T, row(final_norm), final=(l == depth - 1))
    return x2.reshape(batch, seq, d)
```

```python
import functools
import math

import jax
import jax.numpy as jnp
import numpy as np
from jax import lax
from jax.experimental import pallas as pl
from jax.experimental.pallas import tpu as pltpu

F32 = jnp.float32
BF16 = jnp.bfloat16

HEAD_DIM = 64
A_HEADS = 6
A_WIDTH = A_HEADS * HEAD_DIM
A_ROT_DIM = HEAD_DIM // 4
DILATIONS = (1, 4, 16)
WINDOW_RADIUS = 64
B_HEADS = 6
B_NOPE = 64
B_ROPE = 32
B_V = 64
B_WIDTH = B_HEADS * B_V
Q_LORA = 384
KV_LORA = 128
C_GROUPS = 4
C_GROUP_DIM = 64
C_WIDTH = C_GROUPS * C_GROUP_DIM
ROPE_THETA = 500000.0
N_EXPERTS = 8
TOP_K = 2
RMS_EPS = 1e-6
NEG_INF = -1e30

LANES = 128
BF16_SUBLANES = 16
HEAD_PAD = 128
VMEM_LIMIT_BYTES = 56 * 1024 * 1024

TM_PROJ = 512
TQ_WIN = 1024
WIN_SUB = 128
TQ_MLA = 512
TK_MLA = 256
TM_DFT = 1024
TK_DFT = 512
TM_FFN = 512
TM_ROUTER = 1024
EXP_STEPS = 7
TM_EXP = 128 * EXP_STEPS


def _cparams(sem):
    return pltpu.CompilerParams(dimension_semantics=sem, vmem_limit_bytes=VMEM_LIMIT_BYTES)


def _rms(x, g):
    return x * lax.rsqrt(jnp.mean(x * x, axis=-1, keepdims=True) + RMS_EPS) * g


def _dot(a, b):
    return jnp.dot(a, b, preferred_element_type=F32)


def _dot_nt(a, b):
    return lax.dot_general(a, b, (((1,), (1,)), ((), ())), preferred_element_type=F32)


def _split_residues(scr, val, nat_ref, view_refs):
    tm = val.shape[0]
    nat_ref[...] = val.astype(BF16)
    for c in range(A_WIDTH // LANES):
        scr[c] = val[:, c * LANES:(c + 1) * LANES]
    for dil, ref in zip(DILATIONS[1:], view_refs):
        for r in range(dil):
            for c in range(A_WIDTH // LANES):
                col = r * A_WIDTH + c * LANES
                ref[0, :, col:col + LANES] = scr[c, pl.ds(r, tm // dil, stride=dil), :].astype(BF16)


def _proj_body(x_ref, g_ref, wm_ref, rot_ref, cosa_ref, sina_ref, cosk_ref, sink_ref,
               cosq_ref, sinq_ref, qn_ref, wuq_ref, kvn_ref, wkn_ref, wv_ref, cs_ref, flip_ref,
               qa_ref, qa4_ref, qa16_ref, ka_ref, ka4_ref, ka16_ref, va_ref, va4_ref, va16_ref,
               qbt_ref, kb_ref, vbt_ref, xcs_ref, xcs_flip_ref, scr):
    h = _rms(x_ref[...], g_ref[...]).astype(BF16)
    z = _dot(h, wm_ref[...])
    o = 0
    qa = z[:, o:o + A_WIDTH]; o += A_WIDTH
    ka = z[:, o:o + A_WIDTH]; o += A_WIDTH
    va = z[:, o:o + A_WIDTH]; o += A_WIDTH
    cq = z[:, o:o + Q_LORA]; o += Q_LORA
    ckv = z[:, o:o + KV_LORA]; o += KV_LORA
    kpe = z[:, o:o + HEAD_PAD]; o += HEAD_PAD
    kpe_rot = z[:, o:o + HEAD_PAD]; o += HEAD_PAD
    fc = z[:, o:o + C_WIDTH]

    cosa = cosa_ref[...]
    sina = sina_ref[...]
    rot = rot_ref[...]
    _split_residues(scr, qa * cosa + _dot(qa.astype(BF16), rot) * sina, qa_ref, (qa4_ref, qa16_ref))
    _split_residues(scr, ka * cosa + _dot(ka.astype(BF16), rot) * sina, ka_ref, (ka4_ref, ka16_ref))
    _split_residues(scr, va, va_ref, (va4_ref, va16_ref))

    cqn = _rms(cq, qn_ref[...]).astype(BF16)
    qbt = _dot_nt(wuq_ref[...], cqn) * ((B_NOPE + B_ROPE) ** -0.5 * math.log2(math.e))
    cosq = cosq_ref[...]
    sinq = sinq_ref[...]
    half = B_ROPE // 2
    for hd in range(B_HEADS):
        r0 = hd * HEAD_PAD
        x1 = qbt[r0 + B_NOPE:r0 + B_NOPE + half]
        x2 = qbt[r0 + B_NOPE + half:r0 + B_NOPE + B_ROPE]
        qbt_ref[0, r0:r0 + B_NOPE, :] = qbt[r0:r0 + B_NOPE].astype(BF16)
        qbt_ref[0, r0 + B_NOPE:r0 + B_NOPE + half, :] = (x1 * cosq - x2 * sinq).astype(BF16)
        qbt_ref[0, r0 + B_NOPE + half:r0 + B_NOPE + B_ROPE, :] = (x2 * cosq + x1 * sinq).astype(BF16)
        qbt_ref[0, r0 + B_NOPE + B_ROPE:r0 + HEAD_PAD, :] = jnp.zeros(
            (HEAD_PAD - B_NOPE - B_ROPE, qbt.shape[1]), BF16)

    ckvn = _rms(ckv, kvn_ref[...]).astype(BF16)
    kn = _dot(ckvn, wkn_ref[...])
    kpe_r = kpe * cosk_ref[...] + kpe_rot * sink_ref[...]
    for hd in range(B_HEADS):
        kb_ref[0, hd] = (kn[:, hd * HEAD_PAD:(hd + 1) * HEAD_PAD] + kpe_r).astype(BF16)
    vbt_ref[0] = _dot_nt(wv_ref[...], ckvn).astype(BF16)

    xcs = _dot(fc.astype(BF16), cs_ref[...]).astype(BF16)
    xcs_ref[0] = xcs
    xcs_flip_ref[0] = _dot(flip_ref[...], xcs).astype(BF16)


def _proj_call(x2, g, wm, rot, tabs, qn, wuq, kvn, wkn, wv, cs, batch, seq):
    t, d = x2.shape
    tm = min(TM_PROJ, seq)
    ns = seq // tm
    nm = wm.shape[1]
    cosa, sina, cosk, sink, cosq, sinq = tabs
    row = lambda i: (i, 0)
    pos = lambda i: (i % ns, 0)
    post = lambda i: (0, i % ns)
    const = lambda i: (0, 0)
    in_specs = [
        pl.BlockSpec((tm, d), row),
        pl.BlockSpec((1, d), const),
        pl.BlockSpec((d, nm), const),
        pl.BlockSpec((A_WIDTH, A_WIDTH), const),
        pl.BlockSpec((tm, A_WIDTH), pos),
        pl.BlockSpec((tm, A_WIDTH), pos),
        pl.BlockSpec((tm, HEAD_PAD), pos),
        pl.BlockSpec((tm, HEAD_PAD), pos),
        pl.BlockSpec((B_ROPE // 2, tm), post),
        pl.BlockSpec((B_ROPE // 2, tm), post),
        pl.BlockSpec((1, Q_LORA), const),
        pl.BlockSpec((B_HEADS * HEAD_PAD, Q_LORA), const),
        pl.BlockSpec((1, KV_LORA), const),
        pl.BlockSpec((KV_LORA, B_HEADS * HEAD_PAD), const),
        pl.BlockSpec((B_WIDTH, KV_LORA), const),
        pl.BlockSpec((C_WIDTH, 2 * C_WIDTH), const),
        pl.BlockSpec((tm, tm), const),
    ]
    bpos = lambda i: (i // ns, i % ns, 0)
    a_shapes, a_specs = [], []
    for _ in range(3):
        a_shapes.append(jax.ShapeDtypeStruct((t, A_WIDTH), BF16))
        a_specs.append(pl.BlockSpec((tm, A_WIDTH), row))
        for dil in DILATIONS[1:]:
            a_shapes.append(jax.ShapeDtypeStruct((batch, seq // dil, dil * A_WIDTH), BF16))
            a_specs.append(pl.BlockSpec((1, tm // dil, dil * A_WIDTH), bpos))
    out_shape = (
        *a_shapes,
        jax.ShapeDtypeStruct((batch, B_HEADS * HEAD_PAD, seq), BF16),
        jax.ShapeDtypeStruct((batch, B_HEADS, seq, HEAD_PAD), BF16),
        jax.ShapeDtypeStruct((batch, B_WIDTH, seq), BF16),
        jax.ShapeDtypeStruct((batch, seq, 2 * C_WIDTH), BF16),
        jax.ShapeDtypeStruct((batch, seq, 2 * C_WIDTH), BF16),
    )
    out_specs = (
        *a_specs,
        pl.BlockSpec((1, B_HEADS * HEAD_PAD, tm), lambda i: (i // ns, 0, i % ns)),
        pl.BlockSpec((1, B_HEADS, tm, HEAD_PAD), lambda i: (i // ns, 0, i % ns, 0)),
        pl.BlockSpec((1, B_WIDTH, tm), lambda i: (i // ns, 0, i % ns)),
        pl.BlockSpec((1, tm, 2 * C_WIDTH), bpos),
        pl.BlockSpec((1, tm, 2 * C_WIDTH), lambda i: (i // ns, ns - 1 - i % ns, 0)),
    )
    flip = jnp.asarray(np.eye(tm, dtype=np.float32)[::-1], BF16)
    outs = pl.pallas_call(
        _proj_body, grid=(t // tm,), in_specs=in_specs, out_specs=out_specs, out_shape=out_shape,
        scratch_shapes=[pltpu.VMEM((A_WIDTH // LANES, tm, LANES), F32)],
        compiler_params=_cparams(("arbitrary",)), name="proj",
    )(x2, g, wm, rot, cosa, sina, cosk, sink, cosq, sinq, qn, wuq, kvn, wkn, wv, cs, flip)
    nd = len(DILATIONS)
    qkv_views = [tuple(outs[a * nd + j] for a in range(3)) for j in range(nd)]
    return qkv_views, outs[3 * nd:]


def _win_body(q_ref, kp_ref, km_ref, kn_ref, vp_ref, vm_ref, vn_ref, o_ref, l_ref, *, seg_len, tq):
    j = pl.program_id(2)
    span = WIN_SUB + 2 * WINDOW_RADIUS
    lane = lax.broadcasted_iota(jnp.int32, (1, LANES), 1)
    lo_half = lane < HEAD_DIM
    rel = (lax.broadcasted_iota(jnp.int32, (WIN_SUB, span), 1) - WINDOW_RADIUS
           - lax.broadcasted_iota(jnp.int32, (WIN_SUB, span), 0))
    band = jnp.abs(rel) <= WINDOW_RADIUS
    kcol = lax.broadcasted_iota(jnp.int32, (1, span), 1)
    n_sub = tq // WIN_SUB
    biases = []
    for sb in range(n_sub):
        kpos = j * tq + sb * WIN_SUB - WINDOW_RADIUS + kcol
        biases.append(jnp.where(band & (kpos >= 0) & (kpos < seg_len), 0.0, NEG_INF))
    l_ref[...] = jnp.zeros_like(l_ref)
    for pair in range(A_HEADS // 2):
        cs = slice(pair * LANES, (pair + 1) * LANES)
        q2 = q_ref[0, :, cs] * (HEAD_DIM ** -0.5 * math.log2(math.e))
        qh = (jnp.where(lo_half, q2, 0).astype(BF16), jnp.where(lo_half, 0, q2).astype(BF16))
        kcat = jnp.concatenate([kp_ref[0, :, cs], km_ref[0, :, cs], kn_ref[0, :, cs]], axis=0)
        vcat = jnp.concatenate([vp_ref[0, :, cs], vm_ref[0, :, cs], vn_ref[0, :, cs]], axis=0)
        for sb in range(n_sub):
            r0 = sb * WIN_SUB
            ks = kcat[r0:r0 + span]
            vs = vcat[r0:r0 + span]
            outs = []
            for hh in range(2):
                s = _dot_nt(qh[hh][r0:r0 + WIN_SUB], ks) + biases[sb]
                m = jnp.max(s, axis=-1, keepdims=True)
                p = jnp.exp2(s - m)
                den = jnp.sum(p, axis=-1, keepdims=True)
                outs.append(_dot(p.astype(BF16), vs) / den)
                head = 2 * pair + hh
                l_ref[0, r0:r0 + WIN_SUB, head:head + 1] = m * math.log(2.0) + jnp.log(den)
            o_ref[0, r0:r0 + WIN_SUB, cs] = jnp.where(lo_half, outs[0], outs[1]).astype(BF16)


def _win_call(q, k, v, batch, seq, dil):
    seg = seq // dil
    tq = min(TQ_WIN, seg)
    nb = tq // WINDOW_RADIUS
    last = seg // WINDOW_RADIUS - 1
    main = pl.BlockSpec((1, tq, A_WIDTH), lambda b, r, j: (b, j, r))
    prev = pl.BlockSpec((1, WINDOW_RADIUS, A_WIDTH), lambda b, r, j: (b, jnp.maximum(j * nb - 1, 0), r))
    nxt = pl.BlockSpec((1, WINDOW_RADIUS, A_WIDTH), lambda b, r, j: (b, jnp.minimum((j + 1) * nb, last), r))
    return pl.pallas_call(
        functools.partial(_win_body, seg_len=seg, tq=tq),
        grid=(batch, dil, seg // tq),
        in_specs=[main, prev, main, nxt, prev, main, nxt],
        out_specs=(main, pl.BlockSpec((1, tq, LANES), lambda b, r, j: (b, j, r))),
        out_shape=(jax.ShapeDtypeStruct((batch, seg, dil * A_WIDTH), BF16),
                   jax.ShapeDtypeStruct((batch, seg, dil * LANES), F32)),
        compiler_params=_cparams(("arbitrary", "arbitrary", "arbitrary")), name=f"win_d{dil}",
    )(q, k, k, k, v, v, v)


def _mla_body(qt_ref, k_ref, vt_ref, o_ref, sa, sb, pa, pb, acc_ref, *, seq, tk):
    tq = qt_ref.shape[2]
    nk = seq // tk
    qts = [qt_ref[0, h * HEAD_PAD:(h + 1) * HEAD_PAD, :] for h in range(B_HEADS)]
    ones = jnp.ones((BF16_SUBLANES, tk), BF16)

    def scores(h, kt, dst):
        start = kt * tk if isinstance(kt, int) else pl.multiple_of(kt * tk, tk)
        dst[h] = _dot(k_ref[0, h, pl.ds(start, tk), :], qts[h])

    def values(h, kt, src, alpha):
        start = kt * tk if isinstance(kt, int) else pl.multiple_of(kt * tk, tk)
        vt = jnp.concatenate([vt_ref[0, h * B_V:(h + 1) * B_V, pl.ds(start, tk)], ones], axis=0)
        acc_ref[h] = alpha * acc_ref[h] + _dot(vt, src[h])

    def softmax(h, src, dst, m):
        st = src[h]
        m_new = jnp.maximum(m, jnp.max(st, axis=0, keepdims=True))
        dst[h] = jnp.exp2(st - m_new).astype(BF16)
        return m_new, jnp.exp2(m - m_new)

    for h in range(B_HEADS):
        scores(h, 0, sa)
        acc_ref[h] = jnp.zeros((B_V + BF16_SUBLANES, tq), F32)
    pb[...] = jnp.zeros_like(pb)

    def step(i, carry):
        out = []
        for h in range(B_HEADS):
            m, alpha_prev = carry[h]
            t0 = 2 * i
            scores(h, t0 + 1, sb)
            values(h, jnp.maximum(t0 - 1, 0), pb, alpha_prev)
            m, alpha0 = softmax(h, sa, pa, m)
            scores(h, jnp.minimum(t0 + 2, nk - 1), sa)
            values(h, t0, pa, alpha0)
            m, alpha1 = softmax(h, sb, pb, m)
            out.append((m, alpha1))
        return tuple(out)

    init = tuple((jnp.full((1, tq), NEG_INF, F32), jnp.ones((1, tq), F32)) for _ in range(B_HEADS))
    res = lax.fori_loop(0, nk // 2, step, init)
    outs = []
    for h in range(B_HEADS):
        values(h, nk - 1, pb, res[h][1])
        acc = acc_ref[h]
        outs.append(acc[:B_V] / acc[B_V:B_V + 1])
    o_ref[...] = jnp.concatenate(outs, axis=0).T.astype(BF16)


def _mla_call(qbt, kb, vbt, batch, seq):
    tq = min(TQ_MLA, seq)
    tk = min(TK_MLA, seq)
    nq = seq // tq
    assert (seq // tk) % 2 == 0
    return pl.pallas_call(
        functools.partial(_mla_body, seq=seq, tk=tk),
        grid=(batch, nq),
        in_specs=[
            pl.BlockSpec((1, B_HEADS * HEAD_PAD, tq), lambda b, qi: (b, 0, qi)),
            pl.BlockSpec((1, B_HEADS, seq, HEAD_PAD), lambda b, qi: (b, 0, 0, 0)),
            pl.BlockSpec((1, B_WIDTH, seq), lambda b, qi: (b, 0, 0)),
        ],
        out_specs=pl.BlockSpec((tq, B_WIDTH), lambda b, qi: (b * nq + qi, 0)),
        out_shape=jax.ShapeDtypeStruct((batch * seq, B_WIDTH), BF16),
        scratch_shapes=[pltpu.VMEM((B_HEADS, tk, tq), F32), pltpu.VMEM((B_HEADS, tk, tq), F32),
                        pltpu.VMEM((B_HEADS, tk, tq), BF16), pltpu.VMEM((B_HEADS, tk, tq), BF16),
                        pltpu.VMEM((B_HEADS, B_V + BF16_SUBLANES, tq), F32)],
        compiler_params=_cparams(("arbitrary", "arbitrary")), name="mla_attn",
    )(qbt, kb, vbt)


def _dft_body(c_ref, ns_ref, x_ref, xf_ref, xfp_ref, xh_ref, o_ref, acc_ref, *, seq):
    k = pl.program_id(1)
    tm = c_ref.shape[0]
    tk = x_ref.shape[1]

    @pl.when(k == 0)
    def _():
        row = lax.broadcasted_iota(jnp.int32, (tm, 1), 0)
        sign = (1 - 2 * (row & 1)).astype(F32) * (seq ** -0.5)
        for b in range(x_ref.shape[0]):
            acc_ref[b] = sign * xh_ref[b, 0:1, 0:C_WIDTH].astype(F32)

    c = c_ref[...]
    ns = ns_ref[...]
    first_row = lax.broadcasted_iota(jnp.int32, (tk, 1), 0) == 0
    has_prev = (k > 0).astype(F32)
    for b in range(x_ref.shape[0]):
        x = x_ref[b].astype(F32)
        carry_in = xfp_ref[b, 7:8, :].astype(F32) * has_prev
        xr = jnp.where(first_row, carry_in, pltpu.roll(xf_ref[b].astype(F32), 1, axis=0))
        sym = (x[:, 0:C_WIDTH] + xr[:, 0:C_WIDTH]).astype(BF16)
        anti = (x[:, C_WIDTH:2 * C_WIDTH] - xr[:, C_WIDTH:2 * C_WIDTH]).astype(BF16)
        acc_ref[b] += _dot(c, sym) + _dot(ns, anti)

    @pl.when(k == pl.num_programs(1) - 1)
    def _():
        o_ref[...] = acc_ref[...].astype(o_ref.dtype)


def _dft_call(cmat, nsmat, x3, x3_flip, batch, seq):
    half = seq // 2
    tm = min(TM_DFT, seq)
    tk = min(TK_DFT, half)
    rows8 = tk // 8
    blk = lambda rows, index: pl.BlockSpec((batch, rows, 2 * C_WIDTH), index)
    y = pl.pallas_call(
        functools.partial(_dft_body, seq=seq), grid=(seq // tm, half // tk),
        in_specs=[pl.BlockSpec((tm, tk), lambda i, k: (i, k)),
                  pl.BlockSpec((tm, tk), lambda i, k: (i, k)),
                  blk(tk, lambda i, k: (0, k, 0)),
                  blk(tk, lambda i, k: (0, k, 0)),
                  blk(8, lambda i, k: (0, jnp.maximum(k * rows8 - 1, 0), 0)),
                  blk(8, lambda i, k: (0, half // 8, 0))],
        out_specs=pl.BlockSpec((batch, tm, C_WIDTH), lambda i, k: (0, i, 0)),
        out_shape=jax.ShapeDtypeStruct((batch, seq, C_WIDTH), BF16),
        scratch_shapes=[pltpu.VMEM((batch, tm, C_WIDTH), F32)],
        compiler_params=_cparams(("arbitrary", "arbitrary")), name="seq_dft",
    )(cmat, nsmat, x3, x3_flip, x3_flip, x3)
    return y.reshape(batch * seq, C_WIDTH)


def _merge_residues(scr, ref, dil):
    slabs, tm, _ = scr.shape
    for r in range(dil):
        for c in range(slabs):
            col = (r * slabs + c) * LANES
            scr[c, pl.ds(r, tm // dil, stride=dil), :] = ref[0, :, col:col + LANES].astype(F32)
    return jnp.concatenate([scr[c] for c in range(slabs)], axis=1)


def _out_body(x_ref, o1_ref, o4_ref, o16_ref, l1_ref, l4_ref, l16_ref, ob_ref, oc_ref,
              ga_ref, gb_ref, gc_ref, wa_ref, wb_ref, wc_ref, spread_ref, y_ref, so4, sl4, so16, sl16):
    l1 = l1_ref[0]
    l4 = _merge_residues(sl4, l4_ref, DILATIONS[1])
    l16 = _merge_residues(sl16, l16_ref, DILATIONS[2])
    o4 = _merge_residues(so4, o4_ref, DILATIONS[1])
    o16 = _merge_residues(so16, o16_ref, DILATIONS[2])
    m = jnp.maximum(jnp.maximum(l1, l4), l16)
    e1, e4, e16 = jnp.exp(l1 - m), jnp.exp(l4 - m), jnp.exp(l16 - m)
    inv = 1.0 / (e1 + e4 + e16)

    def per_lane(w):
        hi = w.astype(BF16)
        lo = (w - hi.astype(F32)).astype(BF16)
        return _dot(hi, spread_ref[...]) + _dot(lo, spread_ref[...])

    oa = (o1_ref[0].astype(F32) * per_lane(e1 * inv) + o4 * per_lane(e4 * inv)
          + o16 * per_lane(e16 * inv))
    na = _rms(oa, ga_ref[...]).astype(BF16)
    nb = _rms(ob_ref[...].astype(F32), gb_ref[...]).astype(BF16)
    nc = _rms(oc_ref[...].astype(F32), gc_ref[...]).astype(BF16)
    y_ref[...] = (x_ref[...] + _dot(na, wa_ref[...]) + _dot(nb, wb_ref[...]) + _dot(nc, wc_ref[...]))


def _out_call(x2, oa, la, ob, oc, ga, gb, gc, wa, wb, wc, batch, seq):
    t, d = x2.shape
    tm = min(TM_PROJ, seq)
    ns = seq // tm
    row = lambda i: (i, 0)
    const = lambda i: (0, 0)
    wide = lambda w: pl.BlockSpec((tm, w), row)
    view = lambda dil, w: pl.BlockSpec((1, tm // dil, dil * w), lambda i: (i // ns, i % ns, 0))
    slab = lambda w: pltpu.VMEM((w // LANES, tm, LANES), F32)
    spread = np.zeros((LANES, A_WIDTH), np.float32)
    for hd in range(A_HEADS):
        spread[hd, hd * HEAD_DIM:(hd + 1) * HEAD_DIM] = 1.0
    return pl.pallas_call(
        _out_body, grid=(t // tm,),
        in_specs=[wide(d)] + [view(dil, A_WIDTH) for dil in DILATIONS]
                 + [view(dil, LANES) for dil in DILATIONS] + [wide(B_WIDTH), wide(C_WIDTH),
                  pl.BlockSpec((1, A_WIDTH), const), pl.BlockSpec((1, B_WIDTH), const),
                  pl.BlockSpec((1, C_WIDTH), const),
                  pl.BlockSpec((A_WIDTH, d), const), pl.BlockSpec((B_WIDTH, d), const),
                  pl.BlockSpec((C_WIDTH, d), const), pl.BlockSpec((LANES, A_WIDTH), const)],
        out_specs=wide(d),
        out_shape=jax.ShapeDtypeStruct((t, d), F32),
        scratch_shapes=[slab(A_WIDTH), slab(LANES), slab(A_WIDTH), slab(LANES)],
        compiler_params=_cparams(("arbitrary",)), name="out_proj",
    )(x2, oa[0], oa[1], oa[2], la[0], la[1], la[2], ob, oc, ga, gb, gc, wa, wb, wc,
      jnp.asarray(spread, BF16))


def _ffn_body(x_ref, g_ref, wg_ref, wu_ref, wd_ref, y_ref):
    x = x_ref[...]
    h = _rms(x, g_ref[...]).astype(BF16)
    gate = _dot(h, wg_ref[...])
    up = _dot(h, wu_ref[...])
    y_ref[...] = x + _dot((jax.nn.silu(gate) * up).astype(BF16), wd_ref[...])


def _ffn_call(x2, g, wg, wu, wd):
    t, d = x2.shape
    d_ff = wg.shape[1]
    tm = min(TM_FFN, t)
    resident = lambda shape: pl.BlockSpec(shape, lambda i: (0, 0), pipeline_mode=pl.Buffered(1))
    return pl.pallas_call(
        _ffn_body, grid=(t // tm,),
        in_specs=[pl.BlockSpec((tm, d), lambda i: (i, 0)),
                  pl.BlockSpec((1, d), lambda i: (0, 0)),
                  resident((d, d_ff)), resident((d, d_ff)), resident((d_ff, d))],
        out_specs=pl.BlockSpec((tm, d), lambda i: (i, 0)),
        out_shape=jax.ShapeDtypeStruct((t, d), F32),
        compiler_params=_cparams(("arbitrary",)), name="dense_ffn",
    )(x2, g, wg, wu, wd)


def _to_token_tiles(ref, val):
    rows, d = val.shape
    parts = d // LANES
    for c in range(parts):
        ref[pl.ds(c, rows, stride=parts), :] = val[:, c * LANES:(c + 1) * LANES]


def _from_token_tiles(ref, rows):
    parts = ref.shape[0] // rows
    return jnp.concatenate([ref[pl.ds(c, rows, stride=parts), :] for c in range(parts)], axis=1)


def _router_body(x_ref, g_ref, wr_ref, idx_ref, gate_ref, htile_ref):
    h = _rms(x_ref[...], g_ref[...])
    _to_token_tiles(htile_ref, h)
    logits = lax.dot_general(wr_ref[...], h, (((1,), (1,)), ((), ())),
                             precision=lax.Precision.HIGHEST, preferred_element_type=F32)
    eid = lax.broadcasted_iota(jnp.int32, logits.shape, 0)
    m1 = jnp.max(logits, axis=0, keepdims=True)
    i1 = jnp.min(jnp.where(logits == m1, eid, N_EXPERTS), axis=0, keepdims=True)
    rest = jnp.where(eid == i1, -jnp.inf, logits)
    m2 = jnp.max(rest, axis=0, keepdims=True)
    i2 = jnp.min(jnp.where(rest == m2, eid, N_EXPERTS), axis=0, keepdims=True)
    e2 = jnp.exp(m2 - m1)
    idx_ref[...] = jnp.concatenate([i1, i2], axis=0)
    gate_ref[...] = jnp.concatenate([1.0 / (1.0 + e2), e2 / (1.0 + e2)], axis=0)


def _router_call(x2, g, wr_t):
    t, d = x2.shape
    tm = min(TM_ROUTER, t)
    return pl.pallas_call(
        _router_body, grid=(t // tm,),
        in_specs=[pl.BlockSpec((tm, d), lambda i: (i, 0)),
                  pl.BlockSpec((1, d), lambda i: (0, 0)),
                  pl.BlockSpec((N_EXPERTS, d), lambda i: (0, 0))],
        out_specs=(pl.BlockSpec((TOP_K, tm), lambda i: (0, i)),
                   pl.BlockSpec((TOP_K, tm), lambda i: (0, i)),
                   pl.BlockSpec((tm * d // LANES, LANES), lambda i: (i, 0))),
        out_shape=(jax.ShapeDtypeStruct((TOP_K, t), jnp.int32),
                   jax.ShapeDtypeStruct((TOP_K, t), F32),
                   jax.ShapeDtypeStruct((t * d // LANES, LANES), F32)),
        compiler_params=_cparams(("arbitrary",)), name="router",
    )(x2, g, wr_t)


def _expert_body(bexp_ref, tok_cur, tok_next, dst_prev, dst_cur, h_hbm, wg_ref, wu_ref, wd_ref,
                 y_hbm, xb0, xb1, hbuf, acc_ref, yb0, yb1, gsem, ssem):
    i = pl.program_id(0)
    f = pl.program_id(1)
    nb = pl.num_programs(0)
    nf = pl.num_programs(1)
    rows = hbuf.shape[0]
    parts = xb0.shape[0] // rows
    chunk = rows // EXP_STEPS

    def variant(xcur, xnxt, ycur, yprv, s, compute):
        def tile(ref, first_row):
            if not isinstance(first_row, int):
                first_row = pl.multiple_of(first_row, parts)
            return ref.at[pl.ds(first_row, parts)]

        def row_in(r, src, buf, sem):
            return pltpu.make_async_copy(tile(h_hbm, src), tile(buf, r * parts), sem)

        def row_out(r, dst, buf, sem):
            return pltpu.make_async_copy(tile(buf, r * parts), tile(y_hbm, dst), sem)

        def all_rows(fn):
            def body(r, c):
                fn(r)
                return c
            lax.fori_loop(0, rows, body, 0, unroll=8)

        @pl.when(f == 0)
        def _():
            @pl.when(i == 0)
            def _():
                all_rows(lambda r: row_in(r, tok_cur[0, 0, r], xcur, gsem.at[s]).start())
                yprv[...] = jnp.zeros_like(yprv)

            all_rows(lambda r: row_in(r, 0, xcur, gsem.at[s]).wait())
            if compute:
                hbuf[...] = _from_token_tiles(xcur, rows).astype(BF16)
                acc_ref[...] = jnp.zeros_like(acc_ref)

        base = pl.multiple_of(f * chunk, chunk)
        for k in range(chunk):
            r = base + k
            row_in(r, tok_next[0, 0, r], xnxt, gsem.at[1 - s]).start()
            row_out(r, dst_prev[0, 0, r], yprv, ssem.at[1 - s]).start()
        if compute:
            h = hbuf[...]
            gate = _dot(h, wg_ref[0])
            up = _dot(h, wu_ref[0])
            acc_ref[...] += _dot((jax.nn.silu(gate) * up).astype(BF16), wd_ref[0])

        @pl.when(f == nf - 1)
        def _():
            if compute:
                _to_token_tiles(ycur, acc_ref[...])
            else:
                ycur[...] = jnp.zeros_like(ycur)
            all_rows(lambda r: row_out(r, 0, yprv, ssem.at[1 - s]).wait())

            @pl.when(i == nb - 1)
            def _():
                all_rows(lambda r: row_out(r, dst_cur[0, 0, r], ycur, ssem.at[s]).start())
                all_rows(lambda r: row_out(r, 0, ycur, ssem.at[s]).wait())
                all_rows(lambda r: row_in(r, 0, xnxt, gsem.at[1 - s]).wait())

    even = i % 2 == 0
    padding_only = bexp_ref[i] >= N_EXPERTS
    for is_even, bufs in ((True, (xb0, xb1, yb0, yb1, 0)), (False, (xb1, xb0, yb1, yb0, 1))):
        for compute in (True, False):
            @pl.when((even == is_even) & (padding_only != compute))
            def _(bufs=bufs, compute=compute):
                variant(*bufs, compute)


def _expert_call(h_tiles, d, wg, wu, wd, bexp, row_tok, row_dst, n_out_rows):
    parts = d // LANES
    n_blocks = bexp.shape[0]
    d_ff = wg.shape[2]
    assert d_ff % EXP_STEPS == 0 and (d_ff // EXP_STEPS) % LANES == 0
    tf = d_ff // EXP_STEPS
    last = n_blocks - 1
    rows_of = lambda which: pl.BlockSpec((1, 1, TM_EXP), which, memory_space=pltpu.SMEM)
    tiles = pltpu.VMEM((TM_EXP * parts, LANES), F32)
    grid_spec = pltpu.PrefetchScalarGridSpec(
        num_scalar_prefetch=1,
        grid=(n_blocks, EXP_STEPS),
        in_specs=[
            rows_of(lambda i, f, be: (i, 0, 0)),
            rows_of(lambda i, f, be: (jnp.minimum(i + 1, last), 0, 0)),
            rows_of(lambda i, f, be: (jnp.maximum(i - 1, 0), 0, 0)),
            rows_of(lambda i, f, be: (i, 0, 0)),
            pl.BlockSpec(memory_space=pl.ANY),
            pl.BlockSpec((1, d, tf), lambda i, f, be: (be[i] % N_EXPERTS, 0, f)),
            pl.BlockSpec((1, d, tf), lambda i, f, be: (be[i] % N_EXPERTS, 0, f)),
            pl.BlockSpec((1, tf, d), lambda i, f, be: (be[i] % N_EXPERTS, f, 0)),
        ],
        out_specs=pl.BlockSpec(memory_space=pl.ANY),
        scratch_shapes=[tiles, tiles, pltpu.VMEM((TM_EXP, d), BF16), pltpu.VMEM((TM_EXP, d), F32),
                        tiles, tiles, pltpu.SemaphoreType.DMA((2,)), pltpu.SemaphoreType.DMA((2,))],
    )
    return pl.pallas_call(
        _expert_body, grid_spec=grid_spec,
        out_shape=jax.ShapeDtypeStruct((n_out_rows * parts, LANES), F32),
        compiler_params=_cparams(("arbitrary", "arbitrary")), name="experts",
    )(bexp, row_tok, row_tok, row_dst, row_dst, h_tiles, wg, wu, wd)


def _combine_body(x_ref, y0_ref, y1_ref, gate_ref, gf_ref, o_ref, *, final):
    g0 = gate_ref[:, 0:1]
    g1 = gate_ref[:, 1:2]
    rows = x_ref.shape[0]
    y = x_ref[...] + (_from_token_tiles(y0_ref, rows) * g0 + _from_token_tiles(y1_ref, rows) * g1)
    o_ref[...] = _rms(y, gf_ref[...]) if final else y


def _combine_call(x2, y_tiles, gates_t, gfinal, final):
    t, d = x2.shape
    tm = min(TM_ROUTER, t)
    nt = t // tm
    parts = d // LANES
    return pl.pallas_call(
        functools.partial(_combine_body, final=final), grid=(nt,),
        in_specs=[pl.BlockSpec((tm, d), lambda i: (i, 0)),
                  pl.BlockSpec((tm * parts, LANES), lambda i: (i, 0)),
                  pl.BlockSpec((tm * parts, LANES), lambda i: (i + nt, 0)),
                  pl.BlockSpec((tm, TOP_K), lambda i: (i, 0)),
                  pl.BlockSpec((1, d), lambda i: (0, 0))],
        out_specs=pl.BlockSpec((tm, d), lambda i: (i, 0)),
        out_shape=jax.ShapeDtypeStruct((t, d), F32),
        compiler_params=_cparams(("arbitrary",)), name="moe_combine",
    )(x2, y_tiles, y_tiles, gates_t, gfinal)


def _final_norm_body(x_ref, g_ref, o_ref):
    o_ref[...] = _rms(x_ref[...], g_ref[...])


def _final_norm_call(x2, g):
    t, d = x2.shape
    tm = min(TM_ROUTER, t)
    return pl.pallas_call(
        _final_norm_body, grid=(t // tm,),
        in_specs=[pl.BlockSpec((tm, d), lambda i: (i, 0)), pl.BlockSpec((1, d), lambda i: (0, 0))],
        out_specs=pl.BlockSpec((tm, d), lambda i: (i, 0)),
        out_shape=jax.ShapeDtypeStruct((t, d), F32),
        compiler_params=_cparams(("arbitrary",)), name="final_norm",
    )(x2, g)


def _route_plan(idx, t):
    n_assign = TOP_K * t
    flat_e = idx.reshape(-1)
    order = jnp.argsort(flat_e).astype(jnp.int32)
    counts = jnp.sum(flat_e[:, None] == jnp.arange(N_EXPERTS, dtype=jnp.int32)[None, :], axis=0,
                     dtype=jnp.int32)
    padded = ((counts + TM_EXP - 1) // TM_EXP) * TM_EXP
    start = jnp.cumsum(counts) - counts
    cum_padded = jnp.cumsum(padded)
    start_padded = cum_padded - padded
    n_blocks = -(-n_assign // TM_EXP) + N_EXPERTS
    block_start = jnp.arange(n_blocks, dtype=jnp.int32) * TM_EXP
    bexp = jnp.minimum(jnp.sum(block_start[:, None] >= cum_padded[None, :], axis=1),
                       N_EXPERTS - 1).astype(jnp.int32)
    off = block_start[:, None] + jnp.arange(TM_EXP, dtype=jnp.int32)[None, :] - start_padded[bexp][:, None]
    valid = off < counts[bexp][:, None]
    a = order[jnp.clip(start[bexp][:, None] + off, 0, n_assign - 1)]
    bexp = bexp + N_EXPERTS * (~jnp.any(valid, axis=1)).astype(jnp.int32)
    pad_rank = (jnp.cumsum((~valid).reshape(-1).astype(jnp.int32)) - 1).reshape(valid.shape)
    row_tok = jnp.where(valid, a % t, 0).astype(jnp.int32)
    row_dst = jnp.where(valid, a, n_assign + pad_rank).astype(jnp.int32)
    n_out_rows = n_blocks * TM_EXP
    return bexp, row_tok.reshape(n_blocks, 1, TM_EXP), row_dst.reshape(n_blocks, 1, TM_EXP), n_out_rows


def _rope_tables(seq):
    pos = jnp.arange(seq, dtype=F32)[:, None]
    half_a = A_ROT_DIM // 2
    ang_a = pos * (ROPE_THETA ** (-jnp.arange(half_a, dtype=F32) / half_a))[None, :]
    one = jnp.ones((seq, HEAD_DIM - A_ROT_DIM), F32)
    cos_head = jnp.concatenate([jnp.cos(ang_a), jnp.cos(ang_a), one], axis=1)
    sin_head = jnp.concatenate([jnp.sin(ang_a), jnp.sin(ang_a), 0 * one], axis=1)
    cosa = jnp.tile(cos_head, (1, A_HEADS))
    sina = jnp.tile(sin_head, (1, A_HEADS))
    half_b = B_ROPE // 2
    ang_b = pos * (ROPE_THETA ** (-jnp.arange(half_b, dtype=F32) / half_b))[None, :]
    zl = jnp.zeros((seq, B_NOPE), F32)
    zr = jnp.zeros((seq, HEAD_PAD - B_NOPE - B_ROPE), F32)
    cosk = jnp.concatenate([zl, jnp.cos(ang_b), jnp.cos(ang_b), zr], axis=1)
    sink = jnp.concatenate([zl, jnp.sin(ang_b), jnp.sin(ang_b), zr], axis=1)
    return cosa, sina, cosk, sink, jnp.cos(ang_b).T, jnp.sin(ang_b).T


def _rotate_half_matrix():
    r = np.zeros((A_WIDTH, A_WIDTH), np.float32)
    half = A_ROT_DIM // 2
    for hd in range(A_HEADS):
        for i in range(half):
            r[hd * HEAD_DIM + half + i, hd * HEAD_DIM + i] = -1.0
            r[hd * HEAD_DIM + i, hd * HEAD_DIM + half + i] = 1.0
    return jnp.asarray(r, BF16)


def _dft_constants(seq):
    lo_n = 64
    hi_n = seq // lo_n
    k = jnp.arange(seq, dtype=jnp.int32)[:, None]
    ang_hi = ((k * jnp.arange(hi_n // 2, dtype=jnp.int32)[None, :]) % hi_n).astype(F32) * (2.0 * np.pi / hi_n)
    ang_lo = ((k * jnp.arange(lo_n, dtype=jnp.int32)[None, :]) % seq).astype(F32) * (2.0 * np.pi / seq)
    ch, sh = jnp.cos(ang_hi)[:, :, None], jnp.sin(ang_hi)[:, :, None]
    cl, sl = jnp.cos(ang_lo)[:, None, :], jnp.sin(ang_lo)[:, None, :]
    scale = seq ** -0.5
    cmat = ((ch * cl - sh * sl) * scale).astype(BF16).reshape(seq, seq // 2)
    nsmat = ((sh * cl + ch * sl) * -scale).astype(BF16).reshape(seq, seq // 2)
    c = np.arange(C_GROUP_DIM)
    angc = 2.0 * np.pi * ((c[:, None] * c[None, :]) % C_GROUP_DIM) / C_GROUP_DIM
    eye = np.eye(C_GROUPS)
    cs = np.concatenate([np.kron(eye, np.cos(angc)), np.kron(eye, np.sin(angc))], axis=1)
    return cmat, nsmat, jnp.asarray(cs * C_GROUP_DIM ** -0.5, BF16)


def _layer_weights(w_in, w_uq, w_ukv, w_out):
    d = w_in.shape[0]
    c_kpe = 3 * A_WIDTH + Q_LORA + KV_LORA
    w_kpe = w_in[:, c_kpe:c_kpe + B_ROPE]
    half = B_ROPE // 2
    w_kpe_rot = jnp.concatenate([-w_kpe[:, half:], w_kpe[:, :half]], axis=1)
    zl = jnp.zeros((d, B_NOPE), w_in.dtype)
    zr = jnp.zeros((d, HEAD_PAD - B_NOPE - B_ROPE), w_in.dtype)
    wm = jnp.concatenate([w_in[:, :c_kpe], zl, w_kpe, zr, zl, w_kpe_rot, zr,
                          w_in[:, c_kpe + B_ROPE:]], axis=1).astype(BF16)
    uq = w_uq.reshape(Q_LORA, B_HEADS, B_NOPE + B_ROPE)
    uq = jnp.pad(uq, ((0, 0), (0, 0), (0, HEAD_PAD - B_NOPE - B_ROPE)))
    wuq_t = uq.reshape(Q_LORA, B_HEADS * HEAD_PAD).T.astype(BF16)
    ukv = w_ukv.reshape(KV_LORA, B_HEADS, B_NOPE + B_V)
    wkn = jnp.pad(ukv[:, :, :B_NOPE], ((0, 0), (0, 0), (0, HEAD_PAD - B_NOPE)))
    wkn = wkn.reshape(KV_LORA, B_HEADS * HEAD_PAD).astype(BF16)
    wv_t = ukv[:, :, B_NOPE:].reshape(KV_LORA, B_WIDTH).T.astype(BF16)
    wo = w_out.astype(BF16)
    return wm, wuq_t, wkn, wv_t, wo[:A_WIDTH], wo[A_WIDTH:A_WIDTH + B_WIDTH], wo[A_WIDTH + B_WIDTH:]


def kernel(x, attn_norm, w_in, q_norm, w_uq, kv_norm, w_ukv, mix_gain, w_out, ffn_norm,
           w_ffn_gate, w_ffn_up, w_ffn_down, w_router, w_exp_gate, w_exp_up, w_exp_down,
           final_norm):
    batch, seq, d = x.shape
    depth = w_in.shape[0]
    t = batch * seq
    assert seq % (max(DILATIONS) * 2 * WIN_SUB) == 0
    tabs = _rope_tables(seq)
    rot = _rotate_half_matrix()
    cmat, nsmat, cs = _dft_constants(seq)
    row = lambda v: v.reshape(1, -1)

    x2 = x.reshape(t, d)
    for l in range(depth):
        wm, wuq_t, wkn, wv_t, wa, wb, wc = _layer_weights(w_in[l], w_uq[l], w_ukv[l], w_out[l])
        qkv_views, (qbt, kb, vbt, xcs, xcs_flip) = _proj_call(
            x2, row(attn_norm[l]), wm, rot, tabs, row(q_norm[l]), wuq_t, row(kv_norm[l]), wkn, wv_t,
            cs, batch, seq)
        win = [_win_call(*(a.reshape(batch, seq // dil, dil * A_WIDTH) for a in qkv), batch, seq, dil)
               for dil, qkv in zip(DILATIONS, qkv_views)]
        ob = _mla_call(qbt, kb, vbt, batch, seq)
        oc = _dft_call(cmat, nsmat, xcs, xcs_flip, batch, seq)
        g = mix_gain[l]
        x2 = _out_call(x2, [w[0] for w in win], [w[1] for w in win], ob, oc,
                       row(g[:A_WIDTH]), row(g[A_WIDTH:A_WIDTH + B_WIDTH]), row(g[A_WIDTH + B_WIDTH:]),
                       wa, wb, wc, batch, seq)
        i = l // 2
        if l % 2 == 0:
            x2 = _ffn_call(x2, row(ffn_norm[l]), w_ffn_gate[i].astype(BF16), w_ffn_up[i].astype(BF16),
                           w_ffn_down[i].astype(BF16))
            if l == depth - 1:
                x2 = _final_norm_call(x2, row(final_norm))
        else:
            idx, gates, h_tiles = _router_call(x2, row(ffn_norm[l]), w_router[i].T)
            bexp, row_tok, row_dst, n_out_rows = _route_plan(idx, t)
            parts = d // LANES
            y_tiles = _expert_call(h_tiles, d, w_exp_gate[i].astype(BF16), w_exp_up[i].astype(BF16),
                                   w_exp_down[i].astype(BF16), bexp, row_tok * parts, row_dst * parts,
                                   n_out_rows)
            x2 = _combine_call(x2, y_tiles, gates.T, row(final_norm), final=(l == depth - 1))
    return x2.reshape(batch, seq, d)
```
